```python
import jax, jax.numpy as jnp
from jax import lax
import numpy as np


D_MODEL = 1024
BATCH = 16
SEQ = 4096
DEPTH = 2
DEC_BATCH = 2
DEC_SEQ = 16384
PAST_LEN = 128

N_BRANCH = 4
BRANCH_W = D_MODEL // 4
N_GROUPS = 4
GROUP_W = BRANCH_W // N_GROUPS
POOL_WINDOWS = (2, 4, 8, 16)
CHUNK = 128
OFF_A = 0
OFF_B = OFF_A + BRANCH_W
OFF_CH = OFF_B + BRANCH_W
OFF_CB = OFF_CH + BRANCH_W
OFF_CC = OFF_CB + BRANCH_W
OFF_DU = OFF_CC + BRANCH_W
OFF_DV = OFF_DU + BRANCH_W
OFF_G = OFF_DV + BRANCH_W
IN_COLS = OFF_G + N_BRANCH * D_MODEL
N_EXPERT_GROUPS = 4
EXPERTS_PER_GROUP = 8
N_EXPERTS = N_EXPERT_GROUPS * EXPERTS_PER_GROUP
TOP_K = 2
D_EXPERT = D_MODEL // 2
MOE_BLOCK = 256
DN_ALPHA = (2 * DEPTH) ** 0.25
DN_BETA = (8 * DEPTH) ** -0.25
LN_EPS = 1e-5

kernel_name = 'hybrid_bidir_gated_merge_hmoe_encoder'


def layer_norm(x, g, b):
    xf = x.astype(jnp.float32)
    mu = jnp.mean(xf, axis=-1, keepdims=True)
    xc = xf - mu
    var = jnp.mean(xc * xc, axis=-1, keepdims=True)
    return (xc * lax.rsqrt(var + LN_EPS) * g + b).astype(x.dtype)


def pool_mixer(za, pool_w, pool_scale):
    bsz, s, _ = za.shape
    zf = za.astype(jnp.float32)
    cs = jnp.concatenate([jnp.zeros((bsz, 1, BRANCH_W), jnp.float32), jnp.cumsum(zf, axis=1)], axis=1)
    t = jnp.arange(s)
    outs = []
    for k, w in enumerate(POOL_WINDOWS):
        lo = jnp.clip(t - w // 2, 0, s)
        hi = jnp.clip(t - w // 2 + w, 0, s)
        sl = slice(k * GROUP_W, (k + 1) * GROUP_W)
        csg = cs[:, :, sl]
        mean = (csg[:, hi] - csg[:, lo]) / (hi - lo).astype(jnp.float32)[None, :, None]
        outs.append(mean - zf[:, :, sl])
    pooled = jnp.stack(outs, axis=2)
    y = jnp.einsum('bsgc,gcd->bsgd', pooled, pool_w.astype(jnp.float32)).reshape(bsz, s, BRANCH_W)
    return (y * pool_scale).astype(za.dtype)


def fourier_mixer(zb):
    bsz, s, _ = zb.shape
    zg = zb.astype(jnp.float32).reshape(bsz, s, N_GROUPS, GROUP_W)
    y = jnp.fft.fft2(zg, axes=(1, 3), norm='ortho').real
    return y.reshape(bsz, s, BRANCH_W).astype(zb.dtype)


def shortconv_mixer(h, gate_b, gate_c, conv_w, conv_b):
    q = gate_c * h
    qp = jnp.pad(q, ((0, 0), (1, 1), (0, 0)))
    c = qp[:, :-2] * conv_w[0] + qp[:, 1:-1] * conv_w[1] + qp[:, 2:] * conv_w[2] + conv_b
    return gate_b * c


def sgu_mixer(u, v, ln_g, ln_b, sgu_w, sgu_b):
    bsz, s, _ = u.shape
    u = jax.nn.gelu(u)
    v = layer_norm(jax.nn.gelu(v), ln_g, ln_b)
    vc = v.reshape(bsz, s // CHUNK, CHUNK, N_GROUPS, GROUP_W)
    sp = jnp.einsum('gpq,bnqgc->bnpgc', sgu_w, vc) + jnp.transpose(sgu_b)[None, None, :, :, None]
    return u * sp.reshape(bsz, s, BRANCH_W)


def hier_moe(x2d, w_rg, b_rg, w_re, b_re, w1, w3, w2):
    n = x2d.shape[0]
    xf = x2d.astype(jnp.float32)
    rows = jnp.arange(n)
    lg = xf @ w_rg.astype(jnp.float32) + b_rg.astype(jnp.float32)
    pg = jax.nn.softmax(lg, axis=-1)
    g_idx = jnp.argmax(lg, axis=-1).astype(jnp.int32)
    p_group = pg[rows, g_idx]
    le = (xf @ w_re.astype(jnp.float32) + b_re.astype(jnp.float32)).reshape(n, N_EXPERT_GROUPS, EXPERTS_PER_GROUP)
    pe = jax.nn.softmax(le[rows, g_idx], axis=-1)
    top_p, top_i = lax.top_k(pe, TOP_K)
    e_idx = g_idx[:, None] * EXPERTS_PER_GROUP + top_i.astype(jnp.int32)
    wts = p_group[:, None] * top_p

    n_slots = n * TOP_K
    e_flat = e_idx.reshape(-1)
    tok_flat = jnp.repeat(jnp.arange(n, dtype=jnp.int32), TOP_K)
    order = jnp.argsort(e_flat)
    e_sorted = e_flat[order]
    tok_sorted = tok_flat[order]
    w_sorted = wts.reshape(-1)[order]
    counts = jnp.bincount(e_flat, length=N_EXPERTS).astype(jnp.int32)
    padded = ((counts + MOE_BLOCK - 1) // MOE_BLOCK) * MOE_BLOCK
    pad_end = jnp.cumsum(padded)
    pad_start = pad_end - padded
    start = jnp.cumsum(counts) - counts
    rank = jnp.arange(n_slots, dtype=jnp.int32) - start[e_sorted]
    dest = pad_start[e_sorted] + rank
    buf_len = ((n_slots + MOE_BLOCK - 1) // MOE_BLOCK + N_EXPERTS) * MOE_BLOCK
    n_blocks = buf_len // MOE_BLOCK
    xb = jnp.zeros((buf_len, D_MODEL), x2d.dtype).at[dest].set(x2d[tok_sorted])
    block_e = jnp.searchsorted(pad_end, jnp.arange(n_blocks, dtype=jnp.int32) * MOE_BLOCK, side='right')
    block_e = jnp.minimum(block_e, N_EXPERTS - 1).astype(jnp.int32)

    def expert_block(args):
        xblk, e = args
        hid = jax.nn.silu(xblk @ w1[e]) * (xblk @ w3[e])
        return hid @ w2[e]

    yb = lax.map(expert_block, (xb.reshape(n_blocks, MOE_BLOCK, D_MODEL), block_e)).reshape(buf_len, D_MODEL)
    y_slots = yb[dest] * w_sorted[:, None].astype(yb.dtype)
    return jax.ops.segment_sum(y_slots, tok_sorted, num_segments=n)


def encoder_layer(x, w_in, b_in, pool_w, pool_scale, conv_w, conv_b, sgu_ln_g, sgu_ln_b, sgu_w, sgu_b,
                  branch_proj, w_out, ln1_g, ln1_b, w_rg, b_rg, w_re, b_re, w1, w3, w2, ln2_g, ln2_b):
    bsz, s, d = x.shape
    z = x @ w_in + b_in
    y_a = pool_mixer(z[..., OFF_A:OFF_B], pool_w, pool_scale)
    y_b = fourier_mixer(z[..., OFF_B:OFF_CH])
    y_c = shortconv_mixer(z[..., OFF_CH:OFF_CB], z[..., OFF_CB:OFF_CC], z[..., OFF_CC:OFF_DU], conv_w, conv_b)
    y_d = sgu_mixer(z[..., OFF_DU:OFF_DV], z[..., OFF_DV:OFF_G], sgu_ln_g, sgu_ln_b, sgu_w, sgu_b)
    gates = jax.nn.sigmoid(z[..., OFF_G:].reshape(bsz, s, N_BRANCH, D_MODEL))
    merged = None
    for k, y_k in enumerate((y_a, y_b, y_c, y_d)):
        term = gates[:, :, k, :] * (y_k @ branch_proj[k])
        merged = term if merged is None else merged + term
    x = layer_norm(DN_ALPHA * x + merged @ w_out, ln1_g, ln1_b)
    moe = hier_moe(x.reshape(bsz * s, d), w_rg, b_rg, w_re, b_re, w1, w3, w2).reshape(bsz, s, d)
    return layer_norm(DN_ALPHA * x + moe, ln2_g, ln2_b)


def run_trunk(x, params):
    for l in range(DEPTH):
        x = encoder_layer(x, *[p[l] for p in params])
    return x


def setup_inputs(seed: int = 0) -> dict:
    key = jax.random.key(seed)
    ks = jax.random.split(key, 26)
    f32 = jnp.float32
    nrm = lambda k, shape: jax.random.normal(k, shape, f32)
    L = DEPTH
    return {
        'x_prompt': nrm(ks[0], (BATCH, SEQ, D_MODEL)),
        'x_sample': nrm(ks[1], (DEC_BATCH, DEC_SEQ, D_MODEL)),
        'w_in': nrm(ks[2], (L, D_MODEL, IN_COLS)) * D_MODEL ** -0.5,
        'b_in': nrm(ks[3], (L, IN_COLS)) * 0.01,
        'pool_w': nrm(ks[4], (L, N_GROUPS, GROUP_W, GROUP_W)) * GROUP_W ** -0.5,
        'pool_scale': 1.0 + 0.01 * nrm(ks[5], (L, BRANCH_W)),
        'conv_w': nrm(ks[6], (L, 3, BRANCH_W)) * 3 ** -0.5,
        'conv_b': nrm(ks[7], (L, BRANCH_W)) * 0.01,
        'sgu_ln_g': 1.0 + 0.01 * nrm(ks[8], (L, BRANCH_W)),
        'sgu_ln_b': nrm(ks[9], (L, BRANCH_W)) * 0.01,
        'sgu_w': nrm(ks[10], (L, N_GROUPS, CHUNK, CHUNK)) * CHUNK ** -0.5,
        'sgu_b': 1.0 + 0.01 * nrm(ks[11], (L, N_GROUPS, CHUNK)),
        'branch_proj': nrm(ks[12], (L, N_BRANCH, BRANCH_W, D_MODEL)) * BRANCH_W ** -0.5 * DN_BETA,
        'w_out': nrm(ks[13], (L, D_MODEL, D_MODEL)) * D_MODEL ** -0.5 * DN_BETA,
        'ln1_g': 1.0 + 0.01 * nrm(ks[14], (L, D_MODEL)),
        'ln1_b': nrm(ks[15], (L, D_MODEL)) * 0.01,
        'w_rg': nrm(ks[16], (L, D_MODEL, N_EXPERT_GROUPS)) * D_MODEL ** -0.5,
        'b_rg': nrm(ks[17], (L, N_EXPERT_GROUPS)) * 0.01,
        'w_re': nrm(ks[18], (L, D_MODEL, N_EXPERTS)) * D_MODEL ** -0.5,
        'b_re': nrm(ks[19], (L, N_EXPERTS)) * 0.01,
        'w1': nrm(ks[20], (L, N_EXPERTS, D_MODEL, D_EXPERT)) * D_MODEL ** -0.5,
        'w3': nrm(ks[21], (L, N_EXPERTS, D_MODEL, D_EXPERT)) * D_MODEL ** -0.5,
        'w2': nrm(ks[22], (L, N_EXPERTS, D_EXPERT, D_MODEL)) * D_EXPERT ** -0.5 * DN_BETA,
        'ln2_g': 1.0 + 0.01 * nrm(ks[23], (L, D_MODEL)),
        'ln2_b': nrm(ks[24], (L, D_MODEL)) * 0.01,
    }


def reference(x_prompt, x_sample, w_in, b_in, pool_w, pool_scale, conv_w, conv_b, sgu_ln_g, sgu_ln_b,
              sgu_w, sgu_b, branch_proj, w_out, ln1_g, ln1_b, w_rg, b_rg, w_re, b_re, w1, w3, w2,
              ln2_g, ln2_b):
    params = (w_in, b_in, pool_w, pool_scale, conv_w, conv_b, sgu_ln_g, sgu_ln_b, sgu_w, sgu_b,
              branch_proj, w_out, ln1_g, ln1_b, w_rg, b_rg, w_re, b_re, w1, w3, w2, ln2_g, ln2_b)
    y_prompt = run_trunk(x_prompt, params)
    y_sample = run_trunk(x_sample, params)
    return (y_prompt, y_sample)
```

```python
import functools
import math

import numpy as np
import jax
import jax.numpy as jnp
from jax import lax
from jax.experimental import pallas as pl
from jax.experimental.pallas import tpu as pltpu

F32 = jnp.float32
BF16 = jnp.bfloat16

D_MODEL = 1024
DEPTH = 2
BRANCH_W = 256
N_GROUPS = 4
GROUP_W = 64
POOL_WINDOWS = (2, 4, 8, 16)
CHUNK = 128
OFF_A = 0
OFF_B = 256
OFF_CH = 512
OFF_DU = 1280
OFF_G = 1792
N_EXPERT_GROUPS = 4
EXPERTS_PER_GROUP = 8
N_EXPERTS = 32
D_EXPERT = 512
MOE_BLOCK = 256
DN_ALPHA = (2 * DEPTH) ** 0.25
LN_EPS = 1e-5

N_CLASS_IDS = 256
N_REACHABLE_CLASSES = N_EXPERT_GROUPS * (EXPERTS_PER_GROUP * (EXPERTS_PER_GROUP - 1) // 2)
META_W = 128
ROW_W = D_MODEL + META_W
HALO = 16
TILE = 256
NEG_BIG = -1e30
VMEM_LIMIT = 56 * 1024 * 1024


def _dot(a, b):
    return jnp.dot(a, b, preferred_element_type=F32)


def _layer_norm(x, g, b):
    mu = jnp.mean(x, axis=-1, keepdims=True)
    xc = x - mu
    var = jnp.mean(xc * xc, axis=-1, keepdims=True)
    return xc * lax.rsqrt(var + LN_EPS) * g + b


def _gelu_tanh(x):
    return 0.5 * x * (1.0 + jnp.tanh(math.sqrt(2.0 / math.pi) * (x + 0.044715 * (x * x * x))))


def _fft_factors(seq):
    n1 = 1 << (int(math.log2(seq)) // 2)
    return n1, seq // n1


@functools.lru_cache(maxsize=None)
def _fft_tables(seq):
    n1, n2 = _fft_factors(seq)
    c = np.arange(GROUP_W)
    ang = 2.0 * np.pi * ((c[:, None] * c[None, :]) % GROUP_W) / GROUP_W
    eye = np.eye(N_GROUPS)
    cs = np.concatenate([np.kron(eye, np.cos(ang)), np.kron(eye, np.sin(ang))], axis=1)
    k1 = np.arange(n1)
    t1 = np.arange(n1)
    t2 = np.arange(n2)
    t = t1[None, None, :] * n2 + t2[:, None, None]
    ang1 = 2.0 * np.pi * ((k1[None, :, None] * t) % seq) / seq
    gc, gs = np.cos(ang1), np.sin(ang1)
    lt = np.concatenate([np.concatenate([gc, -gs], axis=2), np.concatenate([-gs, -gc], axis=2)], axis=1)
    k2 = np.arange(n2)
    ang2 = 2.0 * np.pi * ((k2[:, None] * t2[None, :]) % n2) / n2
    scale = 1.0 / math.sqrt(seq * GROUP_W)
    to16 = lambda a: jnp.asarray(a, dtype=F32).astype(BF16)
    return to16(cs), to16(lt), to16(np.cos(ang2) * scale), to16(np.sin(ang2) * scale)


def _fft_stage1_kernel(x_ref, wb_ref, bb_ref, cs_ref, lt_ref, o_ref, u_ref, *, n1, tb):
    x = x_ref[...].reshape(n1 * tb, D_MODEL).astype(BF16)
    zb = _dot(x, wb_ref[...]) + bb_ref[...]
    u = _dot(zb.astype(BF16), cs_ref[...])
    for c in range(4):
        u_ref[c] = u[:, c * 128:(c + 1) * 128]
    for j in range(tb):
        q = [u_ref[c, pl.ds(j, n1, stride=tb), :] for c in range(4)]
        stacked = jnp.concatenate([jnp.concatenate(q[0:2], axis=1),
                                   jnp.concatenate(q[2:4], axis=1)], axis=0).astype(BF16)
        b = _dot(lt_ref[j], stacked)
        o_ref[:, j * 512:j * 512 + BRANCH_W] = b[:n1].astype(BF16)
        o_ref[:, j * 512 + BRANCH_W:(j + 1) * 512] = b[n1:].astype(BF16)


def _fft_stage2_kernel(b_ref, c2_ref, s2_ref, o_ref, *, kb):
    for k in range(kb):
        slab = b_ref[k]
        y = _dot(c2_ref[...], slab[:, :BRANCH_W]) + _dot(s2_ref[...], slab[:, BRANCH_W:])
        o_ref[:, k * BRANCH_W:(k + 1) * BRANCH_W] = y.astype(BF16)


def _fourier_mixer(x, w_b, b_b, seq):
    n_tok = x.shape[0]
    bsz = n_tok // seq
    n1, n2 = _fft_factors(seq)
    tb = 8
    kb = 8
    cs, lt, c2, s2 = _fft_tables(seq)
    x4 = x.reshape(bsz, n1, n2, D_MODEL)
    bt = pl.pallas_call(
        functools.partial(_fft_stage1_kernel, n1=n1, tb=tb),
        grid=(bsz, n2 // tb),
        in_specs=[
            pl.BlockSpec((None, n1, tb, D_MODEL), lambda b, j: (b, 0, j, 0)),
            pl.BlockSpec((D_MODEL, BRANCH_W), lambda b, j: (0, 0)),
            pl.BlockSpec((1, BRANCH_W), lambda b, j: (0, 0)),
            pl.BlockSpec((BRANCH_W, 2 * BRANCH_W), lambda b, j: (0, 0)),
            pl.BlockSpec((tb, 2 * n1, 2 * n1), lambda b, j: (j, 0, 0)),
        ],
        out_specs=pl.BlockSpec((None, n1, tb * 512), lambda b, j: (b, 0, j)),
        out_shape=jax.ShapeDtypeStruct((bsz, n1, n2 * 512), BF16),
        scratch_shapes=[pltpu.VMEM((4, n1 * tb, 128), F32)],
        compiler_params=pltpu.CompilerParams(
            dimension_semantics=("arbitrary", "arbitrary"), vmem_limit_bytes=VMEM_LIMIT),
        name="fft_stage1",
    )(x4, w_b, b_b, cs, lt)
    bt4 = bt.reshape(bsz, n1, n2, 512)
    y = pl.pallas_call(
        functools.partial(_fft_stage2_kernel, kb=kb),
        grid=(bsz, n1 // kb),
        in_specs=[
            pl.BlockSpec((None, kb, n2, 512), lambda b, i: (b, i, 0, 0)),
            pl.BlockSpec((n2, n2), lambda b, i: (0, 0)),
            pl.BlockSpec((n2, n2), lambda b, i: (0, 0)),
        ],
        out_specs=pl.BlockSpec((None, n2, kb * BRANCH_W), lambda b, i: (b, 0, i)),
        out_shape=jax.ShapeDtypeStruct((bsz, n2, n1 * BRANCH_W), BF16),
        compiler_params=pltpu.CompilerParams(
            dimension_semantics=("arbitrary", "arbitrary"), vmem_limit_bytes=VMEM_LIMIT),
        name="fft_stage2",
    )(bt4, c2, s2)
    return y.reshape(n_tok, BRANCH_W)


def _mixer_kernel(x_ref, xp_ref, xn_ref, yb_ref, cin_ref,
                  win_ref, bin_ref, poolw_ref, pools_ref, convw_ref, convb_ref,
                  lng_ref, lnb_ref, sguw_ref, sgub_ref, bp_ref, wout_ref,
                  ln1g_ref, ln1b_ref, wr_ref, br_ref, tri_ref,
                  x1e_ref, cout_ref, base_ref, *, seq):
    i = pl.program_id(0)
    tile = TILE
    ext = tile + 2 * HALO
    p0 = (i % (seq // tile)) * tile

    @pl.when(i == 0)
    def _():
        base_ref[...] = cin_ref[...]

    x = x_ref[...]
    xc = x.astype(BF16)
    xe = jnp.concatenate([xp_ref[...].astype(BF16), xc, xn_ref[...].astype(BF16)], axis=0)
    pos = p0 - HALO + lax.broadcasted_iota(jnp.int32, (ext, BRANCH_W), 0)
    valid = (pos >= 0) & (pos < seq)
    lane = lax.broadcasted_iota(jnp.int32, (ext, BRANCH_W), 1)
    grp = lane // GROUP_W

    za = jnp.where(valid, _dot(xe, win_ref[:, OFF_A:OFF_A + BRANCH_W]) + bin_ref[:, OFF_A:OFF_A + BRANCH_W], 0.0)
    s2 = za + pltpu.roll(za, 1, 0)
    s4 = s2 + pltpu.roll(s2, 2, 0)
    s8 = s4 + pltpu.roll(s4, 4, 0)
    s16 = s8 + pltpu.roll(s8, 8, 0)
    c4 = pltpu.roll(s4, ext - 1, 0)
    c8 = pltpu.roll(s8, ext - 3, 0)
    c16 = pltpu.roll(s16, ext - 7, 0)
    wsum = jnp.where(grp == 0, s2, jnp.where(grp == 1, c4, jnp.where(grp == 2, c8, c16)))
    half = jnp.where(grp == 0, 1, jnp.where(grp == 1, 2, jnp.where(grp == 2, 4, 8)))
    cnt = jnp.minimum(pos + half, seq) - jnp.maximum(pos - half, 0)
    pooled = (wsum / jnp.maximum(cnt, 1).astype(F32) - za)[HALO:HALO + tile]
    y_a = _dot(pooled.astype(BF16), poolw_ref[...]) * pools_ref[...]

    zc = _dot(xe, win_ref[:, OFF_CH:OFF_DU]) + bin_ref[:, OFF_CH:OFF_DU]
    h = zc[:, 0:BRANCH_W]
    gate_b = zc[:, BRANCH_W:2 * BRANCH_W]
    gate_c = zc[:, 2 * BRANCH_W:3 * BRANCH_W]
    q = jnp.where(valid, gate_c * h, 0.0)
    q_prev = pltpu.roll(q, 1, 0)[HALO:HALO + tile]
    q_next = pltpu.roll(q, ext - 1, 0)[HALO:HALO + tile]
    conv = (q_prev * convw_ref[0:1, :] + q[HALO:HALO + tile] * convw_ref[1:2, :]
            + q_next * convw_ref[2:3, :] + convb_ref[...])
    y_c = gate_b[HALO:HALO + tile] * conv

    zd = _dot(xc, win_ref[:, OFF_DU:OFF_G]) + bin_ref[:, OFF_DU:OFF_G]
    u = _gelu_tanh(zd[:, 0:BRANCH_W])
    v = _layer_norm(_gelu_tanh(zd[:, BRANCH_W:]), lng_ref[...], lnb_ref[...]).astype(BF16)
    grp_c = lax.broadcasted_iota(jnp.int32, (CHUNK, BRANCH_W), 1) // GROUP_W
    sps = []
    for c in range(tile // CHUNK):
        vch = v[c * CHUNK:(c + 1) * CHUNK]
        sp = None
        for g in range(N_GROUPS):
            r = _dot(sguw_ref[g], vch)
            sp = r if sp is None else jnp.where(grp_c == g, r, sp)
        sps.append(sp + sgub_ref[...])
    y_d = u * jnp.concatenate(sps, axis=0)

    merged = None
    for k, y_k in enumerate((y_a, yb_ref[...], y_c, y_d)):
        proj = _dot(y_k.astype(BF16), bp_ref[k])
        lo = OFF_G + k * D_MODEL
        gz = _dot(xc, win_ref[:, lo:lo + D_MODEL]) + bin_ref[:, lo:lo + D_MODEL]
        term = (0.5 * jnp.tanh(0.5 * gz) + 0.5) * proj
        merged = term if merged is None else merged + term
    x1 = _layer_norm(DN_ALPHA * x + _dot(merged.astype(BF16), wout_ref[...]), ln1g_ref[...], ln1b_ref[...])

    xh = x1.astype(BF16)
    xl = (x1 - xh.astype(F32)).astype(BF16)
    wr = wr_ref[...]
    wh = wr.astype(BF16)
    wl = (wr - wh.astype(F32)).astype(BF16)
    logits = _dot(xh, wh) + _dot(xl, wh) + _dot(xh, wl) + br_ref[...]
    ln = lax.broadcasted_iota(jnp.int32, (tile, META_W), 1)
    first8 = ln < EXPERTS_PER_GROUP
    lg = jnp.where(first8, logits, NEG_BIG)
    m = jnp.max(lg, axis=-1, keepdims=True)
    g_idx = jnp.min(jnp.where(lg == m, ln, META_W), axis=-1, keepdims=True)
    p_group = 1.0 / jnp.sum(jnp.exp(lg - m), axis=-1, keepdims=True)
    sel = jnp.zeros_like(logits)
    for g in range(N_EXPERT_GROUPS):
        cand = pltpu.roll(logits, META_W - EXPERTS_PER_GROUP * (g + 1), 1)
        sel = jnp.where(g_idx == g, cand, sel)
    le = jnp.where(first8, sel, NEG_BIG)
    ex = jnp.exp(le - jnp.max(le, axis=-1, keepdims=True))
    pe = jnp.where(first8, ex / jnp.sum(ex, axis=-1, keepdims=True), -1.0)
    p1 = jnp.max(pe, axis=-1, keepdims=True)
    i1 = jnp.min(jnp.where(pe == p1, ln, META_W), axis=-1, keepdims=True)
    pe2 = jnp.where(ln == i1, -1.0, pe)
    p2 = jnp.max(pe2, axis=-1, keepdims=True)
    i2 = jnp.min(jnp.where(pe2 == p2, ln, META_W), axis=-1, keepdims=True)
    first_lo = i1 < i2
    w_lo = p_group * jnp.where(first_lo, p1, p2)
    w_hi = p_group * jnp.where(first_lo, p2, p1)
    cls = g_idx * 64 + jnp.minimum(i1, i2) * EXPERTS_PER_GROUP + jnp.maximum(i1, i2)

    onehot = lax.broadcasted_iota(jnp.int32, (tile, N_CLASS_IDS), 1) == cls
    ohf = onehot.astype(F32)
    before = _dot(tri_ref[...], onehot.astype(BF16))
    base = base_ref[0:1, :]
    rank = jnp.sum(ohf * (before + base), axis=-1, keepdims=True)
    new_base = jnp.broadcast_to(base + jnp.sum(ohf, axis=0, keepdims=True), (8, N_CLASS_IDS))
    base_ref[...] = new_base
    cout_ref[...] = new_base

    meta = jnp.where(ln == 0, w_lo, jnp.where(ln == 1, w_hi,
                     jnp.where(ln == 2, cls.astype(F32), jnp.where(ln == 3, rank, 0.0))))
    x1e_ref[:, 0:D_MODEL] = x1
    x1e_ref[:, D_MODEL:ROW_W] = meta


def _const_spec(shape):
    nd = len(shape)
    return pl.BlockSpec(shape, lambda i, _nd=nd: (0,) * _nd, pipeline_mode=pl.Buffered(1))


def _mixer(x, y_b, counts_in, lw, seq):
    n_tok = x.shape[0]
    n_tiles = n_tok // TILE
    hb = TILE // HALO
    n_hblk = n_tok // HALO
    consts = [lw["w_in"], lw["b_in"], lw["pool_w"], lw["pool_scale"], lw["conv_w"], lw["conv_b"],
              lw["sgu_ln_g"], lw["sgu_ln_b"], lw["sgu_w"], lw["sgu_b"], lw["branch_proj"], lw["w_out"],
              lw["ln1_g"], lw["ln1_b"], lw["w_r"], lw["b_r"], lw["tri"]]
    return pl.pallas_call(
        functools.partial(_mixer_kernel, seq=seq),
        grid=(n_tiles,),
        in_specs=[
            pl.BlockSpec((TILE, D_MODEL), lambda i: (i, 0)),
            pl.BlockSpec((HALO, D_MODEL), lambda i: (jnp.maximum(i * hb - 1, 0), 0)),
            pl.BlockSpec((HALO, D_MODEL), lambda i: (jnp.minimum((i + 1) * hb, n_hblk - 1), 0)),
            pl.BlockSpec((TILE, BRANCH_W), lambda i: (i, 0)),
            pl.BlockSpec((8, N_CLASS_IDS), lambda i: (0, 0)),
        ] + [_const_spec(c.shape) for c in consts],
        out_specs=[
            pl.BlockSpec((TILE, ROW_W), lambda i: (i, 0)),
            pl.BlockSpec((8, N_CLASS_IDS), lambda i: (0, 0)),
        ],
        out_shape=[
            jax.ShapeDtypeStruct((n_tok, ROW_W), F32),
            jax.ShapeDtypeStruct((8, N_CLASS_IDS), F32),
        ],
        scratch_shapes=[pltpu.VMEM((8, N_CLASS_IDS), F32)],
        compiler_params=pltpu.CompilerParams(
            dimension_semantics=("arbitrary",), vmem_limit_bytes=VMEM_LIMIT),
        name="mixer",
    )(x, x, x, y_b, counts_in, *consts)


def _cumsum_sublanes(a):
    n = a.shape[0]
    row = lax.broadcasted_iota(jnp.int32, a.shape, 0)
    s = 1
    while s < n:
        a = a + jnp.where(row >= s, pltpu.roll(a, s, 0), 0.0)
        s *= 2
    return a


def _cumsum_lanes(a):
    n = a.shape[1]
    col = lax.broadcasted_iota(jnp.int32, a.shape, 1)
    s = 1
    while s < n:
        a = a + jnp.where(col >= s, pltpu.roll(a, s, 1), 0.0)
        s *= 2
    return a


def _tables_kernel(cnt_ref, tab_ref, start_ref, *, nbp):
    cnt = cnt_ref[...]
    nb_row = jnp.floor((cnt + (MOE_BLOCK - 1)) * (1.0 / MOE_BLOCK))
    start_ref[...] = (_cumsum_lanes(nb_row) - nb_row) * MOE_BLOCK

    cnt_col = jnp.broadcast_to(cnt[0:1, :], (N_CLASS_IDS, N_CLASS_IDS)).T
    nb_col = jnp.floor((cnt_col + (MOE_BLOCK - 1)) * (1.0 / MOE_BLOCK))
    end_col = _cumsum_sublanes(nb_col)
    reps = nbp // N_CLASS_IDS
    end_t = jnp.concatenate([end_col] * reps, axis=1)
    j = lax.broadcasted_iota(jnp.int32, (N_CLASS_IDS, nbp), 1).astype(F32)
    blk_cls = jnp.sum((end_t <= j).astype(F32), axis=0, keepdims=True)
    total = end_t[N_CLASS_IDS - 1:N_CLASS_IDS, :]
    jr = j[0:1, :]
    active = jr < total
    last_cls = jnp.max(jnp.where(active, blk_cls, 0.0), axis=-1, keepdims=True)
    cls_i = jnp.where(active, blk_cls, last_cls).astype(jnp.int32)
    g8 = (cls_i >> 6) * EXPERTS_PER_GROUP
    e_lo = g8 + ((cls_i >> 3) & 7)
    e_hi = g8 + (cls_i & 7)
    bidx = jnp.minimum(jr, total - 1.0).astype(jnp.int32)
    zero = jnp.zeros_like(bidx)
    tab_ref[...] = jnp.concatenate([e_lo, e_hi, bidx, total.astype(jnp.int32), zero, zero, zero, zero], axis=0)


def _tables(counts, nbp):
    return pl.pallas_call(
        functools.partial(_tables_kernel, nbp=nbp),
        out_shape=[jax.ShapeDtypeStruct((8, nbp), jnp.int32),
                   jax.ShapeDtypeStruct((8, N_CLASS_IDS), F32)],
        name="moe_tables",
    )(counts)


def _dest_kernel(meta_ref, start_ref, dest_ref):
    meta = meta_ref[...]
    cls = meta[:, 2:3].astype(jnp.int32)
    rank = meta[:, 3:4]
    onehot = lax.broadcasted_iota(jnp.int32, (TILE, N_CLASS_IDS), 1) == cls
    start = jnp.sum(jnp.where(onehot, start_ref[0:1, :], 0.0), axis=-1, keepdims=True)
    dest_col = jnp.broadcast_to(start + rank, (TILE, META_W))
    dest_ref[...] = dest_col.T[0:1, :].astype(jnp.int32)


def _dest(x1e, start):
    n_tok = x1e.shape[0]
    n_tiles = n_tok // TILE
    return pl.pallas_call(
        _dest_kernel,
        grid=(n_tiles,),
        in_specs=[pl.BlockSpec((TILE, META_W), lambda i: (i, D_MODEL // META_W)),
                  pl.BlockSpec((8, N_CLASS_IDS), lambda i: (0, 0))],
        out_specs=pl.BlockSpec((None, 1, TILE), lambda i: (i, 0, 0)),
        out_shape=jax.ShapeDtypeStruct((n_tiles, 1, TILE), jnp.int32),
        compiler_params=pltpu.CompilerParams(dimension_semantics=("arbitrary",)),
        name="moe_dest",
    )(x1e, start)


def _dispatch_kernel(dest_ref, x_ref, xb_in_ref, xb_ref, sem):
    del xb_in_ref

    def row_copy(r, d):
        return pltpu.make_async_copy(x_ref.at[pl.ds(r, 1), :], xb_ref.at[pl.ds(d, 1), :], sem)

    def issue(r, c):
        row_copy(r, dest_ref[0, r]).start()
        return c

    lax.fori_loop(0, TILE, issue, 0)

    def drain(r, c):
        row_copy(r, dest_ref[0, r]).wait()
        return c

    lax.fori_loop(0, TILE, drain, 0)


def _dispatch(dest, x1e, xb):
    n_tiles = x1e.shape[0] // TILE
    return pl.pallas_call(
        _dispatch_kernel,
        grid=(n_tiles,),
        in_specs=[pl.BlockSpec((None, 1, TILE), lambda i: (i, 0, 0), memory_space=pltpu.SMEM),
                  pl.BlockSpec((TILE, ROW_W), lambda i: (i, 0)),
                  pl.BlockSpec(memory_space=pl.ANY)],
        out_specs=pl.BlockSpec(memory_space=pl.ANY),
        out_shape=jax.ShapeDtypeStruct(xb.shape, xb.dtype),
        scratch_shapes=[pltpu.SemaphoreType.DMA],
        input_output_aliases={2: 0},
        compiler_params=pltpu.CompilerParams(dimension_semantics=("arbitrary",), has_side_effects=True),
        name="moe_dispatch",
    )(dest, x1e, xb)


def _gather_kernel(dest_ref, ys_ref, o_ref, sem):
    def row_copy(r, d):
        return pltpu.make_async_copy(ys_ref.at[pl.ds(d, 1), :], o_ref.at[pl.ds(r, 1), :], sem)

    def issue(r, c):
        row_copy(r, dest_ref[0, r]).start()
        return c

    lax.fori_loop(0, TILE, issue, 0)

    def drain(r, c):
        row_copy(r, dest_ref[0, r]).wait()
        return c

    lax.fori_loop(0, TILE, drain, 0)


def _gather(dest, ys, n_tok):
    n_tiles = n_tok // TILE
    return pl.pallas_call(
        _gather_kernel,
        grid=(n_tiles,),
        in_specs=[pl.BlockSpec((None, 1, TILE), lambda i: (i, 0, 0), memory_space=pltpu.SMEM),
                  pl.BlockSpec(memory_space=pl.ANY)],
        out_specs=pl.BlockSpec((TILE, D_MODEL), lambda i: (i, 0)),
        out_shape=jax.ShapeDtypeStruct((n_tok, D_MODEL), F32),
        scratch_shapes=[pltpu.SemaphoreType.DMA],
        compiler_params=pltpu.CompilerParams(dimension_semantics=("arbitrary",)),
        name="moe_gather",
    )(dest, ys)


def _expert_kernel(elo_ref, ehi_ref, bidx_ref, tot_ref,
                   xb_ref, w1lo_ref, w3lo_ref, w2lo_ref, w1hi_ref, w3hi_ref, w2hi_ref,
                   g_ref, b_ref, o_ref):
    del elo_ref, ehi_ref, bidx_ref
    active = pl.program_id(0) < tot_ref[0]

    @pl.when(jnp.logical_not(active))
    def _():
        o_ref[...] = jnp.zeros_like(o_ref)

    @pl.when(active)
    def _():
        x = xb_ref[:, 0:D_MODEL]
        xh = x.astype(BF16)

        def expert(w1_ref, w3_ref, w2_ref):
            a = _dot(xh, w1_ref[...])
            hid = (a * jax.nn.sigmoid(a)) * _dot(xh, w3_ref[...])
            return _dot(hid.astype(BF16), w2_ref[...])

        w_lo = xb_ref[:, D_MODEL:D_MODEL + 1]
        w_hi = xb_ref[:, D_MODEL + 1:D_MODEL + 2]
        moe = w_lo * expert(w1lo_ref, w3lo_ref, w2lo_ref) + w_hi * expert(w1hi_ref, w3hi_ref, w2hi_ref)
        o_ref[...] = _layer_norm(DN_ALPHA * x + moe, g_ref[...], b_ref[...])


def _experts(tab, xb, lw):
    n_rows = xb.shape[0]
    up = lambda sel: pl.BlockSpec((None, D_MODEL, D_EXPERT), lambda j, lo, hi, bi, tot: ((lo, hi)[sel][j], 0, 0))
    down = lambda sel: pl.BlockSpec((None, D_EXPERT, D_MODEL), lambda j, lo, hi, bi, tot: ((lo, hi)[sel][j], 0, 0))
    vec = pl.BlockSpec((1, D_MODEL), lambda j, lo, hi, bi, tot: (0, 0))
    grid_spec = pltpu.PrefetchScalarGridSpec(
        num_scalar_prefetch=4,
        grid=(n_rows // MOE_BLOCK,),
        in_specs=[pl.BlockSpec((MOE_BLOCK, ROW_W), lambda j, lo, hi, bi, tot: (bi[j], 0)),
                  up(0), up(0), down(0), up(1), up(1), down(1), vec, vec],
        out_specs=pl.BlockSpec((MOE_BLOCK, D_MODEL), lambda j, lo, hi, bi, tot: (j, 0)),
    )
    return pl.pallas_call(
        _expert_kernel,
        grid_spec=grid_spec,
        out_shape=jax.ShapeDtypeStruct((n_rows, D_MODEL), F32),
        compiler_params=pltpu.CompilerParams(
            dimension_semantics=("arbitrary",), vmem_limit_bytes=VMEM_LIMIT),
        name="moe_experts",
    )(tab[0], tab[1], tab[2], tab[3, 0:1], xb,
      lw["w1"], lw["w3"], lw["w2"], lw["w1"], lw["w3"], lw["w2"], lw["ln2_g"], lw["ln2_b"])


def _layer_weights(p, l):
    row = lambda a: a[l].reshape(1, -1).astype(F32)
    w_r = jnp.zeros((D_MODEL, META_W), F32)
    w_r = w_r.at[:, 0:N_EXPERT_GROUPS].set(p["w_rg"][l]).at[:, 8:8 + N_EXPERTS].set(p["w_re"][l])
    b_r = jnp.zeros((1, META_W), F32)
    b_r = b_r.at[0, 0:N_EXPERT_GROUPS].set(p["b_rg"][l]).at[0, N_EXPERT_GROUPS:8].set(NEG_BIG)
    b_r = b_r.at[0, 8:8 + N_EXPERTS].set(p["b_re"][l])
    t = np.arange(TILE)
    return {
        "w_in": p["w_in"][l].astype(BF16),
        "b_in": row(p["b_in"]),
        "w_b": p["w_in"][l][:, OFF_B:OFF_B + BRANCH_W].astype(BF16),
        "b_b": p["b_in"][l][OFF_B:OFF_B + BRANCH_W].reshape(1, -1),
        "pool_w": jax.scipy.linalg.block_diag(*[p["pool_w"][l][g] for g in range(N_GROUPS)]).astype(BF16),
        "pool_scale": row(p["pool_scale"]),
        "conv_w": p["conv_w"][l],
        "conv_b": row(p["conv_b"]),
        "sgu_ln_g": row(p["sgu_ln_g"]),
        "sgu_ln_b": row(p["sgu_ln_b"]),
        "sgu_w": p["sgu_w"][l].astype(BF16),
        "sgu_b": jnp.repeat(p["sgu_b"][l].T, GROUP_W, axis=1),
        "branch_proj": p["branch_proj"][l].astype(BF16),
        "w_out": p["w_out"][l].astype(BF16),
        "ln1_g": row(p["ln1_g"]),
        "ln1_b": row(p["ln1_b"]),
        "w_r": w_r,
        "b_r": b_r,
        "tri": jnp.asarray((t[None, :] < t[:, None]).astype(np.float32)).astype(BF16),
        "w1": p["w1"][l].astype(BF16),
        "w3": p["w3"][l].astype(BF16),
        "w2": p["w2"][l].astype(BF16),
        "ln2_g": row(p["ln2_g"]),
        "ln2_b": row(p["ln2_b"]),
    }


def _encoder_layer(xs, seqs, lw):
    n_total = sum(x.shape[0] for x in xs)
    n_blocks = n_total // MOE_BLOCK + N_REACHABLE_CLASSES
    nbp = -(-n_blocks // N_CLASS_IDS) * N_CLASS_IDS
    counts = jnp.zeros((8, N_CLASS_IDS), F32)
    x1es = []
    for x, seq in zip(xs, seqs):
        y_b = _fourier_mixer(x, lw["w_b"], lw["b_b"], seq)
        x1e, counts = _mixer(x, y_b, counts, lw, seq)
        x1es.append(x1e)
    tab, start = _tables(counts, nbp)
    dests = [_dest(x1e, start) for x1e in x1es]
    xb = jnp.zeros((n_blocks * MOE_BLOCK, ROW_W), F32)
    for dest, x1e in zip(dests, x1es):
        xb = _dispatch(dest, x1e, xb)
    ys = _experts(tab, xb, lw)
    return [_gather(dest, ys, x.shape[0]) for dest, x in zip(dests, xs)]


def kernel(x_prompt, x_sample, w_in, b_in, pool_w, pool_scale, conv_w, conv_b, sgu_ln_g, sgu_ln_b,
           sgu_w, sgu_b, branch_proj, w_out, ln1_g, ln1_b, w_rg, b_rg, w_re, b_re, w1, w3, w2,
           ln2_g, ln2_b):
    p = dict(w_in=w_in, b_in=b_in, pool_w=pool_w, pool_scale=pool_scale, conv_w=conv_w, conv_b=conv_b,
             sgu_ln_g=sgu_ln_g, sgu_ln_b=sgu_ln_b, sgu_w=sgu_w, sgu_b=sgu_b, branch_proj=branch_proj,
             w_out=w_out, ln1_g=ln1_g, ln1_b=ln1_b, w_rg=w_rg, b_rg=b_rg, w_re=w_re, b_re=b_re,
             w1=w1, w3=w3, w2=w2, ln2_g=ln2_g, ln2_b=ln2_b)
    shapes = (x_prompt.shape, x_sample.shape)
    seqs = [s[1] for s in shapes]
    xs = [x_prompt.reshape(-1, D_MODEL), x_sample.reshape(-1, D_MODEL)]
    for l in range(w_in.shape[0]):
        xs = _encoder_layer(xs, seqs, _layer_weights(p, l))
    return tuple(x.reshape(s) for x, s in zip(xs, shapes))
```

```python
import functools
import math

import numpy as np
import jax
import jax.numpy as jnp
from jax import lax
from jax.experimental import pallas as pl
from jax.experimental.pallas import tpu as pltpu

F32 = jnp.float32
BF16 = jnp.bfloat16

D_MODEL = 1024
DEPTH = 2
BRANCH_W = 256
N_GROUPS = 4
GROUP_W = 64
POOL_WINDOWS = (2, 4, 8, 16)
CHUNK = 128
OFF_A = 0
OFF_B = 256
OFF_CH = 512
OFF_DU = 1280
OFF_G = 1792
N_EXPERT_GROUPS = 4
EXPERTS_PER_GROUP = 8
N_EXPERTS = 32
D_EXPERT = 512
MOE_BLOCK = 256
DN_ALPHA = (2 * DEPTH) ** 0.25
LN_EPS = 1e-5

N_CLASS_IDS = 256
N_REACHABLE_CLASSES = N_EXPERT_GROUPS * (EXPERTS_PER_GROUP * (EXPERTS_PER_GROUP - 1) // 2)
META_W = 128
ROW_W = D_MODEL + META_W
COUNT_SHAPE = (N_CLASS_IDS, META_W)
HALO = 16
TILE = 512
ROW_DMA_UNROLL = 16
NEG_BIG = -1e30
VMEM_LIMIT = 56 * 1024 * 1024


def _dot(a, b):
    return jnp.dot(a, b, preferred_element_type=F32)


def _layer_norm(x, g, b):
    mu = jnp.mean(x, axis=-1, keepdims=True)
    xc = x - mu
    var = jnp.mean(xc * xc, axis=-1, keepdims=True)
    return xc * lax.rsqrt(var + LN_EPS) * g + b


def _gelu_tanh(x):
    return 0.5 * x * (1.0 + jnp.tanh(math.sqrt(2.0 / math.pi) * (x + 0.044715 * (x * x * x))))


def _fft_factors(seq):
    n1 = 1 << (int(math.log2(seq)) // 2)
    return n1, seq // n1


@functools.lru_cache(maxsize=None)
def _fft_tables(seq):
    n1, n2 = _fft_factors(seq)
    c = np.arange(GROUP_W)
    ang = 2.0 * np.pi * ((c[:, None] * c[None, :]) % GROUP_W) / GROUP_W
    eye = np.eye(N_GROUPS)
    cs = np.concatenate([np.kron(eye, np.cos(ang)), np.kron(eye, np.sin(ang))], axis=1)
    k1 = np.arange(n1)
    t1 = np.arange(n1)
    t2 = np.arange(n2)
    t = t1[None, None, :] * n2 + t2[:, None, None]
    ang1 = 2.0 * np.pi * ((k1[None, :, None] * t) % seq) / seq
    gc, gs = np.cos(ang1), np.sin(ang1)
    lt = np.concatenate([np.concatenate([gc, -gs], axis=2), np.concatenate([-gs, -gc], axis=2)], axis=1)
    k2 = np.arange(n2)
    ang2 = 2.0 * np.pi * ((k2[:, None] * t2[None, :]) % n2) / n2
    scale = 1.0 / math.sqrt(seq * GROUP_W)
    to16 = lambda a: jnp.asarray(a, dtype=F32).astype(BF16)
    return to16(cs), to16(lt), to16(np.cos(ang2) * scale), to16(np.sin(ang2) * scale)


def _fft_stage1_kernel(x_ref, wb_ref, bb_ref, cs_ref, lt_ref, o_ref, u_ref, *, n1, tb):
    x = x_ref[...].reshape(n1 * tb, D_MODEL).astype(BF16)
    zb = _dot(x, wb_ref[...]) + bb_ref[...]
    u = _dot(zb.astype(BF16), cs_ref[...])
    for c in range(4):
        u_ref[c] = u[:, c * 128:(c + 1) * 128]
    for j in range(tb):
        q = [u_ref[c, pl.ds(j, n1, stride=tb), :] for c in range(4)]
        stacked = jnp.concatenate([jnp.concatenate(q[0:2], axis=1),
                                   jnp.concatenate(q[2:4], axis=1)], axis=0).astype(BF16)
        b = _dot(lt_ref[j], stacked)
        o_ref[:, j * 512:j * 512 + BRANCH_W] = b[:n1].astype(BF16)
        o_ref[:, j * 512 + BRANCH_W:(j + 1) * 512] = b[n1:].astype(BF16)


def _fft_stage2_kernel(b_ref, c2_ref, s2_ref, o_ref, *, kb):
    for k in range(kb):
        slab = b_ref[k]
        y = _dot(c2_ref[...], slab[:, :BRANCH_W]) + _dot(s2_ref[...], slab[:, BRANCH_W:])
        o_ref[:, k * BRANCH_W:(k + 1) * BRANCH_W] = y.astype(BF16)


def _fourier_mixer(x, w_b, b_b, seq):
    n_tok = x.shape[0]
    bsz = n_tok // seq
    n1, n2 = _fft_factors(seq)
    tb = 8
    kb = 8
    cs, lt, c2, s2 = _fft_tables(seq)
    x4 = x.reshape(bsz, n1, n2, D_MODEL)
    bt = pl.pallas_call(
        functools.partial(_fft_stage1_kernel, n1=n1, tb=tb),
        grid=(bsz, n2 // tb),
        in_specs=[
            pl.BlockSpec((None, n1, tb, D_MODEL), lambda b, j: (b, 0, j, 0)),
            pl.BlockSpec((D_MODEL, BRANCH_W), lambda b, j: (0, 0)),
            pl.BlockSpec((1, BRANCH_W), lambda b, j: (0, 0)),
            pl.BlockSpec((BRANCH_W, 2 * BRANCH_W), lambda b, j: (0, 0)),
            pl.BlockSpec((tb, 2 * n1, 2 * n1), lambda b, j: (j, 0, 0)),
        ],
        out_specs=pl.BlockSpec((None, n1, tb * 512), lambda b, j: (b, 0, j)),
        out_shape=jax.ShapeDtypeStruct((bsz, n1, n2 * 512), BF16),
        scratch_shapes=[pltpu.VMEM((4, n1 * tb, 128), F32)],
        compiler_params=pltpu.CompilerParams(
            dimension_semantics=("arbitrary", "arbitrary"), vmem_limit_bytes=VMEM_LIMIT),
        name="fft_stage1",
    )(x4, w_b, b_b, cs, lt)
    bt4 = bt.reshape(bsz, n1, n2, 512)
    y = pl.pallas_call(
        functools.partial(_fft_stage2_kernel, kb=kb),
        grid=(bsz, n1 // kb),
        in_specs=[
            pl.BlockSpec((None, kb, n2, 512), lambda b, i: (b, i, 0, 0)),
            pl.BlockSpec((n2, n2), lambda b, i: (0, 0)),
            pl.BlockSpec((n2, n2), lambda b, i: (0, 0)),
        ],
        out_specs=pl.BlockSpec((None, n2, kb * BRANCH_W), lambda b, i: (b, 0, i)),
        out_shape=jax.ShapeDtypeStruct((bsz, n2, n1 * BRANCH_W), BF16),
        compiler_params=pltpu.CompilerParams(
            dimension_semantics=("arbitrary", "arbitrary"), vmem_limit_bytes=VMEM_LIMIT),
        name="fft_stage2",
    )(bt4, c2, s2)
    return y.reshape(n_tok, BRANCH_W)


def _mixer_kernel(x_ref, xp_ref, xn_ref, yb_ref, cin_ref,
                  win_ref, bin_ref, poolw_ref, pools_ref, convw_ref, convb_ref,
                  lng_ref, lnb_ref, sguw_ref, sgub_ref, bp_ref, wout_ref,
                  ln1g_ref, ln1b_ref, wr_ref, br_ref, tri_ref,
                  x1e_ref, cout_ref, base_ref, *, seq):
    i = pl.program_id(0)
    tile = TILE
    ext = tile + 2 * HALO
    p0 = (i % (seq // tile)) * tile

    @pl.when(i == 0)
    def _():
        base_ref[...] = cin_ref[...]

    x = x_ref[...]
    xc = x.astype(BF16)
    xe = jnp.concatenate([xp_ref[...].astype(BF16), xc, xn_ref[...].astype(BF16)], axis=0)
    pos = p0 - HALO + lax.broadcasted_iota(jnp.int32, (ext, BRANCH_W), 0)
    valid = (pos >= 0) & (pos < seq)
    lane = lax.broadcasted_iota(jnp.int32, (ext, BRANCH_W), 1)
    grp = lane // GROUP_W

    za = jnp.where(valid, _dot(xe, win_ref[:, OFF_A:OFF_A + BRANCH_W]) + bin_ref[:, OFF_A:OFF_A + BRANCH_W], 0.0)
    s2 = za + pltpu.roll(za, 1, 0)
    s4 = s2 + pltpu.roll(s2, 2, 0)
    s8 = s4 + pltpu.roll(s4, 4, 0)
    s16 = s8 + pltpu.roll(s8, 8, 0)
    c4 = pltpu.roll(s4, ext - 1, 0)
    c8 = pltpu.roll(s8, ext - 3, 0)
    c16 = pltpu.roll(s16, ext - 7, 0)
    wsum = jnp.where(grp == 0, s2, jnp.where(grp == 1, c4, jnp.where(grp == 2, c8, c16)))
    half = jnp.where(grp == 0, 1, jnp.where(grp == 1, 2, jnp.where(grp == 2, 4, 8)))
    cnt = jnp.minimum(pos + half, seq) - jnp.maximum(pos - half, 0)
    pooled = (wsum / jnp.maximum(cnt, 1).astype(F32) - za)[HALO:HALO + tile]
    y_a = _dot(pooled.astype(BF16), poolw_ref[...]) * pools_ref[...]

    zc = _dot(xe, win_ref[:, OFF_CH:OFF_DU]) + bin_ref[:, OFF_CH:OFF_DU]
    h = zc[:, 0:BRANCH_W]
    gate_b = zc[:, BRANCH_W:2 * BRANCH_W]
    gate_c = zc[:, 2 * BRANCH_W:3 * BRANCH_W]
    q = jnp.where(valid, gate_c * h, 0.0)
    q_prev = pltpu.roll(q, 1, 0)[HALO:HALO + tile]
    q_next = pltpu.roll(q, ext - 1, 0)[HALO:HALO + tile]
    conv = (q_prev * convw_ref[0:1, :] + q[HALO:HALO + tile] * convw_ref[1:2, :]
            + q_next * convw_ref[2:3, :] + convb_ref[...])
    y_c = gate_b[HALO:HALO + tile] * conv

    zd = _dot(xc, win_ref[:, OFF_DU:OFF_G]) + bin_ref[:, OFF_DU:OFF_G]
    u = _gelu_tanh(zd[:, 0:BRANCH_W])
    v = _layer_norm(_gelu_tanh(zd[:, BRANCH_W:]), lng_ref[...], lnb_ref[...]).astype(BF16)
    grp_c = lax.broadcasted_iota(jnp.int32, (CHUNK, BRANCH_W), 1) // GROUP_W
    sps = []
    for c in range(tile // CHUNK):
        vch = v[c * CHUNK:(c + 1) * CHUNK]
        sp = None
        for g in range(N_GROUPS):
            r = _dot(sguw_ref[g], vch)
            sp = r if sp is None else jnp.where(grp_c == g, r, sp)
        sps.append(sp + sgub_ref[...])
    y_d = u * jnp.concatenate(sps, axis=0)

    merged = None
    for k, y_k in enumerate((y_a, yb_ref[...], y_c, y_d)):
        half_proj = _dot(y_k.astype(BF16), bp_ref[k])
        lo = OFF_G + k * D_MODEL
        half_gz = _dot(xc, win_ref[:, lo:lo + D_MODEL]) + bin_ref[:, lo:lo + D_MODEL]
        term = (jnp.tanh(half_gz) + 1.0) * half_proj
        merged = term if merged is None else merged + term
    x1 = _layer_norm(DN_ALPHA * x + _dot(merged.astype(BF16), wout_ref[...]), ln1g_ref[...], ln1b_ref[...])

    xh = x1.astype(BF16)
    xl = (x1 - xh.astype(F32)).astype(BF16)
    wr = wr_ref[...]
    wh = wr.astype(BF16)
    wl = (wr - wh.astype(F32)).astype(BF16)
    nt = (((1,), (1,)), ((), ()))
    dot_nt = lambda a, b: lax.dot_general(a, b, nt, preferred_element_type=F32)
    logits = dot_nt(wh, xh) + dot_nt(wh, xl) + dot_nt(wl, xh) + br_ref[...]
    row8 = lax.broadcasted_iota(jnp.int32, (EXPERTS_PER_GROUP, tile), 0)
    lg = logits[0:8]
    m = jnp.max(lg, axis=0, keepdims=True)
    g_idx = jnp.min(jnp.where(lg == m, row8, 8), axis=0, keepdims=True)
    p_group = 1.0 / jnp.sum(jnp.exp(lg - m), axis=0, keepdims=True)
    le = logits[8:16]
    for g in range(1, N_EXPERT_GROUPS):
        le = jnp.where(g_idx == g, logits[8 + 8 * g:16 + 8 * g], le)
    ex = jnp.exp(le - jnp.max(le, axis=0, keepdims=True))
    pe = ex / jnp.sum(ex, axis=0, keepdims=True)
    p1 = jnp.max(pe, axis=0, keepdims=True)
    i1 = jnp.min(jnp.where(pe == p1, row8, 8), axis=0, keepdims=True)
    pe2 = jnp.where(row8 == i1, -1.0, pe)
    p2 = jnp.max(pe2, axis=0, keepdims=True)
    i2 = jnp.min(jnp.where(pe2 == p2, row8, 8), axis=0, keepdims=True)
    first_lo = i1 < i2
    w_lo = p_group * jnp.where(first_lo, p1, p2)
    w_hi = p_group * jnp.where(first_lo, p2, p1)
    cls = g_idx * 64 + jnp.minimum(i1, i2) * EXPERTS_PER_GROUP + jnp.maximum(i1, i2)

    onehot = lax.broadcasted_iota(jnp.int32, (N_CLASS_IDS, tile), 0) == cls
    ohb = onehot.astype(BF16)
    before = _dot(ohb, tri_ref[...])
    base = base_ref[...]
    base_t = jnp.concatenate([base] * (tile // META_W), axis=1)
    rank = jnp.sum(jnp.where(onehot, before + base_t, 0.0), axis=0, keepdims=True)
    new_base = base + _dot(ohb, jnp.ones((tile, META_W), BF16))
    base_ref[...] = new_base
    cout_ref[...] = new_base

    meta_t = jnp.concatenate([w_lo, w_hi, cls.astype(F32), rank, jnp.zeros((META_W - 4, tile), F32)], axis=0)
    x1e_ref[:, 0:D_MODEL] = x1
    x1e_ref[:, D_MODEL:ROW_W] = meta_t.T


def _const_spec(shape):
    nd = len(shape)
    return pl.BlockSpec(shape, lambda i, _nd=nd: (0,) * _nd, pipeline_mode=pl.Buffered(1))


def _mixer(x, y_b, counts_in, lw, seq):
    n_tok = x.shape[0]
    n_tiles = n_tok // TILE
    hb = TILE // HALO
    n_hblk = n_tok // HALO
    consts = [lw["w_in"], lw["b_in"], lw["pool_w"], lw["pool_scale"], lw["conv_w"], lw["conv_b"],
              lw["sgu_ln_g"], lw["sgu_ln_b"], lw["sgu_w"], lw["sgu_b"], lw["branch_proj"], lw["w_out"],
              lw["ln1_g"], lw["ln1_b"], lw["w_r"], lw["b_r"], lw["tri"]]
    return pl.pallas_call(
        functools.partial(_mixer_kernel, seq=seq),
        grid=(n_tiles,),
        in_specs=[
            pl.BlockSpec((TILE, D_MODEL), lambda i: (i, 0)),
            pl.BlockSpec((HALO, D_MODEL), lambda i: (jnp.maximum(i * hb - 1, 0), 0)),
            pl.BlockSpec((HALO, D_MODEL), lambda i: (jnp.minimum((i + 1) * hb, n_hblk - 1), 0)),
            pl.BlockSpec((TILE, BRANCH_W), lambda i: (i, 0)),
            pl.BlockSpec(COUNT_SHAPE, lambda i: (0, 0)),
        ] + [_const_spec(c.shape) for c in consts],
        out_specs=[
            pl.BlockSpec((TILE, ROW_W), lambda i: (i, 0)),
            pl.BlockSpec(COUNT_SHAPE, lambda i: (0, 0)),
        ],
        out_shape=[
            jax.ShapeDtypeStruct((n_tok, ROW_W), F32),
            jax.ShapeDtypeStruct(COUNT_SHAPE, F32),
        ],
        scratch_shapes=[pltpu.VMEM(COUNT_SHAPE, F32)],
        compiler_params=pltpu.CompilerParams(
            dimension_semantics=("arbitrary",), vmem_limit_bytes=VMEM_LIMIT),
        name="mixer",
    )(x, x, x, y_b, counts_in, *consts)


def _cumsum_sublanes(a):
    n = a.shape[0]
    row = lax.broadcasted_iota(jnp.int32, a.shape, 0)
    s = 1
    while s < n:
        a = a + jnp.where(row >= s, pltpu.roll(a, s, 0), 0.0)
        s *= 2
    return a


def _tables_kernel(cnt_ref, tab_ref, start_ref, *, nbp):
    cnt = cnt_ref[...]
    nb = jnp.floor((cnt + (MOE_BLOCK - 1)) * (1.0 / MOE_BLOCK))
    end = _cumsum_sublanes(nb)
    start_ref[...] = ((end - nb) * MOE_BLOCK).T[0:8, :]
    end_t = jnp.concatenate([end] * (nbp // META_W), axis=1)
    j = lax.broadcasted_iota(jnp.int32, (N_CLASS_IDS, nbp), 1).astype(F32)
    blk_cls = jnp.sum((end_t <= j).astype(F32), axis=0, keepdims=True)
    total = end_t[N_CLASS_IDS - 1:N_CLASS_IDS, :]
    jr = j[0:1, :]
    active = jr < total
    last_cls = jnp.max(jnp.where(active, blk_cls, 0.0), axis=-1, keepdims=True)
    cls_i = jnp.where(active, blk_cls, last_cls).astype(jnp.int32)
    g8 = (cls_i >> 6) * EXPERTS_PER_GROUP
    e_lo = g8 + ((cls_i >> 3) & 7)
    e_hi = g8 + (cls_i & 7)
    bidx = jnp.minimum(jr, total - 1.0).astype(jnp.int32)
    zero = jnp.zeros_like(bidx)
    tab_ref[...] = jnp.concatenate([e_lo, e_hi, bidx, total.astype(jnp.int32), zero, zero, zero, zero], axis=0)


def _tables(counts, nbp):
    return pl.pallas_call(
        functools.partial(_tables_kernel, nbp=nbp),
        out_shape=[jax.ShapeDtypeStruct((8, nbp), jnp.int32),
                   jax.ShapeDtypeStruct((8, N_CLASS_IDS), F32)],
        name="moe_tables",
    )(counts)


def _dest_kernel(meta_ref, start_ref, dest_ref):
    meta = meta_ref[...]
    cls = meta[:, 2:3].astype(jnp.int32)
    rank = meta[:, 3:4]
    onehot = lax.broadcasted_iota(jnp.int32, (TILE, N_CLASS_IDS), 1) == cls
    start = jnp.sum(jnp.where(onehot, start_ref[0:1, :], 0.0), axis=-1, keepdims=True)
    dest_col = jnp.broadcast_to(start + rank, (TILE, META_W))
    dest_ref[...] = dest_col.T[0:1, :].astype(jnp.int32)


def _dest(x1e, start):
    n_tok = x1e.shape[0]
    n_tiles = n_tok // TILE
    return pl.pallas_call(
        _dest_kernel,
        grid=(n_tiles,),
        in_specs=[pl.BlockSpec((TILE, META_W), lambda i: (i, D_MODEL // META_W)),
                  pl.BlockSpec((8, N_CLASS_IDS), lambda i: (0, 0))],
        out_specs=pl.BlockSpec((None, 1, TILE), lambda i: (i, 0, 0)),
        out_shape=jax.ShapeDtypeStruct((n_tiles, 1, TILE), jnp.int32),
        compiler_params=pltpu.CompilerParams(dimension_semantics=("arbitrary",)),
        name="moe_dest",
    )(x1e, start)


def _dispatch_kernel(dest_ref, x_ref, xb_in_ref, xb_ref, sem):
    del xb_in_ref

    def row_copy(r, d):
        return pltpu.make_async_copy(x_ref.at[pl.ds(r, 1), :], xb_ref.at[pl.ds(d, 1), :], sem)

    def issue(r, c):
        row_copy(r, dest_ref[0, r]).start()
        return c

    lax.fori_loop(0, TILE, issue, 0, unroll=ROW_DMA_UNROLL)

    def drain(r, c):
        row_copy(0, 0).wait()
        return c

    lax.fori_loop(0, TILE, drain, 0, unroll=ROW_DMA_UNROLL)


def _dispatch(dest, x1e, xb):
    n_tiles = x1e.shape[0] // TILE
    return pl.pallas_call(
        _dispatch_kernel,
        grid=(n_tiles,),
        in_specs=[pl.BlockSpec((None, 1, TILE), lambda i: (i, 0, 0), memory_space=pltpu.SMEM),
                  pl.BlockSpec((TILE, ROW_W), lambda i: (i, 0)),
                  pl.BlockSpec(memory_space=pl.ANY)],
        out_specs=pl.BlockSpec(memory_space=pl.ANY),
        out_shape=jax.ShapeDtypeStruct(xb.shape, xb.dtype),
        scratch_shapes=[pltpu.SemaphoreType.DMA],
        input_output_aliases={2: 0},
        compiler_params=pltpu.CompilerParams(dimension_semantics=("arbitrary",), has_side_effects=True),
        name="moe_dispatch",
    )(dest, x1e, xb)


def _gather_kernel(dest_ref, ys_ref, o_ref, sem):
    def row_copy(r, d):
        return pltpu.make_async_copy(ys_ref.at[pl.ds(d, 1), :], o_ref.at[pl.ds(r, 1), :], sem)

    def issue(r, c):
        row_copy(r, dest_ref[0, r]).start()
        return c

    lax.fori_loop(0, TILE, issue, 0, unroll=ROW_DMA_UNROLL)

    def drain(r, c):
        row_copy(0, 0).wait()
        return c

    lax.fori_loop(0, TILE, drain, 0, unroll=ROW_DMA_UNROLL)


def _gather(dest, ys, n_tok):
    n_tiles = n_tok // TILE
    return pl.pallas_call(
        _gather_kernel,
        grid=(n_tiles,),
        in_specs=[pl.BlockSpec((None, 1, TILE), lambda i: (i, 0, 0), memory_space=pltpu.SMEM),
                  pl.BlockSpec(memory_space=pl.ANY)],
        out_specs=pl.BlockSpec((TILE, D_MODEL), lambda i: (i, 0)),
        out_shape=jax.ShapeDtypeStruct((n_tok, D_MODEL), F32),
        scratch_shapes=[pltpu.SemaphoreType.DMA],
        compiler_params=pltpu.CompilerParams(dimension_semantics=("arbitrary",)),
        name="moe_gather",
    )(dest, ys)


def _expert_kernel(elo_ref, ehi_ref, bidx_ref, tot_ref,
                   xb_ref, w1lo_ref, w3lo_ref, w2lo_ref, w1hi_ref, w3hi_ref, w2hi_ref,
                   g_ref, b_ref, o_ref):
    del elo_ref, ehi_ref, bidx_ref
    active = pl.program_id(0) < tot_ref[0]

    @pl.when(jnp.logical_not(active))
    def _():
        o_ref[...] = jnp.zeros_like(o_ref)

    @pl.when(active)
    def _():
        x = xb_ref[:, 0:D_MODEL]
        xh = x.astype(BF16)

        def expert(w1_ref, w3_ref, w2_ref):
            half_a = _dot(xh, w1_ref[...])
            hid = (half_a * (jnp.tanh(half_a) + 1.0)) * _dot(xh, w3_ref[...])
            return _dot(hid.astype(BF16), w2_ref[...])

        w_lo = xb_ref[:, D_MODEL:D_MODEL + 1]
        w_hi = xb_ref[:, D_MODEL + 1:D_MODEL + 2]
        moe = w_lo * expert(w1lo_ref, w3lo_ref, w2lo_ref) + w_hi * expert(w1hi_ref, w3hi_ref, w2hi_ref)
        o_ref[...] = _layer_norm(DN_ALPHA * x + moe, g_ref[...], b_ref[...])


def _experts(tab, xb, lw):
    n_rows = xb.shape[0]
    up = lambda sel: pl.BlockSpec((None, D_MODEL, D_EXPERT), lambda j, lo, hi, bi, tot: ((lo, hi)[sel][j], 0, 0))
    down = lambda sel: pl.BlockSpec((None, D_EXPERT, D_MODEL), lambda j, lo, hi, bi, tot: ((lo, hi)[sel][j], 0, 0))
    vec = pl.BlockSpec((1, D_MODEL), lambda j, lo, hi, bi, tot: (0, 0))
    grid_spec = pltpu.PrefetchScalarGridSpec(
        num_scalar_prefetch=4,
        grid=(n_rows // MOE_BLOCK,),
        in_specs=[pl.BlockSpec((MOE_BLOCK, ROW_W), lambda j, lo, hi, bi, tot: (bi[j], 0)),
                  up(0), up(0), down(0), up(1), up(1), down(1), vec, vec],
        out_specs=pl.BlockSpec((MOE_BLOCK, D_MODEL), lambda j, lo, hi, bi, tot: (j, 0)),
    )
    return pl.pallas_call(
        _expert_kernel,
        grid_spec=grid_spec,
        out_shape=jax.ShapeDtypeStruct((n_rows, D_MODEL), F32),
        compiler_params=pltpu.CompilerParams(
            dimension_semantics=("arbitrary",), vmem_limit_bytes=VMEM_LIMIT),
        name="moe_experts",
    )(tab[0], tab[1], tab[2], tab[3, 0:1], xb,
      lw["w1"], lw["w3"], lw["w2"], lw["w1"], lw["w3"], lw["w2"], lw["ln2_g"], lw["ln2_b"])


def _layer_weights(p, l):
    row = lambda a: a[l].reshape(1, -1).astype(F32)
    w_r = jnp.zeros((META_W, D_MODEL), F32)
    w_r = w_r.at[0:N_EXPERT_GROUPS].set(p["w_rg"][l].T).at[8:8 + N_EXPERTS].set(p["w_re"][l].T)
    b_r = jnp.zeros((META_W, 1), F32)
    b_r = b_r.at[0:N_EXPERT_GROUPS, 0].set(p["b_rg"][l]).at[N_EXPERT_GROUPS:8, 0].set(NEG_BIG)
    b_r = b_r.at[8:8 + N_EXPERTS, 0].set(p["b_re"][l])
    t = np.arange(TILE)
    gate_half = jnp.asarray(np.where(np.arange(p["w_in"].shape[-1]) >= OFF_G, 0.5, 1.0), F32)
    return {
        "w_in": (p["w_in"][l] * gate_half).astype(BF16),
        "b_in": row(p["b_in"]) * gate_half,
        "w_b": p["w_in"][l][:, OFF_B:OFF_B + BRANCH_W].astype(BF16),
        "b_b": p["b_in"][l][OFF_B:OFF_B + BRANCH_W].reshape(1, -1),
        "pool_w": jax.scipy.linalg.block_diag(*[p["pool_w"][l][g] for g in range(N_GROUPS)]).astype(BF16),
        "pool_scale": row(p["pool_scale"]),
        "conv_w": p["conv_w"][l],
        "conv_b": row(p["conv_b"]),
        "sgu_ln_g": row(p["sgu_ln_g"]),
        "sgu_ln_b": row(p["sgu_ln_b"]),
        "sgu_w": p["sgu_w"][l].astype(BF16),
        "sgu_b": jnp.repeat(p["sgu_b"][l].T, GROUP_W, axis=1),
        "branch_proj": (0.5 * p["branch_proj"][l]).astype(BF16),
        "w_out": p["w_out"][l].astype(BF16),
        "ln1_g": row(p["ln1_g"]),
        "ln1_b": row(p["ln1_b"]),
        "w_r": w_r,
        "b_r": b_r,
        "tri": jnp.asarray((t[:, None] < t[None, :]).astype(np.float32)).astype(BF16),
        "w1": (0.5 * p["w1"][l]).astype(BF16),
        "w3": p["w3"][l].astype(BF16),
        "w2": p["w2"][l].astype(BF16),
        "ln2_g": row(p["ln2_g"]),
        "ln2_b": row(p["ln2_b"]),
    }


def _encoder_layer(xs, seqs, lw):
    n_total = sum(x.shape[0] for x in xs)
    n_blocks = n_total // MOE_BLOCK + N_REACHABLE_CLASSES
    nbp = -(-n_blocks // N_CLASS_IDS) * N_CLASS_IDS
    counts = jnp.zeros(COUNT_SHAPE, F32)
    x1es = []
    for x, seq in zip(xs, seqs):
        y_b = _fourier_mixer(x, lw["w_b"], lw["b_b"], seq)
        x1e, counts = _mixer(x, y_b, counts, lw, seq)
        x1es.append(x1e)
    tab, start = _tables(counts, nbp)
    dests = [_dest(x1e, start) for x1e in x1es]
    xb = jnp.zeros((n_blocks * MOE_BLOCK, ROW_W), F32)
    for dest, x1e in zip(dests, x1es):
        xb = _dispatch(dest, x1e, xb)
    ys = _experts(tab, xb, lw)
    return [_gather(dest, ys, x.shape[0]) for dest, x in zip(dests, xs)]


def kernel(x_prompt, x_sample, w_in, b_in, pool_w, pool_scale, conv_w, conv_b, sgu_ln_g, sgu_ln_b,
           sgu_w, sgu_b, branch_proj, w_out, ln1_g, ln1_b, w_rg, b_rg, w_re, b_re, w1, w3, w2,
           ln2_g, ln2_b):
    p = dict(w_in=w_in, b_in=b_in, pool_w=pool_w, pool_scale=pool_scale, conv_w=conv_w, conv_b=conv_b,
             sgu_ln_g=sgu_ln_g, sgu_ln_b=sgu_ln_b, sgu_w=sgu_w, sgu_b=sgu_b, branch_proj=branch_proj,
             w_out=w_out, ln1_g=ln1_g, ln1_b=ln1_b, w_rg=w_rg, b_rg=b_rg, w_re=w_re, b_re=b_re,
             w1=w1, w3=w3, w2=w2, ln2_g=ln2_g, ln2_b=ln2_b)
    shapes = (x_prompt.shape, x_sample.shape)
    seqs = [s[1] for s in shapes]
    xs = [x_prompt.reshape(-1, D_MODEL), x_sample.reshape(-1, D_MODEL)]
    for l in range(w_in.shape[0]):
        xs = _encoder_layer(xs, seqs, _layer_weights(p, l))
    return tuple(x.reshape(s) for x, s in zip(xs, shapes))
```

```python
import functools
import math

import numpy as np
import jax
import jax.numpy as jnp
from jax import lax
from jax.experimental import pallas as pl
from jax.experimental.pallas import tpu as pltpu

F32 = jnp.float32
BF16 = jnp.bfloat16

D_MODEL = 1024
DEPTH = 2
BRANCH_W = 256
N_GROUPS = 4
GROUP_W = 64
POOL_WINDOWS = (2, 4, 8, 16)
CHUNK = 128
OFF_A = 0
OFF_B = 256
OFF_CH = 512
OFF_DU = 1280
OFF_G = 1792
N_EXPERT_GROUPS = 4
EXPERTS_PER_GROUP = 8
N_EXPERTS = 32
D_EXPERT = 512
MOE_BLOCK = 256
DN_ALPHA = (2 * DEPTH) ** 0.25
LN_EPS = 1e-5

N_CLASS_IDS = 256
N_REACHABLE_CLASSES = N_EXPERT_GROUPS * (EXPERTS_PER_GROUP * (EXPERTS_PER_GROUP - 1) // 2)
META_W = 128
ROW_W = D_MODEL + META_W
COUNT_SHAPE = (N_CLASS_IDS, META_W)
HALO = 16
TILE = 512
DEST_TILES = 8
ROW_DMA_UNROLL = 16
NEG_BIG = -1e30
VMEM_LIMIT = 56 * 1024 * 1024


def _dot(a, b):
    return jnp.dot(a, b, preferred_element_type=F32)


def _layer_norm(x, g, b):
    mu = jnp.mean(x, axis=-1, keepdims=True)
    xc = x - mu
    var = jnp.mean(xc * xc, axis=-1, keepdims=True)
    return xc * lax.rsqrt(var + LN_EPS) * g + b


def _gelu_tanh(x):
    return 0.5 * x * (1.0 + jnp.tanh(math.sqrt(2.0 / math.pi) * (x + 0.044715 * (x * x * x))))


def _fft_factors(seq):
    n1 = 1 << (int(math.log2(seq)) // 2)
    return n1, seq // n1


@functools.lru_cache(maxsize=None)
def _fft_tables(seq):
    n1, n2 = _fft_factors(seq)
    c = np.arange(GROUP_W)
    ang = 2.0 * np.pi * ((c[:, None] * c[None, :]) % GROUP_W) / GROUP_W
    eye = np.eye(N_GROUPS)
    cs = np.concatenate([np.kron(eye, np.cos(ang)), np.kron(eye, np.sin(ang))], axis=1)
    k1 = np.arange(n1)
    t1 = np.arange(n1)
    t2 = np.arange(n2)
    t = t1[None, None, :] * n2 + t2[:, None, None]
    ang1 = 2.0 * np.pi * ((k1[None, :, None] * t) % seq) / seq
    gc, gs = np.cos(ang1), np.sin(ang1)
    lt = np.concatenate([np.concatenate([gc, -gs], axis=2), np.concatenate([-gs, -gc], axis=2)], axis=1)
    k2 = np.arange(n2)
    ang2 = 2.0 * np.pi * ((k2[:, None] * t2[None, :]) % n2) / n2
    scale = 1.0 / math.sqrt(seq * GROUP_W)
    to16 = lambda a: jnp.asarray(a, dtype=F32).astype(BF16)
    return to16(cs), to16(lt), to16(np.cos(ang2) * scale), to16(np.sin(ang2) * scale)


def _fft_stage1_kernel(x_ref, wb_ref, bb_ref, cs_ref, lt_ref, o_ref, u_ref, *, n1, tb):
    x = x_ref[...].reshape(n1 * tb, D_MODEL).astype(BF16)
    zb = _dot(x, wb_ref[...]) + bb_ref[...]
    u = _dot(zb.astype(BF16), cs_ref[...])
    for c in range(4):
        u_ref[c] = u[:, c * 128:(c + 1) * 128]
    for j in range(tb):
        q = [u_ref[c, pl.ds(j, n1, stride=tb), :] for c in range(4)]
        stacked = jnp.concatenate([jnp.concatenate(q[0:2], axis=1),
                                   jnp.concatenate(q[2:4], axis=1)], axis=0).astype(BF16)
        b = _dot(lt_ref[j], stacked)
        o_ref[:, j * 512:j * 512 + BRANCH_W] = b[:n1].astype(BF16)
        o_ref[:, j * 512 + BRANCH_W:(j + 1) * 512] = b[n1:].astype(BF16)


def _fft_stage2_kernel(b_ref, c2_ref, s2_ref, o_ref, *, kb):
    for k in range(kb):
        slab = b_ref[k]
        y = _dot(c2_ref[...], slab[:, :BRANCH_W]) + _dot(s2_ref[...], slab[:, BRANCH_W:])
        o_ref[:, k * BRANCH_W:(k + 1) * BRANCH_W] = y.astype(BF16)


def _fourier_mixer(x, w_b, b_b, seq):
    n_tok = x.shape[0]
    bsz = n_tok // seq
    n1, n2 = _fft_factors(seq)
    tb = 8
    kb = 8
    cs, lt, c2, s2 = _fft_tables(seq)
    x4 = x.reshape(bsz, n1, n2, D_MODEL)
    bt = pl.pallas_call(
        functools.partial(_fft_stage1_kernel, n1=n1, tb=tb),
        grid=(bsz, n2 // tb),
        in_specs=[
            pl.BlockSpec((None, n1, tb, D_MODEL), lambda b, j: (b, 0, j, 0)),
            pl.BlockSpec((D_MODEL, BRANCH_W), lambda b, j: (0, 0)),
            pl.BlockSpec((1, BRANCH_W), lambda b, j: (0, 0)),
            pl.BlockSpec((BRANCH_W, 2 * BRANCH_W), lambda b, j: (0, 0)),
            pl.BlockSpec((tb, 2 * n1, 2 * n1), lambda b, j: (j, 0, 0)),
        ],
        out_specs=pl.BlockSpec((None, n1, tb * 512), lambda b, j: (b, 0, j)),
        out_shape=jax.ShapeDtypeStruct((bsz, n1, n2 * 512), BF16),
        scratch_shapes=[pltpu.VMEM((4, n1 * tb, 128), F32)],
        compiler_params=pltpu.CompilerParams(
            dimension_semantics=("arbitrary", "arbitrary"), vmem_limit_bytes=VMEM_LIMIT),
        name="fft_stage1",
    )(x4, w_b, b_b, cs, lt)
    bt4 = bt.reshape(bsz, n1, n2, 512)
    y = pl.pallas_call(
        functools.partial(_fft_stage2_kernel, kb=kb),
        grid=(bsz, n1 // kb),
        in_specs=[
            pl.BlockSpec((None, kb, n2, 512), lambda b, i: (b, i, 0, 0)),
            pl.BlockSpec((n2, n2), lambda b, i: (0, 0)),
            pl.BlockSpec((n2, n2), lambda b, i: (0, 0)),
        ],
        out_specs=pl.BlockSpec((None, n2, kb * BRANCH_W), lambda b, i: (b, 0, i)),
        out_shape=jax.ShapeDtypeStruct((bsz, n2, n1 * BRANCH_W), BF16),
        compiler_params=pltpu.CompilerParams(
            dimension_semantics=("arbitrary", "arbitrary"), vmem_limit_bytes=VMEM_LIMIT),
        name="fft_stage2",
    )(bt4, c2, s2)
    return y.reshape(n_tok, BRANCH_W)


def _mixer_kernel(x_ref, xp_ref, xn_ref, yb_ref, cin_ref,
                  win_ref, bin_ref, poolw_ref, pools_ref, convw_ref, convb_ref,
                  lng_ref, lnb_ref, sguw_ref, sgub_ref, bp_ref, wout_ref,
                  ln1g_ref, ln1b_ref, wr_ref, br_ref, tri_ref,
                  x1e_ref, mrow_ref, cout_ref, base_ref, pre_ref, *, seq, n_tiles):
    i = pl.program_id(0)
    tile = TILE
    ext = tile + 2 * HALO
    p0 = (jnp.minimum(i, n_tiles - 1) % (seq // tile)) * tile

    @pl.when(i == 0)
    def _():
        base_ref[...] = cin_ref[...]
        pre_ref[...] = jnp.zeros_like(pre_ref)

    live = i > 0

    x = x_ref[...]
    xc = x.astype(BF16)
    xe = jnp.concatenate([xp_ref[...].astype(BF16), xc, xn_ref[...].astype(BF16)], axis=0)
    pos = p0 - HALO + lax.broadcasted_iota(jnp.int32, (ext, BRANCH_W), 0)
    valid = (pos >= 0) & (pos < seq)
    lane = lax.broadcasted_iota(jnp.int32, (ext, BRANCH_W), 1)
    grp = lane // GROUP_W

    def half_gate_logits(k):
        lo = OFF_G + k * D_MODEL
        return _dot(xc, win_ref[:, lo:lo + D_MODEL]) + bin_ref[:, lo:lo + D_MODEL]

    def gated(half_gz, y_k, k):
        return (jnp.tanh(half_gz) + 1.0) * _dot(y_k.astype(BF16), bp_ref[k])

    za = jnp.where(valid, _dot(xe, win_ref[:, OFF_A:OFF_A + BRANCH_W]) + bin_ref[:, OFF_A:OFF_A + BRANCH_W], 0.0)
    zc = _dot(xe, win_ref[:, OFF_CH:OFF_DU]) + bin_ref[:, OFF_CH:OFF_DU]
    zd = _dot(xc, win_ref[:, OFF_DU:OFF_G]) + bin_ref[:, OFF_DU:OFF_G]
    hg0 = half_gate_logits(0)

    xh, xl = _tail_norm(pre_ref[...], ln1g_ref, ln1b_ref, x1e_ref)

    s2 = za + pltpu.roll(za, 1, 0)
    s4 = s2 + pltpu.roll(s2, 2, 0)
    s8 = s4 + pltpu.roll(s4, 4, 0)
    s16 = s8 + pltpu.roll(s8, 8, 0)
    c4 = pltpu.roll(s4, ext - 1, 0)
    c8 = pltpu.roll(s8, ext - 3, 0)
    c16 = pltpu.roll(s16, ext - 7, 0)
    wsum = jnp.where(grp == 0, s2, jnp.where(grp == 1, c4, jnp.where(grp == 2, c8, c16)))
    half = jnp.where(grp == 0, 1, jnp.where(grp == 1, 2, jnp.where(grp == 2, 4, 8)))
    cnt = jnp.minimum(pos + half, seq) - jnp.maximum(pos - half, 0)
    pooled = (wsum / jnp.maximum(cnt, 1).astype(F32) - za)[HALO:HALO + tile]
    y_a = _dot(pooled.astype(BF16), poolw_ref[...]) * pools_ref[...]
    hg1 = half_gate_logits(1)

    h = zc[:, 0:BRANCH_W]
    gate_b = zc[:, BRANCH_W:2 * BRANCH_W]
    gate_c = zc[:, 2 * BRANCH_W:3 * BRANCH_W]
    q = jnp.where(valid, gate_c * h, 0.0)
    q_prev = pltpu.roll(q, 1, 0)[HALO:HALO + tile]
    q_next = pltpu.roll(q, ext - 1, 0)[HALO:HALO + tile]
    conv = (q_prev * convw_ref[0:1, :] + q[HALO:HALO + tile] * convw_ref[1:2, :]
            + q_next * convw_ref[2:3, :] + convb_ref[...])
    y_c = gate_b[HALO:HALO + tile] * conv
    merged = gated(hg0, y_a, 0)
    hg2 = half_gate_logits(2)

    w_lo, w_hi, cls = _tail_route(xh, xl, wr_ref, br_ref)

    u = _gelu_tanh(zd[:, 0:BRANCH_W])
    v = _layer_norm(_gelu_tanh(zd[:, BRANCH_W:]), lng_ref[...], lnb_ref[...]).astype(BF16)
    grp_c = lax.broadcasted_iota(jnp.int32, (CHUNK, BRANCH_W), 1) // GROUP_W
    sps = []
    for c in range(tile // CHUNK):
        vch = v[c * CHUNK:(c + 1) * CHUNK]
        sp = None
        for g in range(N_GROUPS):
            r = _dot(sguw_ref[g], vch)
            sp = r if sp is None else jnp.where(grp_c == g, r, sp)
        sps.append(sp + sgub_ref[...])
    y_d = u * jnp.concatenate(sps, axis=0)
    merged = merged + gated(hg1, yb_ref[...], 1)
    hg3 = half_gate_logits(3)

    _tail_rank(w_lo, w_hi, cls, live, tri_ref, x1e_ref, mrow_ref, cout_ref, base_ref)

    merged = merged + gated(hg2, y_c, 2)
    merged = merged + gated(hg3, y_d, 3)
    pre_ref[...] = DN_ALPHA * x + _dot(merged.astype(BF16), wout_ref[...])


def _tail_norm(pre, ln1g_ref, ln1b_ref, x1e_ref):
    x1 = _layer_norm(pre, ln1g_ref[...], ln1b_ref[...])
    x1e_ref[:, 0:D_MODEL] = x1
    xh = x1.astype(BF16)
    return xh, (x1 - xh.astype(F32)).astype(BF16)


def _tail_route(xh, xl, wr_ref, br_ref):
    tile = TILE
    wr = wr_ref[...]
    wh = wr.astype(BF16)
    wl = (wr - wh.astype(F32)).astype(BF16)
    nt = (((1,), (1,)), ((), ()))
    dot_nt = lambda a, b: lax.dot_general(a, b, nt, preferred_element_type=F32)
    logits = dot_nt(wh, xh) + dot_nt(wh, xl) + dot_nt(wl, xh) + br_ref[...]
    row8 = lax.broadcasted_iota(jnp.int32, (EXPERTS_PER_GROUP, tile), 0)
    lg = logits[0:8]
    m = jnp.max(lg, axis=0, keepdims=True)
    g_idx = jnp.min(jnp.where(lg == m, row8, 8), axis=0, keepdims=True)
    p_group = 1.0 / jnp.sum(jnp.exp(lg - m), axis=0, keepdims=True)
    le = logits[8:16]
    for g in range(1, N_EXPERT_GROUPS):
        le = jnp.where(g_idx == g, logits[8 + 8 * g:16 + 8 * g], le)
    ex = jnp.exp(le - jnp.max(le, axis=0, keepdims=True))
    pe = ex / jnp.sum(ex, axis=0, keepdims=True)
    p1 = jnp.max(pe, axis=0, keepdims=True)
    i1 = jnp.min(jnp.where(pe == p1, row8, 8), axis=0, keepdims=True)
    pe2 = jnp.where(row8 == i1, -1.0, pe)
    p2 = jnp.max(pe2, axis=0, keepdims=True)
    i2 = jnp.min(jnp.where(pe2 == p2, row8, 8), axis=0, keepdims=True)
    first_lo = i1 < i2
    w_lo = p_group * jnp.where(first_lo, p1, p2)
    w_hi = p_group * jnp.where(first_lo, p2, p1)
    cls = g_idx * 64 + jnp.minimum(i1, i2) * EXPERTS_PER_GROUP + jnp.maximum(i1, i2)
    return w_lo, w_hi, cls


def _tail_rank(w_lo, w_hi, cls, live, tri_ref, x1e_ref, mrow_ref, cout_ref, base_ref):
    tile = TILE
    onehot = jnp.logical_and(lax.broadcasted_iota(jnp.int32, (N_CLASS_IDS, tile), 0) == cls, live)
    ohb = onehot.astype(BF16)
    before = _dot(ohb, tri_ref[...])
    base = base_ref[...]
    base_t = jnp.concatenate([base] * (tile // META_W), axis=1)
    rank = jnp.sum(jnp.where(onehot, before + base_t, 0.0), axis=0, keepdims=True)
    new_base = base + _dot(ohb, jnp.ones((tile, META_W), BF16))
    base_ref[...] = new_base
    cout_ref[...] = new_base

    meta_t = jnp.concatenate([w_lo, w_hi, cls.astype(F32), rank, jnp.zeros((META_W - 4, tile), F32)], axis=0)
    x1e_ref[:, D_MODEL:ROW_W] = meta_t.T
    mrow_ref[...] = meta_t[0:8]


def _const_spec(shape):
    nd = len(shape)
    return pl.BlockSpec(shape, lambda i, _nd=nd: (0,) * _nd, pipeline_mode=pl.Buffered(1))


def _mixer(x, y_b, counts_in, lw, seq):
    n_tok = x.shape[0]
    n_tiles = n_tok // TILE
    hb = TILE // HALO
    n_hblk = n_tok // HALO
    consts = [lw["w_in"], lw["b_in"], lw["pool_w"], lw["pool_scale"], lw["conv_w"], lw["conv_b"],
              lw["sgu_ln_g"], lw["sgu_ln_b"], lw["sgu_w"], lw["sgu_b"], lw["branch_proj"], lw["w_out"],
              lw["ln1_g"], lw["ln1_b"], lw["w_r"], lw["b_r"], lw["tri"]]
    cur = lambda i: jnp.minimum(i, n_tiles - 1)
    return pl.pallas_call(
        functools.partial(_mixer_kernel, seq=seq, n_tiles=n_tiles),
        grid=(n_tiles + 1,),
        in_specs=[
            pl.BlockSpec((TILE, D_MODEL), lambda i: (cur(i), 0)),
            pl.BlockSpec((HALO, D_MODEL), lambda i: (jnp.maximum(cur(i) * hb - 1, 0), 0)),
            pl.BlockSpec((HALO, D_MODEL), lambda i: (jnp.minimum((cur(i) + 1) * hb, n_hblk - 1), 0)),
            pl.BlockSpec((TILE, BRANCH_W), lambda i: (cur(i), 0)),
            pl.BlockSpec(COUNT_SHAPE, lambda i: (0, 0)),
        ] + [_const_spec(c.shape) for c in consts],
        out_specs=[
            pl.BlockSpec((TILE, ROW_W), lambda i: (jnp.maximum(i - 1, 0), 0)),
            pl.BlockSpec((None, 8, TILE), lambda i: (jnp.maximum(i - 1, 0), 0, 0)),
            pl.BlockSpec(COUNT_SHAPE, lambda i: (0, 0)),
        ],
        out_shape=[
            jax.ShapeDtypeStruct((n_tok, ROW_W), F32),
            jax.ShapeDtypeStruct((n_tiles, 8, TILE), F32),
            jax.ShapeDtypeStruct(COUNT_SHAPE, F32),
        ],
        scratch_shapes=[pltpu.VMEM(COUNT_SHAPE, F32), pltpu.VMEM((TILE, D_MODEL), F32)],
        compiler_params=pltpu.CompilerParams(
            dimension_semantics=("arbitrary",), vmem_limit_bytes=VMEM_LIMIT),
        name="mixer",
    )(x, x, x, y_b, counts_in, *consts)


def _cumsum_sublanes(a):
    n = a.shape[0]
    row = lax.broadcasted_iota(jnp.int32, a.shape, 0)
    s = 1
    while s < n:
        a = a + jnp.where(row >= s, pltpu.roll(a, s, 0), 0.0)
        s *= 2
    return a


def _tables_kernel(cnt_ref, tab_ref, start_ref, *, nbp):
    cnt = cnt_ref[...]
    nb = jnp.floor((cnt + (MOE_BLOCK - 1)) * (1.0 / MOE_BLOCK))
    end = _cumsum_sublanes(nb)
    start_ref[...] = (end - nb) * MOE_BLOCK
    end_t = jnp.concatenate([end] * (nbp // META_W), axis=1)
    j = lax.broadcasted_iota(jnp.int32, (N_CLASS_IDS, nbp), 1).astype(F32)
    blk_cls = jnp.sum((end_t <= j).astype(F32), axis=0, keepdims=True)
    total = end_t[N_CLASS_IDS - 1:N_CLASS_IDS, :]
    jr = j[0:1, :]
    active = jr < total
    last_cls = jnp.max(jnp.where(active, blk_cls, 0.0), axis=-1, keepdims=True)
    cls_i = jnp.where(active, blk_cls, last_cls).astype(jnp.int32)
    g8 = (cls_i >> 6) * EXPERTS_PER_GROUP
    e_lo = g8 + ((cls_i >> 3) & 7)
    e_hi = g8 + (cls_i & 7)
    bidx = jnp.minimum(jr, total - 1.0).astype(jnp.int32)
    zero = jnp.zeros_like(bidx)
    tab_ref[...] = jnp.concatenate([e_lo, e_hi, bidx, total.astype(jnp.int32), zero, zero, zero, zero], axis=0)


def _tables(counts, nbp):
    return pl.pallas_call(
        functools.partial(_tables_kernel, nbp=nbp),
        out_shape=[jax.ShapeDtypeStruct((8, nbp), jnp.int32),
                   jax.ShapeDtypeStruct(COUNT_SHAPE, F32)],
        name="moe_tables",
    )(counts)


def _dest_kernel(mrow_ref, start_ref, dest_ref, *, tiles):
    start = jnp.concatenate([start_ref[...]] * (TILE // META_W), axis=1)
    cls_ids = lax.broadcasted_iota(jnp.int32, (N_CLASS_IDS, TILE), 0)
    for t in range(tiles):
        cls = mrow_ref[t, 2:3, :].astype(jnp.int32)
        first = jnp.sum(jnp.where(cls_ids == cls, start, 0.0), axis=0, keepdims=True)
        dest_ref[t] = (first + mrow_ref[t, 3:4, :]).astype(jnp.int32)


def _dest(mrow, start):
    n_tiles = mrow.shape[0]
    tiles = math.gcd(n_tiles, DEST_TILES)
    return pl.pallas_call(
        functools.partial(_dest_kernel, tiles=tiles),
        grid=(n_tiles // tiles,),
        in_specs=[pl.BlockSpec((tiles, 8, TILE), lambda i: (i, 0, 0)),
                  pl.BlockSpec(COUNT_SHAPE, lambda i: (0, 0))],
        out_specs=pl.BlockSpec((tiles, 1, TILE), lambda i: (i, 0, 0)),
        out_shape=jax.ShapeDtypeStruct((n_tiles, 1, TILE), jnp.int32),
        compiler_params=pltpu.CompilerParams(dimension_semantics=("arbitrary",)),
        name="moe_dest",
    )(mrow, start)


def _dispatch_kernel(dest_ref, x_ref, xb_in_ref, xb_ref, sem):
    del xb_in_ref

    def row_copy(r, d):
        return pltpu.make_async_copy(x_ref.at[pl.ds(r, 1), :], xb_ref.at[pl.ds(d, 1), :], sem)

    def issue(r, c):
        row_copy(r, dest_ref[0, r]).start()
        return c

    lax.fori_loop(0, TILE, issue, 0, unroll=ROW_DMA_UNROLL)

    def drain(r, c):
        row_copy(0, 0).wait()
        return c

    lax.fori_loop(0, TILE, drain, 0, unroll=ROW_DMA_UNROLL)


def _dispatch(dest, x1e, xb):
    n_tiles = x1e.shape[0] // TILE
    return pl.pallas_call(
        _dispatch_kernel,
        grid=(n_tiles,),
        in_specs=[pl.BlockSpec((None, 1, TILE), lambda i: (i, 0, 0), memory_space=pltpu.SMEM),
                  pl.BlockSpec((TILE, ROW_W), lambda i: (i, 0)),
                  pl.BlockSpec(memory_space=pl.ANY)],
        out_specs=pl.BlockSpec(memory_space=pl.ANY),
        out_shape=jax.ShapeDtypeStruct(xb.shape, xb.dtype),
        scratch_shapes=[pltpu.SemaphoreType.DMA],
        input_output_aliases={2: 0},
        compiler_params=pltpu.CompilerParams(dimension_semantics=("arbitrary",), has_side_effects=True),
        name="moe_dispatch",
    )(dest, x1e, xb)


def _gather_kernel(dest_ref, ys_ref, o_ref, sem):
    def row_copy(r, d):
        return pltpu.make_async_copy(ys_ref.at[pl.ds(d, 1), :], o_ref.at[pl.ds(r, 1), :], sem)

    def issue(r, c):
        row_copy(r, dest_ref[0, r]).start()
        return c

    lax.fori_loop(0, TILE, issue, 0, unroll=ROW_DMA_UNROLL)

    def drain(r, c):
        row_copy(0, 0).wait()
        return c

    lax.fori_loop(0, TILE, drain, 0, unroll=ROW_DMA_UNROLL)


def _gather(dest, ys, n_tok):
    n_tiles = n_tok // TILE
    return pl.pallas_call(
        _gather_kernel,
        grid=(n_tiles,),
        in_specs=[pl.BlockSpec((None, 1, TILE), lambda i: (i, 0, 0), memory_space=pltpu.SMEM),
                  pl.BlockSpec(memory_space=pl.ANY)],
        out_specs=pl.BlockSpec((TILE, D_MODEL), lambda i: (i, 0)),
        out_shape=jax.ShapeDtypeStruct((n_tok, D_MODEL), F32),
        scratch_shapes=[pltpu.SemaphoreType.DMA],
        compiler_params=pltpu.CompilerParams(dimension_semantics=("arbitrary",)),
        name="moe_gather",
    )(dest, ys)


def _expert_kernel(elo_ref, ehi_ref, bidx_ref, tot_ref,
                   xb_ref, w1lo_ref, w3lo_ref, w2lo_ref, w1hi_ref, w3hi_ref, w2hi_ref,
                   g_ref, b_ref, o_ref, pre_ref):
    del elo_ref, ehi_ref, bidx_ref
    j = pl.program_id(0)
    total = tot_ref[0]

    @pl.when(j == 0)
    def _():
        pre_ref[...] = jnp.zeros_like(pre_ref)

    def finish_previous():
        o_ref[...] = _layer_norm(pre_ref[...], g_ref[...], b_ref[...])

    def hidden(half_a, b):
        return ((half_a * (jnp.tanh(half_a) + 1.0)) * b).astype(BF16)

    @pl.when(j < total)
    def _():
        x = xb_ref[:, 0:D_MODEL]
        xh = x.astype(BF16)
        half_a_lo = _dot(xh, w1lo_ref[...])
        b_lo = _dot(xh, w3lo_ref[...])
        finish_previous()
        half_a_hi = _dot(xh, w1hi_ref[...])
        b_hi = _dot(xh, w3hi_ref[...])
        y_lo = _dot(hidden(half_a_lo, b_lo), w2lo_ref[...])
        y_hi = _dot(hidden(half_a_hi, b_hi), w2hi_ref[...])
        w_lo = xb_ref[:, D_MODEL:D_MODEL + 1]
        w_hi = xb_ref[:, D_MODEL + 1:D_MODEL + 2]
        pre_ref[...] = DN_ALPHA * x + (w_lo * y_lo + w_hi * y_hi)

    @pl.when(j == total)
    def _():
        finish_previous()

    @pl.when(j > total)
    def _():
        o_ref[...] = jnp.zeros_like(o_ref)


def _experts(tab, xb, lw):
    n_blocks = xb.shape[0] // MOE_BLOCK
    up = lambda sel: pl.BlockSpec((None, D_MODEL, D_EXPERT), lambda j, lo, hi, bi, tot: ((lo, hi)[sel][j], 0, 0))
    down = lambda sel: pl.BlockSpec((None, D_EXPERT, D_MODEL), lambda j, lo, hi, bi, tot: ((lo, hi)[sel][j], 0, 0))
    vec = pl.BlockSpec((1, D_MODEL), lambda j, lo, hi, bi, tot: (0, 0))
    grid_spec = pltpu.PrefetchScalarGridSpec(
        num_scalar_prefetch=4,
        grid=(n_blocks + 1,),
        in_specs=[pl.BlockSpec((MOE_BLOCK, ROW_W), lambda j, lo, hi, bi, tot: (bi[j], 0)),
                  up(0), up(0), down(0), up(1), up(1), down(1), vec, vec],
        out_specs=pl.BlockSpec((MOE_BLOCK, D_MODEL), lambda j, lo, hi, bi, tot: (jnp.maximum(j - 1, 0), 0)),
        scratch_shapes=[pltpu.VMEM((MOE_BLOCK, D_MODEL), F32)],
    )
    return pl.pallas_call(
        _expert_kernel,
        grid_spec=grid_spec,
        out_shape=jax.ShapeDtypeStruct((n_blocks * MOE_BLOCK, D_MODEL), F32),
        compiler_params=pltpu.CompilerParams(
            dimension_semantics=("arbitrary",), vmem_limit_bytes=VMEM_LIMIT),
        name="moe_experts",
    )(tab[0], tab[1], tab[2], tab[3, 0:1], xb,
      lw["w1"], lw["w3"], lw["w2"], lw["w1"], lw["w3"], lw["w2"], lw["ln2_g"], lw["ln2_b"])


def _layer_weights(p, l):
    row = lambda a: a[l].reshape(1, -1).astype(F32)
    w_r = jnp.zeros((META_W, D_MODEL), F32)
    w_r = w_r.at[0:N_EXPERT_GROUPS].set(p["w_rg"][l].T).at[8:8 + N_EXPERTS].set(p["w_re"][l].T)
    b_r = jnp.zeros((META_W, 1), F32)
    b_r = b_r.at[0:N_EXPERT_GROUPS, 0].set(p["b_rg"][l]).at[N_EXPERT_GROUPS:8, 0].set(NEG_BIG)
    b_r = b_r.at[8:8 + N_EXPERTS, 0].set(p["b_re"][l])
    t = np.arange(TILE)
    gate_half = jnp.asarray(np.where(np.arange(p["w_in"].shape[-1]) >= OFF_G, 0.5, 1.0), F32)
    return {
        "w_in": (p["w_in"][l] * gate_half).astype(BF16),
        "b_in": row(p["b_in"]) * gate_half,
        "w_b": p["w_in"][l][:, OFF_B:OFF_B + BRANCH_W].astype(BF16),
        "b_b": p["b_in"][l][OFF_B:OFF_B + BRANCH_W].reshape(1, -1),
        "pool_w": jax.scipy.linalg.block_diag(*[p["pool_w"][l][g] for g in range(N_GROUPS)]).astype(BF16),
        "pool_scale": row(p["pool_scale"]),
        "conv_w": p["conv_w"][l],
        "conv_b": row(p["conv_b"]),
        "sgu_ln_g": row(p["sgu_ln_g"]),
        "sgu_ln_b": row(p["sgu_ln_b"]),
        "sgu_w": p["sgu_w"][l].astype(BF16),
        "sgu_b": jnp.repeat(p["sgu_b"][l].T, GROUP_W, axis=1),
        "branch_proj": (0.5 * p["branch_proj"][l]).astype(BF16),
        "w_out": p["w_out"][l].astype(BF16),
        "ln1_g": row(p["ln1_g"]),
        "ln1_b": row(p["ln1_b"]),
        "w_r": w_r,
        "b_r": b_r,
        "tri": jnp.asarray((t[:, None] < t[None, :]).astype(np.float32)).astype(BF16),
        "w1": (0.5 * p["w1"][l]).astype(BF16),
        "w3": p["w3"][l].astype(BF16),
        "w2": p["w2"][l].astype(BF16),
        "ln2_g": row(p["ln2_g"]),
        "ln2_b": row(p["ln2_b"]),
    }


def _encoder_layer(xs, seqs, lw):
    n_total = sum(x.shape[0] for x in xs)
    n_blocks = n_total // MOE_BLOCK + N_REACHABLE_CLASSES
    nbp = -(-n_blocks // N_CLASS_IDS) * N_CLASS_IDS
    counts = jnp.zeros(COUNT_SHAPE, F32)
    x1es, mrows = [], []
    for x, seq in zip(xs, seqs):
        y_b = _fourier_mixer(x, lw["w_b"], lw["b_b"], seq)
        x1e, mrow, counts = _mixer(x, y_b, counts, lw, seq)
        x1es.append(x1e)
        mrows.append(mrow)
    tab, start = _tables(counts, nbp)
    dests = [_dest(mrow, start) for mrow in mrows]
    xb = jnp.zeros((n_blocks * MOE_BLOCK, ROW_W), F32)
    for dest, x1e in zip(dests, x1es):
        xb = _dispatch(dest, x1e, xb)
    ys = _experts(tab, xb, lw)
    return [_gather(dest, ys, x.shape[0]) for dest, x in zip(dests, xs)]


def kernel(x_prompt, x_sample, w_in, b_in, pool_w, pool_scale, conv_w, conv_b, sgu_ln_g, sgu_ln_b,
           sgu_w, sgu_b, branch_proj, w_out, ln1_g, ln1_b, w_rg, b_rg, w_re, b_re, w1, w3, w2,
           ln2_g, ln2_b):
    p = dict(w_in=w_in, b_in=b_in, pool_w=pool_w, pool_scale=pool_scale, conv_w=conv_w, conv_b=conv_b,
             sgu_ln_g=sgu_ln_g, sgu_ln_b=sgu_ln_b, sgu_w=sgu_w, sgu_b=sgu_b, branch_proj=branch_proj,
             w_out=w_out, ln1_g=ln1_g, ln1_b=ln1_b, w_rg=w_rg, b_rg=b_rg, w_re=w_re, b_re=b_re,
             w1=w1, w3=w3, w2=w2, ln2_g=ln2_g, ln2_b=ln2_b)
    shapes = (x_prompt.shape, x_sample.shape)
    seqs = [s[1] for s in shapes]
    xs = [x_prompt.reshape(-1, D_MODEL), x_sample.reshape(-1, D_MODEL)]
    for l in range(w_in.shape[0]):
        xs = _encoder_layer(xs, seqs, _layer_weights(p, l))
    return tuple(x.reshape(s) for x, s in zip(xs, shapes))
```

```python
import functools
import math

import numpy as np
import jax
import jax.numpy as jnp
from jax import lax
from jax.experimental import pallas as pl
from jax.experimental.pallas import tpu as pltpu
from jax.experimental.pallas import tpu_sc as plsc

F32 = jnp.float32
BF16 = jnp.bfloat16

D_MODEL = 1024
DEPTH = 2
BRANCH_W = 256
N_GROUPS = 4
GROUP_W = 64
POOL_WINDOWS = (2, 4, 8, 16)
CHUNK = 128
OFF_A = 0
OFF_B = 256
OFF_CH = 512
OFF_DU = 1280
OFF_G = 1792
N_EXPERT_GROUPS = 4
EXPERTS_PER_GROUP = 8
N_EXPERTS = 32
D_EXPERT = 512
MOE_BLOCK = 256
DN_ALPHA = (2 * DEPTH) ** 0.25
LN_EPS = 1e-5

N_CLASS_IDS = 256
N_REACHABLE_CLASSES = N_EXPERT_GROUPS * (EXPERTS_PER_GROUP * (EXPERTS_PER_GROUP - 1) // 2)
META_W = 128
ROW_W = D_MODEL + META_W
COUNT_SHAPE = (N_CLASS_IDS, META_W)
HALO = 16
TILE = 512
DEST_TILES = 8
SC_WINDOW = 128
SC_COLS = 256
NEG_BIG = -1e30
VMEM_LIMIT = 56 * 1024 * 1024


def _dot(a, b):
    return jnp.dot(a, b, preferred_element_type=F32)


def _layer_norm(x, g, b):
    mu = jnp.mean(x, axis=-1, keepdims=True)
    xc = x - mu
    var = jnp.mean(xc * xc, axis=-1, keepdims=True)
    return xc * lax.rsqrt(var + LN_EPS) * g + b


def _gelu_tanh(x):
    return 0.5 * x * (1.0 + jnp.tanh(math.sqrt(2.0 / math.pi) * (x + 0.044715 * (x * x * x))))


def _fft_factors(seq):
    n1 = 1 << (int(math.log2(seq)) // 2)
    return n1, seq // n1


@functools.lru_cache(maxsize=None)
def _fft_tables(seq):
    n1, n2 = _fft_factors(seq)
    c = np.arange(GROUP_W)
    ang = 2.0 * np.pi * ((c[:, None] * c[None, :]) % GROUP_W) / GROUP_W
    eye = np.eye(N_GROUPS)
    cs = np.concatenate([np.kron(eye, np.cos(ang)), np.kron(eye, np.sin(ang))], axis=1)
    k1 = np.arange(n1)
    t1 = np.arange(n1)
    t2 = np.arange(n2)
    t = t1[None, None, :] * n2 + t2[:, None, None]
    ang1 = 2.0 * np.pi * ((k1[None, :, None] * t) % seq) / seq
    gc, gs = np.cos(ang1), np.sin(ang1)
    lt = np.concatenate([np.concatenate([gc, -gs], axis=2), np.concatenate([-gs, -gc], axis=2)], axis=1)
    k2 = np.arange(n2)
    ang2 = 2.0 * np.pi * ((k2[:, None] * t2[None, :]) % n2) / n2
    scale = 1.0 / math.sqrt(seq * GROUP_W)
    to16 = lambda a: jnp.asarray(a, dtype=F32).astype(BF16)
    return to16(cs), to16(lt), to16(np.cos(ang2) * scale), to16(np.sin(ang2) * scale)


def _fft_stage1_kernel(x_ref, wb_ref, bb_ref, cs_ref, lt_ref, o_ref, u_ref, *, n1, tb):
    x = x_ref[...].reshape(n1 * tb, D_MODEL).astype(BF16)
    zb = _dot(x, wb_ref[...]) + bb_ref[...]
    u = _dot(zb.astype(BF16), cs_ref[...])
    for c in range(4):
        u_ref[c] = u[:, c * 128:(c + 1) * 128]
    for j in range(tb):
        q = [u_ref[c, pl.ds(j, n1, stride=tb), :] for c in range(4)]
        stacked = jnp.concatenate([jnp.concatenate(q[0:2], axis=1),
                                   jnp.concatenate(q[2:4], axis=1)], axis=0).astype(BF16)
        b = _dot(lt_ref[j], stacked)
        o_ref[:, j * 512:j * 512 + BRANCH_W] = b[:n1].astype(BF16)
        o_ref[:, j * 512 + BRANCH_W:(j + 1) * 512] = b[n1:].astype(BF16)


def _fft_stage2_kernel(b_ref, c2_ref, s2_ref, o_ref, *, kb):
    for k in range(kb):
        slab = b_ref[k]
        y = _dot(c2_ref[...], slab[:, :BRANCH_W]) + _dot(s2_ref[...], slab[:, BRANCH_W:])
        o_ref[:, k * BRANCH_W:(k + 1) * BRANCH_W] = y.astype(BF16)


def _fourier_mixer(x, w_b, b_b, seq):
    n_tok = x.shape[0]
    bsz = n_tok // seq
    n1, n2 = _fft_factors(seq)
    tb = 8
    kb = 8
    cs, lt, c2, s2 = _fft_tables(seq)
    x4 = x.reshape(bsz, n1, n2, D_MODEL)
    bt = pl.pallas_call(
        functools.partial(_fft_stage1_kernel, n1=n1, tb=tb),
        grid=(bsz, n2 // tb),
        in_specs=[
            pl.BlockSpec((None, n1, tb, D_MODEL), lambda b, j: (b, 0, j, 0)),
            pl.BlockSpec((D_MODEL, BRANCH_W), lambda b, j: (0, 0)),
            pl.BlockSpec((1, BRANCH_W), lambda b, j: (0, 0)),
            pl.BlockSpec((BRANCH_W, 2 * BRANCH_W), lambda b, j: (0, 0)),
            pl.BlockSpec((tb, 2 * n1, 2 * n1), lambda b, j: (j, 0, 0)),
        ],
        out_specs=pl.BlockSpec((None, n1, tb * 512), lambda b, j: (b, 0, j)),
        out_shape=jax.ShapeDtypeStruct((bsz, n1, n2 * 512), BF16),
        scratch_shapes=[pltpu.VMEM((4, n1 * tb, 128), F32)],
        compiler_params=pltpu.CompilerParams(
            dimension_semantics=("arbitrary", "arbitrary"), vmem_limit_bytes=VMEM_LIMIT),
        name="fft_stage1",
    )(x4, w_b, b_b, cs, lt)
    bt4 = bt.reshape(bsz, n1, n2, 512)
    y = pl.pallas_call(
        functools.partial(_fft_stage2_kernel, kb=kb),
        grid=(bsz, n1 // kb),
        in_specs=[
            pl.BlockSpec((None, kb, n2, 512), lambda b, i: (b, i, 0, 0)),
            pl.BlockSpec((n2, n2), lambda b, i: (0, 0)),
            pl.BlockSpec((n2, n2), lambda b, i: (0, 0)),
        ],
        out_specs=pl.BlockSpec((None, n2, kb * BRANCH_W), lambda b, i: (b, 0, i)),
        out_shape=jax.ShapeDtypeStruct((bsz, n2, n1 * BRANCH_W), BF16),
        compiler_params=pltpu.CompilerParams(
            dimension_semantics=("arbitrary", "arbitrary"), vmem_limit_bytes=VMEM_LIMIT),
        name="fft_stage2",
    )(bt4, c2, s2)
    return y.reshape(n_tok, BRANCH_W)


def _mixer_kernel(x_ref, xp_ref, xn_ref, yb_ref, cin_ref,
                  win_ref, bin_ref, poolw_ref, pools_ref, convw_ref, convb_ref,
                  lng_ref, lnb_ref, sguw_ref, sgub_ref, bp_ref, wout_ref,
                  ln1g_ref, ln1b_ref, wr_ref, br_ref, tri_ref,
                  x1e_ref, mrow_ref, cout_ref, base_ref, pre_ref, *, seq, n_tiles):
    i = pl.program_id(0)
    tile = TILE
    ext = tile + 2 * HALO
    p0 = (jnp.minimum(i, n_tiles - 1) % (seq // tile)) * tile

    @pl.when(i == 0)
    def _():
        base_ref[...] = cin_ref[...]
        pre_ref[...] = jnp.zeros_like(pre_ref)

    live = i > 0

    x = x_ref[...]
    xc = x.astype(BF16)
    xe = jnp.concatenate([xp_ref[...].astype(BF16), xc, xn_ref[...].astype(BF16)], axis=0)
    pos = p0 - HALO + lax.broadcasted_iota(jnp.int32, (ext, BRANCH_W), 0)
    valid = (pos >= 0) & (pos < seq)
    lane = lax.broadcasted_iota(jnp.int32, (ext, BRANCH_W), 1)
    grp = lane // GROUP_W

    def half_gate_logits(k):
        lo = OFF_G + k * D_MODEL
        return _dot(xc, win_ref[:, lo:lo + D_MODEL]) + bin_ref[:, lo:lo + D_MODEL]

    def gated(half_gz, y_k, k):
        return (jnp.tanh(half_gz) + 1.0) * _dot(y_k.astype(BF16), bp_ref[k])

    za = jnp.where(valid, _dot(xe, win_ref[:, OFF_A:OFF_A + BRANCH_W]) + bin_ref[:, OFF_A:OFF_A + BRANCH_W], 0.0)
    zc = _dot(xe, win_ref[:, OFF_CH:OFF_DU]) + bin_ref[:, OFF_CH:OFF_DU]
    zd = _dot(xc, win_ref[:, OFF_DU:OFF_G]) + bin_ref[:, OFF_DU:OFF_G]
    hg0 = half_gate_logits(0)

    xh, xl = _tail_norm(pre_ref[...], ln1g_ref, ln1b_ref, x1e_ref)

    s2 = za + pltpu.roll(za, 1, 0)
    s4 = s2 + pltpu.roll(s2, 2, 0)
    s8 = s4 + pltpu.roll(s4, 4, 0)
    s16 = s8 + pltpu.roll(s8, 8, 0)
    c4 = pltpu.roll(s4, ext - 1, 0)
    c8 = pltpu.roll(s8, ext - 3, 0)
    c16 = pltpu.roll(s16, ext - 7, 0)
    wsum = jnp.where(grp == 0, s2, jnp.where(grp == 1, c4, jnp.where(grp == 2, c8, c16)))
    half = jnp.where(grp == 0, 1, jnp.where(grp == 1, 2, jnp.where(grp == 2, 4, 8)))
    cnt = jnp.minimum(pos + half, seq) - jnp.maximum(pos - half, 0)
    pooled = (wsum / jnp.maximum(cnt, 1).astype(F32) - za)[HALO:HALO + tile]
    y_a = _dot(pooled.astype(BF16), poolw_ref[...]) * pools_ref[...]
    hg1 = half_gate_logits(1)

    h = zc[:, 0:BRANCH_W]
    gate_b = zc[:, BRANCH_W:2 * BRANCH_W]
    gate_c = zc[:, 2 * BRANCH_W:3 * BRANCH_W]
    q = jnp.where(valid, gate_c * h, 0.0)
    q_prev = pltpu.roll(q, 1, 0)[HALO:HALO + tile]
    q_next = pltpu.roll(q, ext - 1, 0)[HALO:HALO + tile]
    conv = (q_prev * convw_ref[0:1, :] + q[HALO:HALO + tile] * convw_ref[1:2, :]
            + q_next * convw_ref[2:3, :] + convb_ref[...])
    y_c = gate_b[HALO:HALO + tile] * conv
    merged = gated(hg0, y_a, 0)
    hg2 = half_gate_logits(2)

    w_lo, w_hi, cls = _tail_route(xh, xl, wr_ref, br_ref)

    u = _gelu_tanh(zd[:, 0:BRANCH_W])
    v = _layer_norm(_gelu_tanh(zd[:, BRANCH_W:]), lng_ref[...], lnb_ref[...]).astype(BF16)
    grp_c = lax.broadcasted_iota(jnp.int32, (CHUNK, BRANCH_W), 1) // GROUP_W
    sps = []
    for c in range(tile // CHUNK):
        vch = v[c * CHUNK:(c + 1) * CHUNK]
        sp = None
        for g in range(N_GROUPS):
            r = _dot(sguw_ref[g], vch)
            sp = r if sp is None else jnp.where(grp_c == g, r, sp)
        sps.append(sp + sgub_ref[...])
    y_d = u * jnp.concatenate(sps, axis=0)
    merged = merged + gated(hg1, yb_ref[...], 1)
    hg3 = half_gate_logits(3)

    _tail_rank(w_lo, w_hi, cls, live, tri_ref, x1e_ref, mrow_ref, cout_ref, base_ref)

    merged = merged + gated(hg2, y_c, 2)
    merged = merged + gated(hg3, y_d, 3)
    pre_ref[...] = DN_ALPHA * x + _dot(merged.astype(BF16), wout_ref[...])


def _tail_norm(pre, ln1g_ref, ln1b_ref, x1e_ref):
    x1 = _layer_norm(pre, ln1g_ref[...], ln1b_ref[...])
    x1e_ref[:, 0:D_MODEL] = x1
    xh = x1.astype(BF16)
    return xh, (x1 - xh.astype(F32)).astype(BF16)


def _tail_route(xh, xl, wr_ref, br_ref):
    tile = TILE
    wr = wr_ref[...]
    wh = wr.astype(BF16)
    wl = (wr - wh.astype(F32)).astype(BF16)
    nt = (((1,), (1,)), ((), ()))
    dot_nt = lambda a, b: lax.dot_general(a, b, nt, preferred_element_type=F32)
    logits = dot_nt(wh, xh) + dot_nt(wh, xl) + dot_nt(wl, xh) + br_ref[...]
    row8 = lax.broadcasted_iota(jnp.int32, (EXPERTS_PER_GROUP, tile), 0)
    lg = logits[0:8]
    m = jnp.max(lg, axis=0, keepdims=True)
    g_idx = jnp.min(jnp.where(lg == m, row8, 8), axis=0, keepdims=True)
    p_group = 1.0 / jnp.sum(jnp.exp(lg - m), axis=0, keepdims=True)
    le = logits[8:16]
    for g in range(1, N_EXPERT_GROUPS):
        le = jnp.where(g_idx == g, logits[8 + 8 * g:16 + 8 * g], le)
    ex = jnp.exp(le - jnp.max(le, axis=0, keepdims=True))
    pe = ex / jnp.sum(ex, axis=0, keepdims=True)
    p1 = jnp.max(pe, axis=0, keepdims=True)
    i1 = jnp.min(jnp.where(pe == p1, row8, 8), axis=0, keepdims=True)
    pe2 = jnp.where(row8 == i1, -1.0, pe)
    p2 = jnp.max(pe2, axis=0, keepdims=True)
    i2 = jnp.min(jnp.where(pe2 == p2, row8, 8), axis=0, keepdims=True)
    first_lo = i1 < i2
    w_lo = p_group * jnp.where(first_lo, p1, p2)
    w_hi = p_group * jnp.where(first_lo, p2, p1)
    cls = g_idx * 64 + jnp.minimum(i1, i2) * EXPERTS_PER_GROUP + jnp.maximum(i1, i2)
    return w_lo, w_hi, cls


def _tail_rank(w_lo, w_hi, cls, live, tri_ref, x1e_ref, mrow_ref, cout_ref, base_ref):
    tile = TILE
    onehot = jnp.logical_and(lax.broadcasted_iota(jnp.int32, (N_CLASS_IDS, tile), 0) == cls, live)
    ohb = onehot.astype(BF16)
    before = _dot(ohb, tri_ref[...])
    base = base_ref[...]
    base_t = jnp.concatenate([base] * (tile // META_W), axis=1)
    rank = jnp.sum(jnp.where(onehot, before + base_t, 0.0), axis=0, keepdims=True)
    new_base = base + _dot(ohb, jnp.ones((tile, META_W), BF16))
    base_ref[...] = new_base
    cout_ref[...] = new_base

    meta_t = jnp.concatenate([w_lo, w_hi, cls.astype(F32), rank, jnp.zeros((META_W - 4, tile), F32)], axis=0)
    x1e_ref[:, D_MODEL:ROW_W] = meta_t.T
    mrow_ref[...] = meta_t[0:8]


def _const_spec(shape):
    nd = len(shape)
    return pl.BlockSpec(shape, lambda i, _nd=nd: (0,) * _nd, pipeline_mode=pl.Buffered(1))


def _mixer(x, y_b, counts_in, lw, seq):
    n_tok = x.shape[0]
    n_tiles = n_tok // TILE
    hb = TILE // HALO
    n_hblk = n_tok // HALO
    consts = [lw["w_in"], lw["b_in"], lw["pool_w"], lw["pool_scale"], lw["conv_w"], lw["conv_b"],
              lw["sgu_ln_g"], lw["sgu_ln_b"], lw["sgu_w"], lw["sgu_b"], lw["branch_proj"], lw["w_out"],
              lw["ln1_g"], lw["ln1_b"], lw["w_r"], lw["b_r"], lw["tri"]]
    cur = lambda i: jnp.minimum(i, n_tiles - 1)
    return pl.pallas_call(
        functools.partial(_mixer_kernel, seq=seq, n_tiles=n_tiles),
        grid=(n_tiles + 1,),
        in_specs=[
            pl.BlockSpec((TILE, D_MODEL), lambda i: (cur(i), 0)),
            pl.BlockSpec((HALO, D_MODEL), lambda i: (jnp.maximum(cur(i) * hb - 1, 0), 0)),
            pl.BlockSpec((HALO, D_MODEL), lambda i: (jnp.minimum((cur(i) + 1) * hb, n_hblk - 1), 0)),
            pl.BlockSpec((TILE, BRANCH_W), lambda i: (cur(i), 0)),
            pl.BlockSpec(COUNT_SHAPE, lambda i: (0, 0)),
        ] + [_const_spec(c.shape) for c in consts],
        out_specs=[
            pl.BlockSpec((TILE, ROW_W), lambda i: (jnp.maximum(i - 1, 0), 0)),
            pl.BlockSpec((None, 8, TILE), lambda i: (jnp.maximum(i - 1, 0), 0, 0)),
            pl.BlockSpec(COUNT_SHAPE, lambda i: (0, 0)),
        ],
        out_shape=[
            jax.ShapeDtypeStruct((n_tok, ROW_W), F32),
            jax.ShapeDtypeStruct((n_tiles, 8, TILE), F32),
            jax.ShapeDtypeStruct(COUNT_SHAPE, F32),
        ],
        scratch_shapes=[pltpu.VMEM(COUNT_SHAPE, F32), pltpu.VMEM((TILE, D_MODEL), F32)],
        compiler_params=pltpu.CompilerParams(
            dimension_semantics=("arbitrary",), vmem_limit_bytes=VMEM_LIMIT),
        name="mixer",
    )(x, x, x, y_b, counts_in, *consts)


def _cumsum_sublanes(a):
    n = a.shape[0]
    row = lax.broadcasted_iota(jnp.int32, a.shape, 0)
    s = 1
    while s < n:
        a = a + jnp.where(row >= s, pltpu.roll(a, s, 0), 0.0)
        s *= 2
    return a


def _tables_kernel(cnt_ref, tab_ref, start_ref, *, nbp):
    cnt = cnt_ref[...]
    nb = jnp.floor((cnt + (MOE_BLOCK - 1)) * (1.0 / MOE_BLOCK))
    end = _cumsum_sublanes(nb)
    start_ref[...] = (end - nb) * MOE_BLOCK
    end_t = jnp.concatenate([end] * (nbp // META_W), axis=1)
    j = lax.broadcasted_iota(jnp.int32, (N_CLASS_IDS, nbp), 1).astype(F32)
    blk_cls = jnp.sum((end_t <= j).astype(F32), axis=0, keepdims=True)
    total = end_t[N_CLASS_IDS - 1:N_CLASS_IDS, :]
    jr = j[0:1, :]
    active = jr < total
    last_cls = jnp.max(jnp.where(active, blk_cls, 0.0), axis=-1, keepdims=True)
    cls_i = jnp.where(active, blk_cls, last_cls).astype(jnp.int32)
    g8 = (cls_i >> 6) * EXPERTS_PER_GROUP
    e_lo = g8 + ((cls_i >> 3) & 7)
    e_hi = g8 + (cls_i & 7)
    bidx = jnp.minimum(jr, total - 1.0).astype(jnp.int32)
    match = lax.broadcasted_iota(jnp.int32, (N_CLASS_IDS, nbp), 0) == cls_i
    reps = nbp // META_W
    cnt_j = jnp.sum(jnp.where(match, jnp.concatenate([cnt] * reps, axis=1), 0.0), axis=0, keepdims=True)
    first_j = jnp.sum(jnp.where(match, jnp.concatenate([end - nb] * reps, axis=1), 0.0), axis=0, keepdims=True)
    n_valid = jnp.where(active, jnp.clip(cnt_j - (jr - first_j) * MOE_BLOCK, 0.0, MOE_BLOCK), 0.0).astype(jnp.int32)
    zero = jnp.zeros_like(bidx)
    tab_ref[...] = jnp.concatenate([e_lo, e_hi, bidx, total.astype(jnp.int32), n_valid, zero, zero, zero], axis=0)


def _tables(counts, nbp):
    return pl.pallas_call(
        functools.partial(_tables_kernel, nbp=nbp),
        out_shape=[jax.ShapeDtypeStruct((8, nbp), jnp.int32),
                   jax.ShapeDtypeStruct(COUNT_SHAPE, F32)],
        name="moe_tables",
    )(counts)


def _dest_kernel(mrow_ref, start_ref, dest_ref, *, tiles):
    start = jnp.concatenate([start_ref[...]] * (TILE // META_W), axis=1)
    cls_ids = lax.broadcasted_iota(jnp.int32, (N_CLASS_IDS, TILE), 0)
    for t in range(tiles):
        cls = mrow_ref[t, 2:3, :].astype(jnp.int32)
        first = jnp.sum(jnp.where(cls_ids == cls, start, 0.0), axis=0, keepdims=True)
        dest_ref[t] = (first + mrow_ref[t, 3:4, :]).astype(jnp.int32)


def _dest(mrow, start):
    n_tiles = mrow.shape[0]
    tiles = math.gcd(n_tiles, DEST_TILES)
    return pl.pallas_call(
        functools.partial(_dest_kernel, tiles=tiles),
        grid=(n_tiles // tiles,),
        in_specs=[pl.BlockSpec((tiles, 8, TILE), lambda i: (i, 0, 0)),
                  pl.BlockSpec(COUNT_SHAPE, lambda i: (0, 0))],
        out_specs=pl.BlockSpec((tiles, 1, TILE), lambda i: (i, 0, 0)),
        out_shape=jax.ShapeDtypeStruct((n_tiles, 1, TILE), jnp.int32),
        compiler_params=pltpu.CompilerParams(dimension_semantics=("arbitrary",)),
        name="moe_dest",
    )(mrow, start)


def _sc_mesh():
    return plsc.VectorSubcoreMesh(core_axis_name="core", subcore_axis_name="subcore")


def _col_chunks(width):
    chunks, c = [], 0
    while c < width:
        w = min(SC_COLS, width - c)
        assert c % w == 0
        chunks.append((c // w, w))
        c += w
    return chunks


_SC_PARAMS = dict(core_axis_name=("core", "subcore"), dimension_semantics=(pltpu.PARALLEL,))


def _sc_scatter_rows(idx, src, n_out):
    n, d = src.shape

    @pl.kernel(out_type=jax.ShapeDtypeStruct((n_out, d), src.dtype), mesh=_sc_mesh(), scratch_types=[],
               compiler_params=pltpu.CompilerParams(use_tc_tiling_on_sc=True))
    def scatter(src_hbm, idx_hbm, out_hbm):
        for cb, cw in _col_chunks(d):
            def body(rows_vmem, idx_vmem, cb=cb, cw=cw):
                pltpu.sync_copy(rows_vmem, out_hbm.at[:, pl.ds(cb * cw, cw)].at[idx_vmem.at[0]])

            pltpu.emit_pipeline(
                body, grid=(n // SC_WINDOW,),
                in_specs=[pl.BlockSpec((SC_WINDOW, cw), lambda i, cb=cb: (i, cb)),
                          pl.BlockSpec((1, SC_WINDOW), lambda i: (0, i))],
                out_specs=[], **_SC_PARAMS)(src_hbm, idx_hbm)

    return scatter(src, idx)


def _sc_gather_rows(idx, src):
    n, d = idx.shape[1], src.shape[1]

    @pl.kernel(out_type=jax.ShapeDtypeStruct((n, d), src.dtype), mesh=_sc_mesh(), scratch_types=[],
               compiler_params=pltpu.CompilerParams(use_tc_tiling_on_sc=True))
    def gather(src_hbm, idx_hbm, out_hbm):
        for cb, cw in _col_chunks(d):
            def body(idx_vmem, rows_vmem, cb=cb, cw=cw):
                pltpu.sync_copy(src_hbm.at[:, pl.ds(cb * cw, cw)].at[idx_vmem.at[0]], rows_vmem)

            pltpu.emit_pipeline(
                body, grid=(n // SC_WINDOW,),
                in_specs=[pl.BlockSpec((1, SC_WINDOW), lambda i: (0, i))],
                out_specs=[pl.BlockSpec((SC_WINDOW, cw), lambda i, cb=cb: (i, cb))],
                **_SC_PARAMS)(idx_hbm, out_hbm)

    return gather(src, idx)


def _expert_kernel(elo_ref, ehi_ref, bidx_ref, nvalid_ref, tot_ref,
                   xb_ref, w1lo_ref, w3lo_ref, w2lo_ref, w1hi_ref, w3hi_ref, w2hi_ref,
                   g_ref, b_ref, o_ref, pre_ref):
    del elo_ref, ehi_ref, bidx_ref
    j = pl.program_id(0)
    total = tot_ref[0]

    @pl.when(j == 0)
    def _():
        pre_ref[...] = jnp.zeros_like(pre_ref)

    def finish_previous():
        o_ref[...] = _layer_norm(pre_ref[...], g_ref[...], b_ref[...])

    def hidden(half_a, b):
        return ((half_a * (jnp.tanh(half_a) + 1.0)) * b).astype(BF16)

    @pl.when(j < total)
    def _():
        n_valid = nvalid_ref[j]
        x = jnp.where(lax.broadcasted_iota(jnp.int32, (MOE_BLOCK, D_MODEL), 0) < n_valid, xb_ref[:, 0:D_MODEL], 0.0)
        xh = x.astype(BF16)
        half_a_lo = _dot(xh, w1lo_ref[...])
        b_lo = _dot(xh, w3lo_ref[...])
        finish_previous()
        half_a_hi = _dot(xh, w1hi_ref[...])
        b_hi = _dot(xh, w3hi_ref[...])
        y_lo = _dot(hidden(half_a_lo, b_lo), w2lo_ref[...])
        y_hi = _dot(hidden(half_a_hi, b_hi), w2hi_ref[...])
        live = lax.broadcasted_iota(jnp.int32, (MOE_BLOCK, 1), 0) < n_valid
        w_lo = jnp.where(live, xb_ref[:, D_MODEL:D_MODEL + 1], 0.0)
        w_hi = jnp.where(live, xb_ref[:, D_MODEL + 1:D_MODEL + 2], 0.0)
        pre_ref[...] = DN_ALPHA * x + (w_lo * y_lo + w_hi * y_hi)

    @pl.when(j == total)
    def _():
        finish_previous()

    @pl.when(j > total)
    def _():
        o_ref[...] = jnp.zeros_like(o_ref)


def _experts(tab, xb, lw):
    n_blocks = xb.shape[0] // MOE_BLOCK
    up = lambda sel: pl.BlockSpec((None, D_MODEL, D_EXPERT), lambda j, lo, hi, bi, nv, tot: ((lo, hi)[sel][j], 0, 0))
    down = lambda sel: pl.BlockSpec((None, D_EXPERT, D_MODEL), lambda j, lo, hi, bi, nv, tot: ((lo, hi)[sel][j], 0, 0))
    vec = pl.BlockSpec((1, D_MODEL), lambda j, lo, hi, bi, nv, tot: (0, 0))
    grid_spec = pltpu.PrefetchScalarGridSpec(
        num_scalar_prefetch=5,
        grid=(n_blocks + 1,),
        in_specs=[pl.BlockSpec((MOE_BLOCK, ROW_W), lambda j, lo, hi, bi, nv, tot: (bi[j], 0)),
                  up(0), up(0), down(0), up(1), up(1), down(1), vec, vec],
        out_specs=pl.BlockSpec((MOE_BLOCK, D_MODEL), lambda j, lo, hi, bi, nv, tot: (jnp.maximum(j - 1, 0), 0)),
        scratch_shapes=[pltpu.VMEM((MOE_BLOCK, D_MODEL), F32)],
    )
    return pl.pallas_call(
        _expert_kernel,
        grid_spec=grid_spec,
        out_shape=jax.ShapeDtypeStruct((n_blocks * MOE_BLOCK, D_MODEL), F32),
        compiler_params=pltpu.CompilerParams(
            dimension_semantics=("arbitrary",), vmem_limit_bytes=VMEM_LIMIT),
        name="moe_experts",
    )(tab[0], tab[1], tab[2], tab[4], tab[3, 0:1], xb,
      lw["w1"], lw["w3"], lw["w2"], lw["w1"], lw["w3"], lw["w2"], lw["ln2_g"], lw["ln2_b"])


def _layer_weights(p, l):
    row = lambda a: a[l].reshape(1, -1).astype(F32)
    w_r = jnp.zeros((META_W, D_MODEL), F32)
    w_r = w_r.at[0:N_EXPERT_GROUPS].set(p["w_rg"][l].T).at[8:8 + N_EXPERTS].set(p["w_re"][l].T)
    b_r = jnp.zeros((META_W, 1), F32)
    b_r = b_r.at[0:N_EXPERT_GROUPS, 0].set(p["b_rg"][l]).at[N_EXPERT_GROUPS:8, 0].set(NEG_BIG)
    b_r = b_r.at[8:8 + N_EXPERTS, 0].set(p["b_re"][l])
    t = np.arange(TILE)
    gate_half = jnp.asarray(np.where(np.arange(p["w_in"].shape[-1]) >= OFF_G, 0.5, 1.0), F32)
    return {
        "w_in": (p["w_in"][l] * gate_half).astype(BF16),
        "b_in": row(p["b_in"]) * gate_half,
        "w_b": p["w_in"][l][:, OFF_B:OFF_B + BRANCH_W].astype(BF16),
        "b_b": p["b_in"][l][OFF_B:OFF_B + BRANCH_W].reshape(1, -1),
        "pool_w": jax.scipy.linalg.block_diag(*[p["pool_w"][l][g] for g in range(N_GROUPS)]).astype(BF16),
        "pool_scale": row(p["pool_scale"]),
        "conv_w": p["conv_w"][l],
        "conv_b": row(p["conv_b"]),
        "sgu_ln_g": row(p["sgu_ln_g"]),
        "sgu_ln_b": row(p["sgu_ln_b"]),
        "sgu_w": p["sgu_w"][l].astype(BF16),
        "sgu_b": jnp.repeat(p["sgu_b"][l].T, GROUP_W, axis=1),
        "branch_proj": (0.5 * p["branch_proj"][l]).astype(BF16),
        "w_out": p["w_out"][l].astype(BF16),
        "ln1_g": row(p["ln1_g"]),
        "ln1_b": row(p["ln1_b"]),
        "w_r": w_r,
        "b_r": b_r,
        "tri": jnp.asarray((t[:, None] < t[None, :]).astype(np.float32)).astype(BF16),
        "w1": (0.5 * p["w1"][l]).astype(BF16),
        "w3": p["w3"][l].astype(BF16),
        "w2": p["w2"][l].astype(BF16),
        "ln2_g": row(p["ln2_g"]),
        "ln2_b": row(p["ln2_b"]),
    }


def _encoder_layer(xs, seqs, lw):
    zero_counts = jnp.zeros(COUNT_SHAPE, F32)
    routed = []
    for x, seq in zip(xs, seqs):
        n_blocks = x.shape[0] // MOE_BLOCK + N_REACHABLE_CLASSES
        nbp = -(-(n_blocks + 1) // META_W) * META_W
        y_b = _fourier_mixer(x, lw["w_b"], lw["b_b"], seq)
        x1e, mrow, counts = _mixer(x, y_b, zero_counts, lw, seq)
        tab, start = _tables(counts, nbp)
        dest = _dest(mrow, start).reshape(1, -1)
        routed.append((tab, dest, _sc_scatter_rows(dest, x1e, n_blocks * MOE_BLOCK)))
    return [_sc_gather_rows(dest, _experts(tab, xb, lw)) for tab, dest, xb in routed]


def kernel(x_prompt, x_sample, w_in, b_in, pool_w, pool_scale, conv_w, conv_b, sgu_ln_g, sgu_ln_b,
           sgu_w, sgu_b, branch_proj, w_out, ln1_g, ln1_b, w_rg, b_rg, w_re, b_re, w1, w3, w2,
           ln2_g, ln2_b):
    p = dict(w_in=w_in, b_in=b_in, pool_w=pool_w, pool_scale=pool_scale, conv_w=conv_w, conv_b=conv_b,
             sgu_ln_g=sgu_ln_g, sgu_ln_b=sgu_ln_b, sgu_w=sgu_w, sgu_b=sgu_b, branch_proj=branch_proj,
             w_out=w_out, ln1_g=ln1_g, ln1_b=ln1_b, w_rg=w_rg, b_rg=b_rg, w_re=w_re, b_re=b_re,
             w1=w1, w3=w3, w2=w2, ln2_g=ln2_g, ln2_b=ln2_b)
    shapes = (x_prompt.shape, x_sample.shape)
    seqs = [s[1] for s in shapes]
    xs = [x_prompt.reshape(-1, D_MODEL), x_sample.reshape(-1, D_MODEL)]
    for l in range(w_in.shape[0]):
        xs = _encoder_layer(xs, seqs, _layer_weights(p, l))
    return tuple(x.reshape(s) for x, s in zip(xs, shapes))
```

```python
import functools
import math

import numpy as np
import jax
import jax.numpy as jnp
from jax import lax
from jax.experimental import pallas as pl
from jax.experimental.pallas import tpu as pltpu
from jax.experimental.pallas import tpu_sc as plsc

F32 = jnp.float32
BF16 = jnp.bfloat16

D_MODEL = 1024
DEPTH = 2
BRANCH_W = 256
N_GROUPS = 4
GROUP_W = 64
POOL_WINDOWS = (2, 4, 8, 16)
CHUNK = 128
OFF_A = 0
OFF_B = 256
OFF_CH = 512
OFF_DU = 1280
OFF_G = 1792
N_EXPERT_GROUPS = 4
EXPERTS_PER_GROUP = 8
N_EXPERTS = 32
D_EXPERT = 512
MOE_BLOCK = 256
DN_ALPHA = (2 * DEPTH) ** 0.25
LN_EPS = 1e-5

N_CLASS_IDS = 256
N_REACHABLE_CLASSES = N_EXPERT_GROUPS * (EXPERTS_PER_GROUP * (EXPERTS_PER_GROUP - 1) // 2)
META_W = 128
ROW_W = D_MODEL + META_W
COUNT_SHAPE = (N_CLASS_IDS, META_W)
HALO = 16
TILE = 512
DEST_TILES = 8
DMA_SPLIT = 4
SC_WINDOW = 128
SC_COLS = 256
NEG_BIG = -1e30
VMEM_LIMIT = 56 * 1024 * 1024


def _dot(a, b):
    return jnp.dot(a, b, preferred_element_type=F32)


def _layer_norm(x, g, b):
    mu = jnp.mean(x, axis=-1, keepdims=True)
    xc = x - mu
    var = jnp.mean(xc * xc, axis=-1, keepdims=True)
    return xc * lax.rsqrt(var + LN_EPS) * g + b


def _gelu_tanh(x):
    return 0.5 * x * (1.0 + jnp.tanh(math.sqrt(2.0 / math.pi) * (x + 0.044715 * (x * x * x))))


def _fft_factors(seq):
    n1 = 1 << (int(math.log2(seq)) // 2)
    return n1, seq // n1


@functools.lru_cache(maxsize=None)
def _fft_tables(seq):
    n1, n2 = _fft_factors(seq)
    c = np.arange(GROUP_W)
    ang = 2.0 * np.pi * ((c[:, None] * c[None, :]) % GROUP_W) / GROUP_W
    eye = np.eye(N_GROUPS)
    cs = np.concatenate([np.kron(eye, np.cos(ang)), np.kron(eye, np.sin(ang))], axis=1)
    k1 = np.arange(n1)
    t1 = np.arange(n1)
    t2 = np.arange(n2)
    t = t1[None, None, :] * n2 + t2[:, None, None]
    ang1 = 2.0 * np.pi * ((k1[None, :, None] * t) % seq) / seq
    gc, gs = np.cos(ang1), np.sin(ang1)
    lt = np.concatenate([np.concatenate([gc, -gs], axis=2), np.concatenate([-gs, -gc], axis=2)], axis=1)
    k2 = np.arange(n2)
    ang2 = 2.0 * np.pi * ((k2[:, None] * t2[None, :]) % n2) / n2
    scale = 1.0 / math.sqrt(seq * GROUP_W)
    to16 = lambda a: jnp.asarray(a, dtype=F32).astype(BF16)
    return to16(cs), to16(lt), to16(np.cos(ang2) * scale), to16(np.sin(ang2) * scale)


def _fft_stage1_kernel(*refs, n1, tb):
    x_refs, (wb_ref, bb_ref, cs_ref, lt_ref, o_ref, u_ref) = refs[:DMA_SPLIT], refs[DMA_SPLIT:]
    x = jnp.concatenate([r[...].reshape(n1 * tb, D_MODEL // DMA_SPLIT).astype(BF16) for r in x_refs], axis=1)
    zb = _dot(x, wb_ref[...]) + bb_ref[...]
    u = _dot(zb.astype(BF16), cs_ref[...])
    for c in range(4):
        u_ref[c] = u[:, c * 128:(c + 1) * 128]
    for j in range(tb):
        q = [u_ref[c, pl.ds(j, n1, stride=tb), :] for c in range(4)]
        stacked = jnp.concatenate([jnp.concatenate(q[0:2], axis=1),
                                   jnp.concatenate(q[2:4], axis=1)], axis=0).astype(BF16)
        b = _dot(lt_ref[j], stacked)
        o_ref[:, j * 512:j * 512 + BRANCH_W] = b[:n1].astype(BF16)
        o_ref[:, j * 512 + BRANCH_W:(j + 1) * 512] = b[n1:].astype(BF16)


def _fft_stage2_kernel(b_ref, c2_ref, s2_ref, o_ref, *, kb):
    for k in range(kb):
        slab = b_ref[k]
        y = _dot(c2_ref[...], slab[:, :BRANCH_W]) + _dot(s2_ref[...], slab[:, BRANCH_W:])
        o_ref[:, k * BRANCH_W:(k + 1) * BRANCH_W] = y.astype(BF16)


def _fourier_mixer(x, w_b, b_b, seq):
    n_tok = x.shape[0]
    bsz = n_tok // seq
    n1, n2 = _fft_factors(seq)
    tb = max(8, min(16, (4 << 20) // (n1 * D_MODEL * 4)))
    kb = min(n1, (2 << 20) // (n2 * 512 * 2))
    cs, lt, c2, s2 = _fft_tables(seq)
    x4 = x.reshape(bsz, n1, n2, D_MODEL)
    xw = D_MODEL // DMA_SPLIT
    bt = pl.pallas_call(
        functools.partial(_fft_stage1_kernel, n1=n1, tb=tb),
        grid=(bsz, n2 // tb),
        in_specs=[pl.BlockSpec((None, n1, tb, xw), lambda b, j, c=c: (b, 0, j, c)) for c in range(DMA_SPLIT)] + [
            pl.BlockSpec((D_MODEL, BRANCH_W), lambda b, j: (0, 0)),
            pl.BlockSpec((1, BRANCH_W), lambda b, j: (0, 0)),
            pl.BlockSpec((BRANCH_W, 2 * BRANCH_W), lambda b, j: (0, 0)),
            pl.BlockSpec((tb, 2 * n1, 2 * n1), lambda b, j: (j, 0, 0)),
        ],
        out_specs=pl.BlockSpec((None, n1, tb * 512), lambda b, j: (b, 0, j)),
        out_shape=jax.ShapeDtypeStruct((bsz, n1, n2 * 512), BF16),
        scratch_shapes=[pltpu.VMEM((4, n1 * tb, 128), F32)],
        compiler_params=pltpu.CompilerParams(
            dimension_semantics=("arbitrary", "arbitrary"), vmem_limit_bytes=VMEM_LIMIT),
        name="fft_stage1",
    )(*([x4] * DMA_SPLIT), w_b, b_b, cs, lt)
    bt4 = bt.reshape(bsz, n1, n2, 512)
    y = pl.pallas_call(
        functools.partial(_fft_stage2_kernel, kb=kb),
        grid=(bsz, n1 // kb),
        in_specs=[
            pl.BlockSpec((None, kb, n2, 512), lambda b, i: (b, i, 0, 0)),
            pl.BlockSpec((n2, n2), lambda b, i: (0, 0)),
            pl.BlockSpec((n2, n2), lambda b, i: (0, 0)),
        ],
        out_specs=pl.BlockSpec((None, n2, kb * BRANCH_W), lambda b, i: (b, 0, i)),
        out_shape=jax.ShapeDtypeStruct((bsz, n2, n1 * BRANCH_W), BF16),
        compiler_params=pltpu.CompilerParams(
            dimension_semantics=("arbitrary", "arbitrary"), vmem_limit_bytes=VMEM_LIMIT),
        name="fft_stage2",
    )(bt4, c2, s2)
    return y.reshape(n_tok, BRANCH_W)


def _mixer_kernel(x_ref, xp_ref, xn_ref, yb_ref, cin_ref,
                  win_ref, bin_ref, poolw_ref, pools_ref, convw_ref, convb_ref,
                  lng_ref, lnb_ref, sguw_ref, sgub_ref, bp_ref, wout_ref,
                  ln1g_ref, ln1b_ref, wr_ref, br_ref, tri_ref,
                  x1e_ref, mrow_ref, cout_ref, base_ref, pre_ref, *, seq, n_tiles):
    i = pl.program_id(0)
    tile = TILE
    ext = tile + 2 * HALO
    p0 = (jnp.minimum(i, n_tiles - 1) % (seq // tile)) * tile

    @pl.when(i == 0)
    def _():
        base_ref[...] = cin_ref[...]
        pre_ref[...] = jnp.zeros_like(pre_ref)

    live = i > 0

    x = x_ref[...]
    xc = x.astype(BF16)
    xe = jnp.concatenate([xp_ref[...].astype(BF16), xc, xn_ref[...].astype(BF16)], axis=0)
    pos = p0 - HALO + lax.broadcasted_iota(jnp.int32, (ext, BRANCH_W), 0)
    valid = (pos >= 0) & (pos < seq)
    lane = lax.broadcasted_iota(jnp.int32, (ext, BRANCH_W), 1)
    grp = lane // GROUP_W

    def half_gate_logits(k):
        lo = OFF_G + k * D_MODEL
        return _dot(xc, win_ref[:, lo:lo + D_MODEL]) + bin_ref[:, lo:lo + D_MODEL]

    def gated(half_gz, y_k, k):
        return (jnp.tanh(half_gz) + 1.0) * _dot(y_k.astype(BF16), bp_ref[k])

    za = jnp.where(valid, _dot(xe, win_ref[:, OFF_A:OFF_A + BRANCH_W]) + bin_ref[:, OFF_A:OFF_A + BRANCH_W], 0.0)
    zc = _dot(xe, win_ref[:, OFF_CH:OFF_DU]) + bin_ref[:, OFF_CH:OFF_DU]
    zd = _dot(xc, win_ref[:, OFF_DU:OFF_G]) + bin_ref[:, OFF_DU:OFF_G]
    hg0 = half_gate_logits(0)

    xh, xl = _tail_norm(pre_ref[...], ln1g_ref, ln1b_ref, x1e_ref)

    s2 = za + pltpu.roll(za, 1, 0)
    s4 = s2 + pltpu.roll(s2, 2, 0)
    s8 = s4 + pltpu.roll(s4, 4, 0)
    s16 = s8 + pltpu.roll(s8, 8, 0)
    c4 = pltpu.roll(s4, ext - 1, 0)
    c8 = pltpu.roll(s8, ext - 3, 0)
    c16 = pltpu.roll(s16, ext - 7, 0)
    wsum = jnp.where(grp == 0, s2, jnp.where(grp == 1, c4, jnp.where(grp == 2, c8, c16)))
    half = jnp.where(grp == 0, 1, jnp.where(grp == 1, 2, jnp.where(grp == 2, 4, 8)))
    cnt = jnp.minimum(pos + half, seq) - jnp.maximum(pos - half, 0)
    pooled = (wsum / jnp.maximum(cnt, 1).astype(F32) - za)[HALO:HALO + tile]
    y_a = _dot(pooled.astype(BF16), poolw_ref[...]) * pools_ref[...]
    hg1 = half_gate_logits(1)

    h = zc[:, 0:BRANCH_W]
    gate_b = zc[:, BRANCH_W:2 * BRANCH_W]
    gate_c = zc[:, 2 * BRANCH_W:3 * BRANCH_W]
    q = jnp.where(valid, gate_c * h, 0.0)
    q_prev = pltpu.roll(q, 1, 0)[HALO:HALO + tile]
    q_next = pltpu.roll(q, ext - 1, 0)[HALO:HALO + tile]
    conv = (q_prev * convw_ref[0:1, :] + q[HALO:HALO + tile] * convw_ref[1:2, :]
            + q_next * convw_ref[2:3, :] + convb_ref[...])
    y_c = gate_b[HALO:HALO + tile] * conv
    merged = gated(hg0, y_a, 0)
    hg2 = half_gate_logits(2)

    w_lo, w_hi, cls = _tail_route(xh, xl, wr_ref, br_ref)

    u = _gelu_tanh(zd[:, 0:BRANCH_W])
    v = _layer_norm(_gelu_tanh(zd[:, BRANCH_W:]), lng_ref[...], lnb_ref[...]).astype(BF16)
    grp_c = lax.broadcasted_iota(jnp.int32, (CHUNK, BRANCH_W), 1) // GROUP_W
    sps = []
    for c in range(tile // CHUNK):
        vch = v[c * CHUNK:(c + 1) * CHUNK]
        sp = None
        for g in range(N_GROUPS):
            r = _dot(sguw_ref[g], vch)
            sp = r if sp is None else jnp.where(grp_c == g, r, sp)
        sps.append(sp + sgub_ref[...])
    y_d = u * jnp.concatenate(sps, axis=0)
    merged = merged + gated(hg1, yb_ref[...], 1)
    hg3 = half_gate_logits(3)

    _tail_rank(w_lo, w_hi, cls, live, tri_ref, x1e_ref, mrow_ref, cout_ref, base_ref)

    merged = merged + gated(hg2, y_c, 2)
    merged = merged + gated(hg3, y_d, 3)
    pre_ref[...] = DN_ALPHA * x + _dot(merged.astype(BF16), wout_ref[...])


def _tail_norm(pre, ln1g_ref, ln1b_ref, x1e_ref):
    x1 = _layer_norm(pre, ln1g_ref[...], ln1b_ref[...])
    x1e_ref[:, 0:D_MODEL] = x1
    xh = x1.astype(BF16)
    return xh, (x1 - xh.astype(F32)).astype(BF16)


def _tail_route(xh, xl, wr_ref, br_ref):
    tile = TILE
    wr = wr_ref[...]
    wh = wr.astype(BF16)
    wl = (wr - wh.astype(F32)).astype(BF16)
    nt = (((1,), (1,)), ((), ()))
    dot_nt = lambda a, b: lax.dot_general(a, b, nt, preferred_element_type=F32)
    logits = dot_nt(wh, xh) + dot_nt(wh, xl) + dot_nt(wl, xh) + br_ref[...]
    row8 = lax.broadcasted_iota(jnp.int32, (EXPERTS_PER_GROUP, tile), 0)
    lg = logits[0:8]
    m = jnp.max(lg, axis=0, keepdims=True)
    g_idx = jnp.min(jnp.where(lg == m, row8, 8), axis=0, keepdims=True)
    p_group = 1.0 / jnp.sum(jnp.exp(lg - m), axis=0, keepdims=True)
    le = logits[8:16]
    for g in range(1, N_EXPERT_GROUPS):
        le = jnp.where(g_idx == g, logits[8 + 8 * g:16 + 8 * g], le)
    ex = jnp.exp(le - jnp.max(le, axis=0, keepdims=True))
    pe = ex / jnp.sum(ex, axis=0, keepdims=True)
    p1 = jnp.max(pe, axis=0, keepdims=True)
    i1 = jnp.min(jnp.where(pe == p1, row8, 8), axis=0, keepdims=True)
    pe2 = jnp.where(row8 == i1, -1.0, pe)
    p2 = jnp.max(pe2, axis=0, keepdims=True)
    i2 = jnp.min(jnp.where(pe2 == p2, row8, 8), axis=0, keepdims=True)
    first_lo = i1 < i2
    w_lo = p_group * jnp.where(first_lo, p1, p2)
    w_hi = p_group * jnp.where(first_lo, p2, p1)
    cls = g_idx * 64 + jnp.minimum(i1, i2) * EXPERTS_PER_GROUP + jnp.maximum(i1, i2)
    return w_lo, w_hi, cls


def _tail_rank(w_lo, w_hi, cls, live, tri_ref, x1e_ref, mrow_ref, cout_ref, base_ref):
    tile = TILE
    onehot = jnp.logical_and(lax.broadcasted_iota(jnp.int32, (N_CLASS_IDS, tile), 0) == cls, live)
    ohb = onehot.astype(BF16)
    before = _dot(ohb, tri_ref[...])
    base = base_ref[...]
    base_t = jnp.concatenate([base] * (tile // META_W), axis=1)
    rank = jnp.sum(jnp.where(onehot, before + base_t, 0.0), axis=0, keepdims=True)
    new_base = base + _dot(ohb, jnp.ones((tile, META_W), BF16))
    base_ref[...] = new_base
    cout_ref[...] = new_base

    meta_t = jnp.concatenate([w_lo, w_hi, cls.astype(F32), rank, jnp.zeros((META_W - 4, tile), F32)], axis=0)
    x1e_ref[:, D_MODEL:ROW_W] = meta_t.T
    mrow_ref[...] = meta_t[0:8]


def _const_spec(shape):
    nd = len(shape)
    return pl.BlockSpec(shape, lambda i, _nd=nd: (0,) * _nd, pipeline_mode=pl.Buffered(1))


def _mixer(x, y_b, counts_in, lw, seq):
    n_tok = x.shape[0]
    n_tiles = n_tok // TILE
    hb = TILE // HALO
    n_hblk = n_tok // HALO
    consts = [lw["w_in"], lw["b_in"], lw["pool_w"], lw["pool_scale"], lw["conv_w"], lw["conv_b"],
              lw["sgu_ln_g"], lw["sgu_ln_b"], lw["sgu_w"], lw["sgu_b"], lw["branch_proj"], lw["w_out"],
              lw["ln1_g"], lw["ln1_b"], lw["w_r"], lw["b_r"], lw["tri"]]
    cur = lambda i: jnp.minimum(i, n_tiles - 1)
    return pl.pallas_call(
        functools.partial(_mixer_kernel, seq=seq, n_tiles=n_tiles),
        grid=(n_tiles + 1,),
        in_specs=[
            pl.BlockSpec((TILE, D_MODEL), lambda i: (cur(i), 0)),
            pl.BlockSpec((HALO, D_MODEL), lambda i: (jnp.maximum(cur(i) * hb - 1, 0), 0)),
            pl.BlockSpec((HALO, D_MODEL), lambda i: (jnp.minimum((cur(i) + 1) * hb, n_hblk - 1), 0)),
            pl.BlockSpec((TILE, BRANCH_W), lambda i: (cur(i), 0)),
            pl.BlockSpec(COUNT_SHAPE, lambda i: (0, 0)),
        ] + [_const_spec(c.shape) for c in consts],
        out_specs=[
            pl.BlockSpec((TILE, ROW_W), lambda i: (jnp.maximum(i - 1, 0), 0)),
            pl.BlockSpec((None, 8, TILE), lambda i: (jnp.maximum(i - 1, 0), 0, 0)),
            pl.BlockSpec(COUNT_SHAPE, lambda i: (0, 0)),
        ],
        out_shape=[
            jax.ShapeDtypeStruct((n_tok, ROW_W), F32),
            jax.ShapeDtypeStruct((n_tiles, 8, TILE), F32),
            jax.ShapeDtypeStruct(COUNT_SHAPE, F32),
        ],
        scratch_shapes=[pltpu.VMEM(COUNT_SHAPE, F32), pltpu.VMEM((TILE, D_MODEL), F32)],
        compiler_params=pltpu.CompilerParams(
            dimension_semantics=("arbitrary",), vmem_limit_bytes=VMEM_LIMIT),
        name="mixer",
    )(x, x, x, y_b, counts_in, *consts)


def _cumsum_sublanes(a):
    n = a.shape[0]
    row = lax.broadcasted_iota(jnp.int32, a.shape, 0)
    s = 1
    while s < n:
        a = a + jnp.where(row >= s, pltpu.roll(a, s, 0), 0.0)
        s *= 2
    return a


def _tables_kernel(cnt_ref, tab_ref, start_ref, *, nbp):
    cnt = cnt_ref[...]
    nb = jnp.floor((cnt + (MOE_BLOCK - 1)) * (1.0 / MOE_BLOCK))
    end = _cumsum_sublanes(nb)
    start_ref[...] = (end - nb) * MOE_BLOCK
    end_t = jnp.concatenate([end] * (nbp // META_W), axis=1)
    j = lax.broadcasted_iota(jnp.int32, (N_CLASS_IDS, nbp), 1).astype(F32)
    blk_cls = jnp.sum((end_t <= j).astype(F32), axis=0, keepdims=True)
    total = end_t[N_CLASS_IDS - 1:N_CLASS_IDS, :]
    jr = j[0:1, :]
    active = jr < total
    last_cls = jnp.max(jnp.where(active, blk_cls, 0.0), axis=-1, keepdims=True)
    cls_i = jnp.where(active, blk_cls, last_cls).astype(jnp.int32)
    g8 = (cls_i >> 6) * EXPERTS_PER_GROUP
    e_lo = g8 + ((cls_i >> 3) & 7)
    e_hi = g8 + (cls_i & 7)
    bidx = jnp.minimum(jr, total - 1.0).astype(jnp.int32)
    match = lax.broadcasted_iota(jnp.int32, (N_CLASS_IDS, nbp), 0) == cls_i
    reps = nbp // META_W
    cnt_j = jnp.sum(jnp.where(match, jnp.concatenate([cnt] * reps, axis=1), 0.0), axis=0, keepdims=True)
    first_j = jnp.sum(jnp.where(match, jnp.concatenate([end - nb] * reps, axis=1), 0.0), axis=0, keepdims=True)
    n_valid = jnp.where(active, jnp.clip(cnt_j - (jr - first_j) * MOE_BLOCK, 0.0, MOE_BLOCK), 0.0).astype(jnp.int32)
    zero = jnp.zeros_like(bidx)
    tab_ref[...] = jnp.concatenate([e_lo, e_hi, bidx, total.astype(jnp.int32), n_valid, zero, zero, zero], axis=0)


def _tables(counts, nbp):
    return pl.pallas_call(
        functools.partial(_tables_kernel, nbp=nbp),
        out_shape=[jax.ShapeDtypeStruct((8, nbp), jnp.int32),
                   jax.ShapeDtypeStruct(COUNT_SHAPE, F32)],
        name="moe_tables",
    )(counts)


def _dest_kernel(mrow_ref, start_ref, dest_ref, *, tiles):
    start = jnp.concatenate([start_ref[...]] * (TILE // META_W), axis=1)
    cls_ids = lax.broadcasted_iota(jnp.int32, (N_CLASS_IDS, TILE), 0)
    for t in range(tiles):
        cls = mrow_ref[t, 2:3, :].astype(jnp.int32)
        first = jnp.sum(jnp.where(cls_ids == cls, start, 0.0), axis=0, keepdims=True)
        dest_ref[t] = (first + mrow_ref[t, 3:4, :]).astype(jnp.int32)


def _dest(mrow, start):
    n_tiles = mrow.shape[0]
    tiles = math.gcd(n_tiles, DEST_TILES)
    return pl.pallas_call(
        functools.partial(_dest_kernel, tiles=tiles),
        grid=(n_tiles // tiles,),
        in_specs=[pl.BlockSpec((tiles, 8, TILE), lambda i: (i, 0, 0)),
                  pl.BlockSpec(COUNT_SHAPE, lambda i: (0, 0))],
        out_specs=pl.BlockSpec((tiles, 1, TILE), lambda i: (i, 0, 0)),
        out_shape=jax.ShapeDtypeStruct((n_tiles, 1, TILE), jnp.int32),
        compiler_params=pltpu.CompilerParams(dimension_semantics=("arbitrary",)),
        name="moe_dest",
    )(mrow, start)


def _sc_mesh():
    return plsc.VectorSubcoreMesh(core_axis_name="core", subcore_axis_name="subcore")


def _col_chunks(width):
    chunks, c = [], 0
    while c < width:
        w = min(SC_COLS, width - c)
        assert c % w == 0
        chunks.append((c // w, w))
        c += w
    return chunks


_SC_PARAMS = dict(core_axis_name=("core", "subcore"), dimension_semantics=(pltpu.PARALLEL,))


def _sc_scatter_rows(idx, src, n_out):
    n, d = src.shape

    @pl.kernel(out_type=jax.ShapeDtypeStruct((n_out, d), src.dtype), mesh=_sc_mesh(), scratch_types=[],
               compiler_params=pltpu.CompilerParams(use_tc_tiling_on_sc=True))
    def scatter(src_hbm, idx_hbm, out_hbm):
        for cb, cw in _col_chunks(d):
            def body(rows_vmem, idx_vmem, cb=cb, cw=cw):
                pltpu.sync_copy(rows_vmem, out_hbm.at[:, pl.ds(cb * cw, cw)].at[idx_vmem.at[0]])

            pltpu.emit_pipeline(
                body, grid=(n // SC_WINDOW,),
                in_specs=[pl.BlockSpec((SC_WINDOW, cw), lambda i, cb=cb: (i, cb)),
                          pl.BlockSpec((1, SC_WINDOW), lambda i: (0, i))],
                out_specs=[], **_SC_PARAMS)(src_hbm, idx_hbm)

    return scatter(src, idx)


def _sc_gather_rows(idx, srcs):
    n, cw = idx.shape[1], srcs[0].shape[1]
    assert cw <= SC_COLS

    @pl.kernel(out_type=jax.ShapeDtypeStruct((n, cw * len(srcs)), srcs[0].dtype), mesh=_sc_mesh(), scratch_types=[],
               compiler_params=pltpu.CompilerParams(use_tc_tiling_on_sc=True))
    def gather(*refs):
        src_hbms, idx_hbm, out_hbm = refs[:len(srcs)], refs[len(srcs)], refs[len(srcs) + 1]
        for c, src_hbm in enumerate(src_hbms):
            def body(idx_vmem, rows_vmem, src_hbm=src_hbm):
                pltpu.sync_copy(src_hbm.at[idx_vmem.at[0]], rows_vmem)

            pltpu.emit_pipeline(
                body, grid=(n // SC_WINDOW,),
                in_specs=[pl.BlockSpec((1, SC_WINDOW), lambda i: (0, i))],
                out_specs=[pl.BlockSpec((SC_WINDOW, cw), lambda i, c=c: (i, c))],
                **_SC_PARAMS)(idx_hbm, out_hbm)

    return gather(*srcs, idx)


def _expert_kernel(elo_ref, ehi_ref, bidx_ref, nvalid_ref, tot_ref, *refs):
    del elo_ref, ehi_ref, bidx_ref
    x_refs, meta_ref = refs[:DMA_SPLIT], refs[DMA_SPLIT]
    w1lo_ref, w3lo_ref, w2lo_ref, w1hi_ref, w3hi_ref, w2hi_ref, g_ref, b_ref = refs[DMA_SPLIT + 1:DMA_SPLIT + 9]
    o_refs, pre_ref = refs[DMA_SPLIT + 9:2 * DMA_SPLIT + 9], refs[2 * DMA_SPLIT + 9]
    cw = D_MODEL // DMA_SPLIT
    j = pl.program_id(0)
    total = tot_ref[0]

    @pl.when(j == 0)
    def _():
        pre_ref[...] = jnp.zeros_like(pre_ref)

    def write_out(rows):
        for c, o_ref in enumerate(o_refs):
            o_ref[...] = rows[:, c * cw:(c + 1) * cw]

    def finish_previous():
        write_out(_layer_norm(pre_ref[...], g_ref[...], b_ref[...]))

    def hidden(half_a, b):
        return ((half_a * (jnp.tanh(half_a) + 1.0)) * b).astype(BF16)

    @pl.when(j < total)
    def _():
        n_valid = nvalid_ref[j]
        x = jnp.concatenate([r[...] for r in x_refs], axis=1)
        x = jnp.where(lax.broadcasted_iota(jnp.int32, (MOE_BLOCK, D_MODEL), 0) < n_valid, x, 0.0)
        xh = x.astype(BF16)
        half_a_lo = _dot(xh, w1lo_ref[...])
        b_lo = _dot(xh, w3lo_ref[...])
        finish_previous()
        half_a_hi = _dot(xh, w1hi_ref[...])
        b_hi = _dot(xh, w3hi_ref[...])
        y_lo = _dot(hidden(half_a_lo, b_lo), w2lo_ref[...])
        y_hi = _dot(hidden(half_a_hi, b_hi), w2hi_ref[...])
        live = lax.broadcasted_iota(jnp.int32, (MOE_BLOCK, 1), 0) < n_valid
        w_lo = jnp.where(live, meta_ref[:, 0:1], 0.0)
        w_hi = jnp.where(live, meta_ref[:, 1:2], 0.0)
        pre_ref[...] = DN_ALPHA * x + (w_lo * y_lo + w_hi * y_hi)

    @pl.when(j == total)
    def _():
        finish_previous()

    @pl.when(j > total)
    def _():
        write_out(jnp.zeros((MOE_BLOCK, D_MODEL), F32))


def _experts(tab, xb, lw):
    n_blocks = xb.shape[0] // MOE_BLOCK
    cw = D_MODEL // DMA_SPLIT
    up = lambda sel: pl.BlockSpec((None, D_MODEL, D_EXPERT), lambda j, lo, hi, bi, nv, tot: ((lo, hi)[sel][j], 0, 0))
    down = lambda sel: pl.BlockSpec((None, D_EXPERT, D_MODEL), lambda j, lo, hi, bi, nv, tot: ((lo, hi)[sel][j], 0, 0))
    vec = pl.BlockSpec((1, D_MODEL), lambda j, lo, hi, bi, nv, tot: (0, 0))
    grid_spec = pltpu.PrefetchScalarGridSpec(
        num_scalar_prefetch=5,
        grid=(n_blocks + 1,),
        in_specs=[pl.BlockSpec((MOE_BLOCK, cw), lambda j, lo, hi, bi, nv, tot, c=c: (bi[j], c)) for c in range(DMA_SPLIT)]
        + [pl.BlockSpec((MOE_BLOCK, META_W), lambda j, lo, hi, bi, nv, tot: (bi[j], D_MODEL // META_W)),
           up(0), up(0), down(0), up(1), up(1), down(1), vec, vec],
        out_specs=[pl.BlockSpec((MOE_BLOCK, cw), lambda j, lo, hi, bi, nv, tot: (jnp.maximum(j - 1, 0), 0))] * DMA_SPLIT,
        scratch_shapes=[pltpu.VMEM((MOE_BLOCK, D_MODEL), F32)],
    )
    return pl.pallas_call(
        _expert_kernel,
        grid_spec=grid_spec,
        out_shape=[jax.ShapeDtypeStruct((n_blocks * MOE_BLOCK, cw), F32)] * DMA_SPLIT,
        compiler_params=pltpu.CompilerParams(
            dimension_semantics=("arbitrary",), vmem_limit_bytes=VMEM_LIMIT),
        name="moe_experts",
    )(tab[0], tab[1], tab[2], tab[4], tab[3, 0:1], *([xb] * (DMA_SPLIT + 1)),
      lw["w1"], lw["w3"], lw["w2"], lw["w1"], lw["w3"], lw["w2"], lw["ln2_g"], lw["ln2_b"])


def _layer_weights(p, l):
    row = lambda a: a[l].reshape(1, -1).astype(F32)
    w_r = jnp.zeros((META_W, D_MODEL), F32)
    w_r = w_r.at[0:N_EXPERT_GROUPS].set(p["w_rg"][l].T).at[8:8 + N_EXPERTS].set(p["w_re"][l].T)
    b_r = jnp.zeros((META_W, 1), F32)
    b_r = b_r.at[0:N_EXPERT_GROUPS, 0].set(p["b_rg"][l]).at[N_EXPERT_GROUPS:8, 0].set(NEG_BIG)
    b_r = b_r.at[8:8 + N_EXPERTS, 0].set(p["b_re"][l])
    t = np.arange(TILE)
    gate_half = jnp.asarray(np.where(np.arange(p["w_in"].shape[-1]) >= OFF_G, 0.5, 1.0), F32)
    return {
        "w_in": (p["w_in"][l] * gate_half).astype(BF16),
        "b_in": row(p["b_in"]) * gate_half,
        "w_b": p["w_in"][l][:, OFF_B:OFF_B + BRANCH_W].astype(BF16),
        "b_b": p["b_in"][l][OFF_B:OFF_B + BRANCH_W].reshape(1, -1),
        "pool_w": jax.scipy.linalg.block_diag(*[p["pool_w"][l][g] for g in range(N_GROUPS)]).astype(BF16),
        "pool_scale": row(p["pool_scale"]),
        "conv_w": p["conv_w"][l],
        "conv_b": row(p["conv_b"]),
        "sgu_ln_g": row(p["sgu_ln_g"]),
        "sgu_ln_b": row(p["sgu_ln_b"]),
        "sgu_w": p["sgu_w"][l].astype(BF16),
        "sgu_b": jnp.repeat(p["sgu_b"][l].T, GROUP_W, axis=1),
        "branch_proj": (0.5 * p["branch_proj"][l]).astype(BF16),
        "w_out": p["w_out"][l].astype(BF16),
        "ln1_g": row(p["ln1_g"]),
        "ln1_b": row(p["ln1_b"]),
        "w_r": w_r,
        "b_r": b_r,
        "tri": jnp.asarray((t[:, None] < t[None, :]).astype(np.float32)).astype(BF16),
        "w1": (0.5 * p["w1"][l]).astype(BF16),
        "w3": p["w3"][l].astype(BF16),
        "w2": p["w2"][l].astype(BF16),
        "ln2_g": row(p["ln2_g"]),
        "ln2_b": row(p["ln2_b"]),
    }


def _encoder_layer(xs, seqs, lw):
    zero_counts = jnp.zeros(COUNT_SHAPE, F32)
    routed = []
    for x, seq in zip(xs, seqs):
        n_blocks = x.shape[0] // MOE_BLOCK + N_REACHABLE_CLASSES
        nbp = -(-(n_blocks + 1) // META_W) * META_W
        y_b = _fourier_mixer(x, lw["w_b"], lw["b_b"], seq)
        x1e, mrow, counts = _mixer(x, y_b, zero_counts, lw, seq)
        tab, start = _tables(counts, nbp)
        dest = _dest(mrow, start).reshape(1, -1)
        routed.append((tab, dest, _sc_scatter_rows(dest, x1e, n_blocks * MOE_BLOCK)))
    return [_sc_gather_rows(dest, _experts(tab, xb, lw)) for tab, dest, xb in routed]


def kernel(x_prompt, x_sample, w_in, b_in, pool_w, pool_scale, conv_w, conv_b, sgu_ln_g, sgu_ln_b,
           sgu_w, sgu_b, branch_proj, w_out, ln1_g, ln1_b, w_rg, b_rg, w_re, b_re, w1, w3, w2,
           ln2_g, ln2_b):
    p = dict(w_in=w_in, b_in=b_in, pool_w=pool_w, pool_scale=pool_scale, conv_w=conv_w, conv_b=conv_b,
             sgu_ln_g=sgu_ln_g, sgu_ln_b=sgu_ln_b, sgu_w=sgu_w, sgu_b=sgu_b, branch_proj=branch_proj,
             w_out=w_out, ln1_g=ln1_g, ln1_b=ln1_b, w_rg=w_rg, b_rg=b_rg, w_re=w_re, b_re=b_re,
             w1=w1, w3=w3, w2=w2, ln2_g=ln2_g, ln2_b=ln2_b)
    shapes = (x_prompt.shape, x_sample.shape)
    seqs = [s[1] for s in shapes]
    xs = [x_prompt.reshape(-1, D_MODEL), x_sample.reshape(-1, D_MODEL)]
    for l in range(w_in.shape[0]):
        xs = _encoder_layer(xs, seqs, _layer_weights(p, l))
    return tuple(x.reshape(s) for x, s in zip(xs, shapes))
```

```python
import functools
import math

import numpy as np
import jax
import jax.numpy as jnp
from jax import lax
from jax.experimental import pallas as pl
from jax.experimental.pallas import tpu as pltpu
from jax.experimental.pallas import tpu_sc as plsc

F32 = jnp.float32
BF16 = jnp.bfloat16

D_MODEL = 1024
DEPTH = 2
BRANCH_W = 256
N_GROUPS = 4
GROUP_W = 64
POOL_WINDOWS = (2, 4, 8, 16)
CHUNK = 128
OFF_A = 0
OFF_B = 256
OFF_CH = 512
OFF_DU = 1280
OFF_G = 1792
N_EXPERT_GROUPS = 4
EXPERTS_PER_GROUP = 8
N_EXPERTS = 32
D_EXPERT = 512
MOE_BLOCK = 256
DN_ALPHA = (2 * DEPTH) ** 0.25
LN_EPS = 1e-5

N_CLASS_IDS = 256
N_REACHABLE_CLASSES = N_EXPERT_GROUPS * (EXPERTS_PER_GROUP * (EXPERTS_PER_GROUP - 1) // 2)
META_W = 128
ROW_W = D_MODEL + META_W
COUNT_SHAPE = (N_CLASS_IDS, META_W)
HALO = 16
TILE = 512
DEST_TILES = 8
DMA_SPLIT = 4
SC_WINDOW = 128
SC_COLS = 256
NEG_BIG = -1e30
VMEM_LIMIT = 56 * 1024 * 1024


def _dot(a, b):
    return jnp.dot(a, b, preferred_element_type=F32)


def _layer_norm(x, g, b):
    mu = jnp.mean(x, axis=-1, keepdims=True)
    xc = x - mu
    var = jnp.mean(xc * xc, axis=-1, keepdims=True)
    return xc * lax.rsqrt(var + LN_EPS) * g + b


def _gelu_tanh(x):
    return 0.5 * x * (1.0 + jnp.tanh(math.sqrt(2.0 / math.pi) * (x + 0.044715 * (x * x * x))))


def _fft_factors(seq):
    n1 = 1 << (int(math.log2(seq)) // 2)
    return n1, seq // n1


@functools.lru_cache(maxsize=None)
def _fft_tables(seq):
    n1, n2 = _fft_factors(seq)
    c = np.arange(GROUP_W)
    ang = 2.0 * np.pi * ((c[:, None] * c[None, :]) % GROUP_W) / GROUP_W
    eye = np.eye(N_GROUPS)
    cs = np.concatenate([np.kron(eye, np.cos(ang)), np.kron(eye, np.sin(ang))], axis=1)
    k1 = np.arange(n1)
    t1 = np.arange(n1)
    t2 = np.arange(n2)
    t = t1[None, None, :] * n2 + t2[:, None, None]
    ang1 = 2.0 * np.pi * ((k1[None, :, None] * t) % seq) / seq
    gc, gs = np.cos(ang1), np.sin(ang1)
    lt = np.concatenate([np.concatenate([gc, -gs], axis=2), np.concatenate([-gs, -gc], axis=2)], axis=1)
    k2 = np.arange(n2)
    ang2 = 2.0 * np.pi * ((k2[:, None] * t2[None, :]) % n2) / n2
    scale = 1.0 / math.sqrt(seq * GROUP_W)
    to16 = lambda a: jnp.asarray(a, dtype=F32).astype(BF16)
    return to16(cs), to16(lt), to16(np.cos(ang2) * scale), to16(np.sin(ang2) * scale)


def _fft_stage1_kernel(*refs, n1, tb):
    x_refs, (wb_ref, bb_ref, cs_ref, lt_ref, o_ref, u_ref) = refs[:DMA_SPLIT], refs[DMA_SPLIT:]
    x = jnp.concatenate([r[...].reshape(n1 * tb, D_MODEL // DMA_SPLIT).astype(BF16) for r in x_refs], axis=1)
    zb = _dot(x, wb_ref[...]) + bb_ref[...]
    u = _dot(zb.astype(BF16), cs_ref[...])
    for c in range(4):
        u_ref[c] = u[:, c * 128:(c + 1) * 128]
    for j in range(tb):
        q = [u_ref[c, pl.ds(j, n1, stride=tb), :] for c in range(4)]
        stacked = jnp.concatenate([jnp.concatenate(q[0:2], axis=1),
                                   jnp.concatenate(q[2:4], axis=1)], axis=0).astype(BF16)
        b = _dot(lt_ref[j], stacked)
        o_ref[:, j * 512:j * 512 + BRANCH_W] = b[:n1].astype(BF16)
        o_ref[:, j * 512 + BRANCH_W:(j + 1) * 512] = b[n1:].astype(BF16)


def _fft_stage2_kernel(b_ref, c2_ref, s2_ref, o_ref, *, kb):
    for k in range(kb):
        slab = b_ref[k]
        y = _dot(c2_ref[...], slab[:, :BRANCH_W]) + _dot(s2_ref[...], slab[:, BRANCH_W:])
        o_ref[:, k * BRANCH_W:(k + 1) * BRANCH_W] = y.astype(BF16)


def _fourier_mixer(x, w_b, b_b, seq):
    n_tok = x.shape[0]
    bsz = n_tok // seq
    n1, n2 = _fft_factors(seq)
    tb = max(8, min(16, (4 << 20) // (n1 * D_MODEL * 4)))
    kb = min(n1, (2 << 20) // (n2 * 512 * 2))
    cs, lt, c2, s2 = _fft_tables(seq)
    x4 = x.reshape(bsz, n1, n2, D_MODEL)
    xw = D_MODEL // DMA_SPLIT
    bt = pl.pallas_call(
        functools.partial(_fft_stage1_kernel, n1=n1, tb=tb),
        grid=(bsz, n2 // tb),
        in_specs=[pl.BlockSpec((None, n1, tb, xw), lambda b, j, c=c: (b, 0, j, c)) for c in range(DMA_SPLIT)] + [
            pl.BlockSpec((D_MODEL, BRANCH_W), lambda b, j: (0, 0)),
            pl.BlockSpec((1, BRANCH_W), lambda b, j: (0, 0)),
            pl.BlockSpec((BRANCH_W, 2 * BRANCH_W), lambda b, j: (0, 0)),
            pl.BlockSpec((tb, 2 * n1, 2 * n1), lambda b, j: (j, 0, 0)),
        ],
        out_specs=pl.BlockSpec((None, n1, tb * 512), lambda b, j: (b, 0, j)),
        out_shape=jax.ShapeDtypeStruct((bsz, n1, n2 * 512), BF16),
        scratch_shapes=[pltpu.VMEM((4, n1 * tb, 128), F32)],
        compiler_params=pltpu.CompilerParams(
            dimension_semantics=("arbitrary", "arbitrary"), vmem_limit_bytes=VMEM_LIMIT),
        name="fft_stage1",
    )(*([x4] * DMA_SPLIT), w_b, b_b, cs, lt)
    bt4 = bt.reshape(bsz, n1, n2, 512)
    y = pl.pallas_call(
        functools.partial(_fft_stage2_kernel, kb=kb),
        grid=(bsz, n1 // kb),
        in_specs=[
            pl.BlockSpec((None, kb, n2, 512), lambda b, i: (b, i, 0, 0)),
            pl.BlockSpec((n2, n2), lambda b, i: (0, 0)),
            pl.BlockSpec((n2, n2), lambda b, i: (0, 0)),
        ],
        out_specs=pl.BlockSpec((None, n2, kb * BRANCH_W), lambda b, i: (b, 0, i)),
        out_shape=jax.ShapeDtypeStruct((bsz, n2, n1 * BRANCH_W), BF16),
        compiler_params=pltpu.CompilerParams(
            dimension_semantics=("arbitrary", "arbitrary"), vmem_limit_bytes=VMEM_LIMIT),
        name="fft_stage2",
    )(bt4, c2, s2)
    return y.reshape(n_tok, BRANCH_W)


def _mixer_kernel(x_ref, xp_ref, xn_ref, yb_ref, cin_ref,
                  win_ref, bin_ref, poolw_ref, pools_ref, convw_ref, convb_ref,
                  lng_ref, lnb_ref, sguw_ref, sgub_ref, bp_ref, wout_ref,
                  ln1g_ref, ln1b_ref, wr_ref, br_ref, tri_ref,
                  x1e_ref, mrow_ref, cout_ref, base_ref, pre_ref, *, seq, n_tiles):
    i = pl.program_id(0)
    tile = TILE
    ext = tile + 2 * HALO
    p0 = (jnp.minimum(i, n_tiles - 1) % (seq // tile)) * tile

    @pl.when(i == 0)
    def _():
        base_ref[...] = cin_ref[...]
        pre_ref[...] = jnp.zeros_like(pre_ref)

    live = i > 0

    x = x_ref[...]
    xc = x.astype(BF16)
    xe = jnp.concatenate([xp_ref[...].astype(BF16), xc, xn_ref[...].astype(BF16)], axis=0)
    pos = p0 - HALO + lax.broadcasted_iota(jnp.int32, (ext, BRANCH_W), 0)
    valid = (pos >= 0) & (pos < seq)
    lane = lax.broadcasted_iota(jnp.int32, (ext, BRANCH_W), 1)
    grp = lane // GROUP_W

    def half_gate_logits(k):
        lo = OFF_G + k * D_MODEL
        return _dot(xc, win_ref[:, lo:lo + D_MODEL]) + bin_ref[:, lo:lo + D_MODEL]

    def gated(half_gz, y_k, k):
        return (jnp.tanh(half_gz) + 1.0) * _dot(y_k.astype(BF16), bp_ref[k])

    za = jnp.where(valid, _dot(xe, win_ref[:, OFF_A:OFF_A + BRANCH_W]) + bin_ref[:, OFF_A:OFF_A + BRANCH_W], 0.0)
    zc = _dot(xe, win_ref[:, OFF_CH:OFF_DU]) + bin_ref[:, OFF_CH:OFF_DU]
    zd = _dot(xc, win_ref[:, OFF_DU:OFF_G]) + bin_ref[:, OFF_DU:OFF_G]
    hg0 = half_gate_logits(0)

    xh, xl = _tail_norm(pre_ref[...], ln1g_ref, ln1b_ref, x1e_ref)

    s2 = za + pltpu.roll(za, 1, 0)
    s4 = s2 + pltpu.roll(s2, 2, 0)
    s8 = s4 + pltpu.roll(s4, 4, 0)
    s16 = s8 + pltpu.roll(s8, 8, 0)
    c4 = pltpu.roll(s4, ext - 1, 0)
    c8 = pltpu.roll(s8, ext - 3, 0)
    c16 = pltpu.roll(s16, ext - 7, 0)
    wsum = jnp.where(grp == 0, s2, jnp.where(grp == 1, c4, jnp.where(grp == 2, c8, c16)))
    half = jnp.where(grp == 0, 1, jnp.where(grp == 1, 2, jnp.where(grp == 2, 4, 8)))
    cnt = jnp.minimum(pos + half, seq) - jnp.maximum(pos - half, 0)
    pooled = (wsum / jnp.maximum(cnt, 1).astype(F32) - za)[HALO:HALO + tile]
    y_a = _dot(pooled.astype(BF16), poolw_ref[...]) * pools_ref[...]
    hg1 = half_gate_logits(1)

    h = zc[:, 0:BRANCH_W]
    gate_b = zc[:, BRANCH_W:2 * BRANCH_W]
    gate_c = zc[:, 2 * BRANCH_W:3 * BRANCH_W]
    q = jnp.where(valid, gate_c * h, 0.0)
    q_prev = pltpu.roll(q, 1, 0)[HALO:HALO + tile]
    q_next = pltpu.roll(q, ext - 1, 0)[HALO:HALO + tile]
    conv = (q_prev * convw_ref[0:1, :] + q[HALO:HALO + tile] * convw_ref[1:2, :]
            + q_next * convw_ref[2:3, :] + convb_ref[...])
    y_c = gate_b[HALO:HALO + tile] * conv
    merged = gated(hg0, y_a, 0)
    hg2 = half_gate_logits(2)

    w_lo, w_hi, cls = _tail_route(xh, xl, wr_ref, br_ref)

    u = _gelu_tanh(zd[:, 0:BRANCH_W])
    v = _layer_norm(_gelu_tanh(zd[:, BRANCH_W:]), lng_ref[...], lnb_ref[...]).astype(BF16)
    grp_c = lax.broadcasted_iota(jnp.int32, (CHUNK, BRANCH_W), 1) // GROUP_W
    sps = []
    for c in range(tile // CHUNK):
        vch = v[c * CHUNK:(c + 1) * CHUNK]
        sp = None
        for g in range(N_GROUPS):
            r = _dot(sguw_ref[g], vch)
            sp = r if sp is None else jnp.where(grp_c == g, r, sp)
        sps.append(sp + sgub_ref[...])
    y_d = u * jnp.concatenate(sps, axis=0)
    merged = merged + gated(hg1, yb_ref[...], 1)
    hg3 = half_gate_logits(3)

    _tail_rank(w_lo, w_hi, cls, live, tri_ref, x1e_ref, mrow_ref, cout_ref, base_ref)

    merged = merged + gated(hg2, y_c, 2)
    merged = merged + gated(hg3, y_d, 3)
    pre_ref[...] = DN_ALPHA * x + _dot(merged.astype(BF16), wout_ref[...])


def _tail_norm(pre, ln1g_ref, ln1b_ref, x1e_ref):
    x1 = _layer_norm(pre, ln1g_ref[...], ln1b_ref[...])
    x1e_ref[:, 0:D_MODEL] = x1
    xh = x1.astype(BF16)
    return xh, (x1 - xh.astype(F32)).astype(BF16)


def _tail_route(xh, xl, wr_ref, br_ref):
    tile = TILE
    wr = wr_ref[...]
    wh = wr.astype(BF16)
    wl = (wr - wh.astype(F32)).astype(BF16)
    nt = (((1,), (1,)), ((), ()))
    dot_nt = lambda a, b: lax.dot_general(a, b, nt, preferred_element_type=F32)
    logits = dot_nt(wh, xh) + dot_nt(wh, xl) + dot_nt(wl, xh) + br_ref[...]
    row8 = lax.broadcasted_iota(jnp.int32, (EXPERTS_PER_GROUP, tile), 0)
    lg = logits[0:8]
    m = jnp.max(lg, axis=0, keepdims=True)
    g_idx = jnp.min(jnp.where(lg == m, row8, 8), axis=0, keepdims=True)
    p_group = 1.0 / jnp.sum(jnp.exp(lg - m), axis=0, keepdims=True)
    le = logits[8:16]
    for g in range(1, N_EXPERT_GROUPS):
        le = jnp.where(g_idx == g, logits[8 + 8 * g:16 + 8 * g], le)
    ex = jnp.exp(le - jnp.max(le, axis=0, keepdims=True))
    pe = ex / jnp.sum(ex, axis=0, keepdims=True)
    p1 = jnp.max(pe, axis=0, keepdims=True)
    i1 = jnp.min(jnp.where(pe == p1, row8, 8), axis=0, keepdims=True)
    pe2 = jnp.where(row8 == i1, -1.0, pe)
    p2 = jnp.max(pe2, axis=0, keepdims=True)
    i2 = jnp.min(jnp.where(pe2 == p2, row8, 8), axis=0, keepdims=True)
    first_lo = i1 < i2
    w_lo = p_group * jnp.where(first_lo, p1, p2)
    w_hi = p_group * jnp.where(first_lo, p2, p1)
    cls = g_idx * 64 + jnp.minimum(i1, i2) * EXPERTS_PER_GROUP + jnp.maximum(i1, i2)
    return w_lo, w_hi, cls


def _tail_rank(w_lo, w_hi, cls, live, tri_ref, x1e_ref, mrow_ref, cout_ref, base_ref):
    tile = TILE
    onehot = jnp.logical_and(lax.broadcasted_iota(jnp.int32, (N_CLASS_IDS, tile), 0) == cls, live)
    ohb = onehot.astype(BF16)
    before = _dot(ohb, tri_ref[...])
    base = base_ref[...]
    base_t = jnp.concatenate([base] * (tile // META_W), axis=1)
    rank = jnp.sum(jnp.where(onehot, before + base_t, 0.0), axis=0, keepdims=True)
    new_base = base + _dot(ohb, jnp.ones((tile, META_W), BF16))
    base_ref[...] = new_base
    cout_ref[...] = new_base

    meta_t = jnp.concatenate([w_lo, w_hi, cls.astype(F32), rank, jnp.zeros((META_W - 4, tile), F32)], axis=0)
    x1e_ref[:, D_MODEL:ROW_W] = meta_t.T
    mrow_ref[...] = meta_t[0:8]


def _const_spec(shape):
    nd = len(shape)
    return pl.BlockSpec(shape, lambda i, _nd=nd: (0,) * _nd, pipeline_mode=pl.Buffered(1))


def _mixer(x, y_b, counts_in, lw, seq):
    n_tok = x.shape[0]
    n_tiles = n_tok // TILE
    hb = TILE // HALO
    n_hblk = n_tok // HALO
    consts = [lw["w_in"], lw["b_in"], lw["pool_w"], lw["pool_scale"], lw["conv_w"], lw["conv_b"],
              lw["sgu_ln_g"], lw["sgu_ln_b"], lw["sgu_w"], lw["sgu_b"], lw["branch_proj"], lw["w_out"],
              lw["ln1_g"], lw["ln1_b"], lw["w_r"], lw["b_r"], lw["tri"]]
    cur = lambda i: jnp.minimum(i, n_tiles - 1)
    return pl.pallas_call(
        functools.partial(_mixer_kernel, seq=seq, n_tiles=n_tiles),
        grid=(n_tiles + 1,),
        in_specs=[
            pl.BlockSpec((TILE, D_MODEL), lambda i: (cur(i), 0)),
            pl.BlockSpec((HALO, D_MODEL), lambda i: (jnp.maximum(cur(i) * hb - 1, 0), 0)),
            pl.BlockSpec((HALO, D_MODEL), lambda i: (jnp.minimum((cur(i) + 1) * hb, n_hblk - 1), 0)),
            pl.BlockSpec((TILE, BRANCH_W), lambda i: (cur(i), 0)),
            pl.BlockSpec(COUNT_SHAPE, lambda i: (0, 0)),
        ] + [_const_spec(c.shape) for c in consts],
        out_specs=[
            pl.BlockSpec((TILE, ROW_W), lambda i: (jnp.maximum(i - 1, 0), 0)),
            pl.BlockSpec((None, 8, TILE), lambda i: (jnp.maximum(i - 1, 0), 0, 0)),
            pl.BlockSpec(COUNT_SHAPE, lambda i: (0, 0)),
        ],
        out_shape=[
            jax.ShapeDtypeStruct((n_tok, ROW_W), F32),
            jax.ShapeDtypeStruct((n_tiles, 8, TILE), F32),
            jax.ShapeDtypeStruct(COUNT_SHAPE, F32),
        ],
        scratch_shapes=[pltpu.VMEM(COUNT_SHAPE, F32), pltpu.VMEM((TILE, D_MODEL), F32)],
        compiler_params=pltpu.CompilerParams(
            dimension_semantics=("arbitrary",), vmem_limit_bytes=VMEM_LIMIT),
        name="mixer",
    )(x, x, x, y_b, counts_in, *consts)


def _cumsum_sublanes(a):
    n = a.shape[0]
    row = lax.broadcasted_iota(jnp.int32, a.shape, 0)
    s = 1
    while s < n:
        a = a + jnp.where(row >= s, pltpu.roll(a, s, 0), 0.0)
        s *= 2
    return a


def _tables_kernel(cnt_ref, tab_ref, start_ref, *, nbp):
    cnt = cnt_ref[...]
    nb = jnp.floor((cnt + (MOE_BLOCK - 1)) * (1.0 / MOE_BLOCK))
    end = _cumsum_sublanes(nb)
    start_ref[...] = (end - nb) * MOE_BLOCK
    end_t = jnp.concatenate([end] * (nbp // META_W), axis=1)
    j = lax.broadcasted_iota(jnp.int32, (N_CLASS_IDS, nbp), 1).astype(F32)
    blk_cls = jnp.sum((end_t <= j).astype(F32), axis=0, keepdims=True)
    total = end_t[N_CLASS_IDS - 1:N_CLASS_IDS, :]
    jr = j[0:1, :]
    active = jr < total
    last_cls = jnp.max(jnp.where(active, blk_cls, 0.0), axis=-1, keepdims=True)
    cls_i = jnp.where(active, blk_cls, last_cls).astype(jnp.int32)
    grp = cls_i >> 6
    e_lo = (cls_i >> 3) & 7
    e_hi = cls_i & 7
    bidx = jnp.minimum(jr, total - 1.0).astype(jnp.int32)
    match = lax.broadcasted_iota(jnp.int32, (N_CLASS_IDS, nbp), 0) == cls_i
    reps = nbp // META_W
    cnt_j = jnp.sum(jnp.where(match, jnp.concatenate([cnt] * reps, axis=1), 0.0), axis=0, keepdims=True)
    first_j = jnp.sum(jnp.where(match, jnp.concatenate([end - nb] * reps, axis=1), 0.0), axis=0, keepdims=True)
    n_valid = jnp.where(active, jnp.clip(cnt_j - (jr - first_j) * MOE_BLOCK, 0.0, MOE_BLOCK), 0.0).astype(jnp.int32)
    zero = jnp.zeros_like(bidx)
    tab_ref[...] = jnp.concatenate([e_lo, e_hi, bidx, total.astype(jnp.int32), n_valid, grp, zero, zero], axis=0)


def _tables(counts, nbp):
    return pl.pallas_call(
        functools.partial(_tables_kernel, nbp=nbp),
        out_shape=[jax.ShapeDtypeStruct((8, nbp), jnp.int32),
                   jax.ShapeDtypeStruct(COUNT_SHAPE, F32)],
        name="moe_tables",
    )(counts)


def _dest_kernel(mrow_ref, start_ref, dest_ref, *, tiles):
    start = jnp.concatenate([start_ref[...]] * (TILE // META_W), axis=1)
    cls_ids = lax.broadcasted_iota(jnp.int32, (N_CLASS_IDS, TILE), 0)
    for t in range(tiles):
        cls = mrow_ref[t, 2:3, :].astype(jnp.int32)
        first = jnp.sum(jnp.where(cls_ids == cls, start, 0.0), axis=0, keepdims=True)
        dest_ref[t] = (first + mrow_ref[t, 3:4, :]).astype(jnp.int32)


def _dest(mrow, start):
    n_tiles = mrow.shape[0]
    tiles = math.gcd(n_tiles, DEST_TILES)
    return pl.pallas_call(
        functools.partial(_dest_kernel, tiles=tiles),
        grid=(n_tiles // tiles,),
        in_specs=[pl.BlockSpec((tiles, 8, TILE), lambda i: (i, 0, 0)),
                  pl.BlockSpec(COUNT_SHAPE, lambda i: (0, 0))],
        out_specs=pl.BlockSpec((tiles, 1, TILE), lambda i: (i, 0, 0)),
        out_shape=jax.ShapeDtypeStruct((n_tiles, 1, TILE), jnp.int32),
        compiler_params=pltpu.CompilerParams(dimension_semantics=("arbitrary",)),
        name="moe_dest",
    )(mrow, start)


def _sc_mesh():
    return plsc.VectorSubcoreMesh(core_axis_name="core", subcore_axis_name="subcore")


def _col_chunks(width):
    chunks, c = [], 0
    while c < width:
        w = min(SC_COLS, width - c)
        assert c % w == 0
        chunks.append((c // w, w))
        c += w
    return chunks


_SC_PARAMS = dict(core_axis_name=("core", "subcore"), dimension_semantics=(pltpu.PARALLEL,))


def _sc_scatter_rows(idx, src, n_out):
    n, d = src.shape

    @pl.kernel(out_type=jax.ShapeDtypeStruct((n_out, d), src.dtype), mesh=_sc_mesh(), scratch_types=[],
               compiler_params=pltpu.CompilerParams(use_tc_tiling_on_sc=True))
    def scatter(src_hbm, idx_hbm, out_hbm):
        for cb, cw in _col_chunks(d):
            def body(rows_vmem, idx_vmem, cb=cb, cw=cw):
                pltpu.sync_copy(rows_vmem, out_hbm.at[:, pl.ds(cb * cw, cw)].at[idx_vmem.at[0]])

            pltpu.emit_pipeline(
                body, grid=(n // SC_WINDOW,),
                in_specs=[pl.BlockSpec((SC_WINDOW, cw), lambda i, cb=cb: (i, cb)),
                          pl.BlockSpec((1, SC_WINDOW), lambda i: (0, i))],
                out_specs=[], **_SC_PARAMS)(src_hbm, idx_hbm)

    return scatter(src, idx)


def _sc_gather_rows(idx, src):
    n, d = idx.shape[1], src.shape[1]

    @pl.kernel(out_type=jax.ShapeDtypeStruct((n, d), src.dtype), mesh=_sc_mesh(), scratch_types=[],
               compiler_params=pltpu.CompilerParams(use_tc_tiling_on_sc=True))
    def gather(src_hbm, idx_hbm, out_hbm):
        for cb, cw in _col_chunks(d):
            def body(idx_vmem, rows_vmem, cb=cb, cw=cw):
                pltpu.sync_copy(src_hbm.at[:, pl.ds(cb * cw, cw)].at[idx_vmem.at[0]], rows_vmem)

            pltpu.emit_pipeline(
                body, grid=(n // SC_WINDOW,),
                in_specs=[pl.BlockSpec((1, SC_WINDOW), lambda i: (0, i))],
                out_specs=[pl.BlockSpec((SC_WINDOW, cw), lambda i, cb=cb: (i, cb))],
                **_SC_PARAMS)(idx_hbm, out_hbm)

    return gather(src, idx)


def _expert_kernel(grp_ref, elo_ref, ehi_ref, bidx_ref, nvalid_ref, tot_ref,
                   xb_ref, w1_ref, w3_ref, w2_ref, g_ref, b_ref, o_ref, pre_ref):
    del grp_ref, bidx_ref
    j = pl.program_id(0)
    total = tot_ref[0]

    @pl.when(j == 0)
    def _():
        pre_ref[...] = jnp.zeros_like(pre_ref)

    def finish_previous():
        o_ref[...] = _layer_norm(pre_ref[...], g_ref[...], b_ref[...])

    def hidden(half_a, b):
        return ((half_a * (jnp.tanh(half_a) + 1.0)) * b).astype(BF16)

    @pl.when(j < total)
    def _():
        n_valid = nvalid_ref[j]
        x = jnp.where(lax.broadcasted_iota(jnp.int32, (MOE_BLOCK, D_MODEL), 0) < n_valid, xb_ref[:, 0:D_MODEL], 0.0)
        xh = x.astype(BF16)
        e_lo = elo_ref[j]
        e_hi = ehi_ref[j]
        half_a_lo = _dot(xh, w1_ref[e_lo])
        b_lo = _dot(xh, w3_ref[e_lo])
        finish_previous()
        half_a_hi = _dot(xh, w1_ref[e_hi])
        b_hi = _dot(xh, w3_ref[e_hi])
        y_lo = _dot(hidden(half_a_lo, b_lo), w2_ref[e_lo])
        y_hi = _dot(hidden(half_a_hi, b_hi), w2_ref[e_hi])
        live = lax.broadcasted_iota(jnp.int32, (MOE_BLOCK, 1), 0) < n_valid
        w_lo = jnp.where(live, xb_ref[:, D_MODEL:D_MODEL + 1], 0.0)
        w_hi = jnp.where(live, xb_ref[:, D_MODEL + 1:D_MODEL + 2], 0.0)
        pre_ref[...] = DN_ALPHA * x + (w_lo * y_lo + w_hi * y_hi)

    @pl.when(j == total)
    def _():
        finish_previous()

    @pl.when(j > total)
    def _():
        o_ref[...] = jnp.zeros_like(o_ref)


def _experts(tab, xb, lw):
    n_blocks = xb.shape[0] // MOE_BLOCK
    grp_w = lambda shape: pl.BlockSpec((EXPERTS_PER_GROUP,) + shape, lambda j, g, lo, hi, bi, nv, tot: (g[j], 0, 0),
                                       pipeline_mode=pl.Buffered(1))
    vec = pl.BlockSpec((1, D_MODEL), lambda j, g, lo, hi, bi, nv, tot: (0, 0))
    grid_spec = pltpu.PrefetchScalarGridSpec(
        num_scalar_prefetch=6,
        grid=(n_blocks + 1,),
        in_specs=[pl.BlockSpec((MOE_BLOCK, ROW_W), lambda j, g, lo, hi, bi, nv, tot: (bi[j], 0)),
                  grp_w((D_MODEL, D_EXPERT)), grp_w((D_MODEL, D_EXPERT)), grp_w((D_EXPERT, D_MODEL)), vec, vec],
        out_specs=pl.BlockSpec((MOE_BLOCK, D_MODEL), lambda j, g, lo, hi, bi, nv, tot: (jnp.maximum(j - 1, 0), 0)),
        scratch_shapes=[pltpu.VMEM((MOE_BLOCK, D_MODEL), F32)],
    )
    return pl.pallas_call(
        _expert_kernel,
        grid_spec=grid_spec,
        out_shape=jax.ShapeDtypeStruct((n_blocks * MOE_BLOCK, D_MODEL), F32),
        compiler_params=pltpu.CompilerParams(
            dimension_semantics=("arbitrary",), vmem_limit_bytes=VMEM_LIMIT),
        name="moe_experts",
    )(tab[5], tab[0], tab[1], tab[2], tab[4], tab[3, 0:1], xb,
      lw["w1"], lw["w3"], lw["w2"], lw["ln2_g"], lw["ln2_b"])


def _layer_weights(p, l):
    row = lambda a: a[l].reshape(1, -1).astype(F32)
    w_r = jnp.zeros((META_W, D_MODEL), F32)
    w_r = w_r.at[0:N_EXPERT_GROUPS].set(p["w_rg"][l].T).at[8:8 + N_EXPERTS].set(p["w_re"][l].T)
    b_r = jnp.zeros((META_W, 1), F32)
    b_r = b_r.at[0:N_EXPERT_GROUPS, 0].set(p["b_rg"][l]).at[N_EXPERT_GROUPS:8, 0].set(NEG_BIG)
    b_r = b_r.at[8:8 + N_EXPERTS, 0].set(p["b_re"][l])
    t = np.arange(TILE)
    gate_half = jnp.asarray(np.where(np.arange(p["w_in"].shape[-1]) >= OFF_G, 0.5, 1.0), F32)
    return {
        "w_in": (p["w_in"][l] * gate_half).astype(BF16),
        "b_in": row(p["b_in"]) * gate_half,
        "w_b": p["w_in"][l][:, OFF_B:OFF_B + BRANCH_W].astype(BF16),
        "b_b": p["b_in"][l][OFF_B:OFF_B + BRANCH_W].reshape(1, -1),
        "pool_w": jax.scipy.linalg.block_diag(*[p["pool_w"][l][g] for g in range(N_GROUPS)]).astype(BF16),
        "pool_scale": row(p["pool_scale"]),
        "conv_w": p["conv_w"][l],
        "conv_b": row(p["conv_b"]),
        "sgu_ln_g": row(p["sgu_ln_g"]),
        "sgu_ln_b": row(p["sgu_ln_b"]),
        "sgu_w": p["sgu_w"][l].astype(BF16),
        "sgu_b": jnp.repeat(p["sgu_b"][l].T, GROUP_W, axis=1),
        "branch_proj": (0.5 * p["branch_proj"][l]).astype(BF16),
        "w_out": p["w_out"][l].astype(BF16),
        "ln1_g": row(p["ln1_g"]),
        "ln1_b": row(p["ln1_b"]),
        "w_r": w_r,
        "b_r": b_r,
        "tri": jnp.asarray((t[:, None] < t[None, :]).astype(np.float32)).astype(BF16),
        "w1": (0.5 * p["w1"][l]).astype(BF16),
        "w3": p["w3"][l].astype(BF16),
        "w2": p["w2"][l].astype(BF16),
        "ln2_g": row(p["ln2_g"]),
        "ln2_b": row(p["ln2_b"]),
    }


def _encoder_layer(xs, seqs, lw):
    zero_counts = jnp.zeros(COUNT_SHAPE, F32)
    routed = []
    for x, seq in zip(xs, seqs):
        n_blocks = x.shape[0] // MOE_BLOCK + N_REACHABLE_CLASSES
        nbp = -(-(n_blocks + 1) // META_W) * META_W
        y_b = _fourier_mixer(x, lw["w_b"], lw["b_b"], seq)
        x1e, mrow, counts = _mixer(x, y_b, zero_counts, lw, seq)
        tab, start = _tables(counts, nbp)
        dest = _dest(mrow, start).reshape(1, -1)
        routed.append((tab, dest, _sc_scatter_rows(dest, x1e, n_blocks * MOE_BLOCK)))
    return [_sc_gather_rows(dest, _experts(tab, xb, lw)) for tab, dest, xb in routed]


def kernel(x_prompt, x_sample, w_in, b_in, pool_w, pool_scale, conv_w, conv_b, sgu_ln_g, sgu_ln_b,
           sgu_w, sgu_b, branch_proj, w_out, ln1_g, ln1_b, w_rg, b_rg, w_re, b_re, w1, w3, w2,
           ln2_g, ln2_b):
    p = dict(w_in=w_in, b_in=b_in, pool_w=pool_w, pool_scale=pool_scale, conv_w=conv_w, conv_b=conv_b,
             sgu_ln_g=sgu_ln_g, sgu_ln_b=sgu_ln_b, sgu_w=sgu_w, sgu_b=sgu_b, branch_proj=branch_proj,
             w_out=w_out, ln1_g=ln1_g, ln1_b=ln1_b, w_rg=w_rg, b_rg=b_rg, w_re=w_re, b_re=b_re,
             w1=w1, w3=w3, w2=w2, ln2_g=ln2_g, ln2_b=ln2_b)
    shapes = (x_prompt.shape, x_sample.shape)
    seqs = [s[1] for s in shapes]
    xs = [x_prompt.reshape(-1, D_MODEL), x_sample.reshape(-1, D_MODEL)]
    for l in range(w_in.shape[0]):
        xs = _encoder_layer(xs, seqs, _layer_weights(p, l))
    return tuple(x.reshape(s) for x, s in zip(xs, shapes))
```

```python
import functools
import math

import numpy as np
import jax
import jax.numpy as jnp
from jax import lax
from jax.experimental import pallas as pl
from jax.experimental.pallas import tpu as pltpu
from jax.experimental.pallas import tpu_sc as plsc

F32 = jnp.float32
BF16 = jnp.bfloat16

D_MODEL = 1024
DEPTH = 2
BRANCH_W = 256
N_GROUPS = 4
GROUP_W = 64
POOL_WINDOWS = (2, 4, 8, 16)
CHUNK = 128
OFF_A = 0
OFF_B = 256
OFF_CH = 512
OFF_DU = 1280
OFF_G = 1792
N_EXPERT_GROUPS = 4
EXPERTS_PER_GROUP = 8
N_EXPERTS = 32
D_EXPERT = 512
MOE_BLOCK = 256
DN_ALPHA = (2 * DEPTH) ** 0.25
LN_EPS = 1e-5

N_CLASS_IDS = 256
N_REACHABLE_CLASSES = N_EXPERT_GROUPS * (EXPERTS_PER_GROUP * (EXPERTS_PER_GROUP - 1) // 2)
META_W = 128
ROW_W = D_MODEL + META_W
COUNT_SHAPE = (N_CLASS_IDS, META_W)
HALO = 16
TILE = 512
DEST_TILES = 8
DMA_SPLIT = 4
SC_WINDOW = 128
SC_COLS = 256
NEG_BIG = -1e30
VMEM_LIMIT = 56 * 1024 * 1024


def _dot(a, b):
    return jnp.dot(a, b, preferred_element_type=F32)


def _layer_norm(x, g, b):
    mu = jnp.mean(x, axis=-1, keepdims=True)
    xc = x - mu
    var = jnp.mean(xc * xc, axis=-1, keepdims=True)
    return xc * lax.rsqrt(var + LN_EPS) * g + b


def _gelu_tanh(x):
    return 0.5 * x * (1.0 + jnp.tanh(math.sqrt(2.0 / math.pi) * (x + 0.044715 * (x * x * x))))


def _fft_factors(seq):
    n1 = 1 << (int(math.log2(seq)) // 2)
    return n1, seq // n1


@functools.lru_cache(maxsize=None)
def _fft_tables(seq):
    n1, n2 = _fft_factors(seq)
    c = np.arange(GROUP_W)
    ang = 2.0 * np.pi * ((c[:, None] * c[None, :]) % GROUP_W) / GROUP_W
    eye = np.eye(N_GROUPS)
    cs = np.concatenate([np.kron(eye, np.cos(ang)), np.kron(eye, np.sin(ang))], axis=1)
    k1 = np.arange(n1)
    t1 = np.arange(n1)
    t2 = np.arange(n2)
    t = t1[None, None, :] * n2 + t2[:, None, None]
    ang1 = 2.0 * np.pi * ((k1[None, :, None] * t) % seq) / seq
    gc, gs = np.cos(ang1), np.sin(ang1)
    lt = np.concatenate([np.concatenate([gc, -gs], axis=2), np.concatenate([-gs, -gc], axis=2)], axis=1)
    k2 = np.arange(n2)
    ang2 = 2.0 * np.pi * ((k2[:, None] * t2[None, :]) % n2) / n2
    scale = 1.0 / math.sqrt(seq * GROUP_W)
    to16 = lambda a: jnp.asarray(a, dtype=F32).astype(BF16)
    return to16(cs), to16(lt), to16(np.cos(ang2) * scale), to16(np.sin(ang2) * scale)


def _fft_stage1_kernel(*refs, n1, tb):
    x_refs, (wb_ref, bb_ref, cs_ref, lt_ref, o_ref, u_ref) = refs[:DMA_SPLIT], refs[DMA_SPLIT:]
    x = jnp.concatenate([r[...].reshape(n1 * tb, D_MODEL // DMA_SPLIT).astype(BF16) for r in x_refs], axis=1)
    zb = _dot(x, wb_ref[...]) + bb_ref[...]
    u = _dot(zb.astype(BF16), cs_ref[...])
    for c in range(4):
        u_ref[c] = u[:, c * 128:(c + 1) * 128]
    for j in range(tb):
        q = [u_ref[c, pl.ds(j, n1, stride=tb), :] for c in range(4)]
        stacked = jnp.concatenate([jnp.concatenate(q[0:2], axis=1),
                                   jnp.concatenate(q[2:4], axis=1)], axis=0).astype(BF16)
        b = _dot(lt_ref[j], stacked)
        o_ref[:, j * 512:j * 512 + BRANCH_W] = b[:n1].astype(BF16)
        o_ref[:, j * 512 + BRANCH_W:(j + 1) * 512] = b[n1:].astype(BF16)


def _fft_stage2_kernel(b_ref, c2_ref, s2_ref, o_ref, *, kb):
    for k in range(kb):
        slab = b_ref[k]
        y = _dot(c2_ref[...], slab[:, :BRANCH_W]) + _dot(s2_ref[...], slab[:, BRANCH_W:])
        o_ref[:, k * BRANCH_W:(k + 1) * BRANCH_W] = y.astype(BF16)


def _fourier_mixer(x, w_b, b_b, seq):
    n_tok = x.shape[0]
    bsz = n_tok // seq
    n1, n2 = _fft_factors(seq)
    tb = max(8, min(16, (4 << 20) // (n1 * D_MODEL * 4)))
    kb = min(n1, (2 << 20) // (n2 * 512 * 2))
    cs, lt, c2, s2 = _fft_tables(seq)
    x4 = x.reshape(bsz, n1, n2, D_MODEL)
    xw = D_MODEL // DMA_SPLIT
    bt = pl.pallas_call(
        functools.partial(_fft_stage1_kernel, n1=n1, tb=tb),
        grid=(bsz, n2 // tb),
        in_specs=[pl.BlockSpec((None, n1, tb, xw), lambda b, j, c=c: (b, 0, j, c)) for c in range(DMA_SPLIT)] + [
            pl.BlockSpec((D_MODEL, BRANCH_W), lambda b, j: (0, 0)),
            pl.BlockSpec((1, BRANCH_W), lambda b, j: (0, 0)),
            pl.BlockSpec((BRANCH_W, 2 * BRANCH_W), lambda b, j: (0, 0)),
            pl.BlockSpec((tb, 2 * n1, 2 * n1), lambda b, j: (j, 0, 0)),
        ],
        out_specs=pl.BlockSpec((None, n1, tb * 512), lambda b, j: (b, 0, j)),
        out_shape=jax.ShapeDtypeStruct((bsz, n1, n2 * 512), BF16),
        scratch_shapes=[pltpu.VMEM((4, n1 * tb, 128), F32)],
        compiler_params=pltpu.CompilerParams(
            dimension_semantics=("arbitrary", "arbitrary"), vmem_limit_bytes=VMEM_LIMIT),
        name="fft_stage1",
    )(*([x4] * DMA_SPLIT), w_b, b_b, cs, lt)
    bt4 = bt.reshape(bsz, n1, n2, 512)
    y = pl.pallas_call(
        functools.partial(_fft_stage2_kernel, kb=kb),
        grid=(bsz, n1 // kb),
        in_specs=[
            pl.BlockSpec((None, kb, n2, 512), lambda b, i: (b, i, 0, 0)),
            pl.BlockSpec((n2, n2), lambda b, i: (0, 0)),
            pl.BlockSpec((n2, n2), lambda b, i: (0, 0)),
        ],
        out_specs=pl.BlockSpec((None, n2, kb * BRANCH_W), lambda b, i: (b, 0, i)),
        out_shape=jax.ShapeDtypeStruct((bsz, n2, n1 * BRANCH_W), BF16),
        compiler_params=pltpu.CompilerParams(
            dimension_semantics=("arbitrary", "arbitrary"), vmem_limit_bytes=VMEM_LIMIT),
        name="fft_stage2",
    )(bt4, c2, s2)
    return y.reshape(n_tok, BRANCH_W)


def _mixer_kernel(x_ref, xp_ref, xn_ref, yb_ref, cin_ref,
                  win_ref, bin_ref, poolw_ref, pools_ref, convw_ref, convb_ref,
                  lng_ref, lnb_ref, sguw_ref, sgub_ref, bp_ref, wout_ref,
                  ln1g_ref, ln1b_ref, wr_ref, br_ref, tri_ref,
                  x1e_ref, mrow_ref, cout_ref, base_ref, pre_ref, *, seq, n_tiles):
    i = pl.program_id(0)
    tile = TILE
    ext = tile + 2 * HALO
    p0 = (jnp.minimum(i, n_tiles - 1) % (seq // tile)) * tile

    @pl.when(i == 0)
    def _():
        base_ref[...] = cin_ref[...]
        pre_ref[...] = jnp.zeros_like(pre_ref)

    live = i > 0

    x = x_ref[...]
    xc = x.astype(BF16)
    xe = jnp.concatenate([xp_ref[...].astype(BF16), xc, xn_ref[...].astype(BF16)], axis=0)
    pos = p0 - HALO + lax.broadcasted_iota(jnp.int32, (ext, BRANCH_W), 0)
    valid = (pos >= 0) & (pos < seq)
    lane = lax.broadcasted_iota(jnp.int32, (ext, BRANCH_W), 1)
    grp = lane // GROUP_W

    def half_gate_logits(k):
        lo = OFF_G + k * D_MODEL
        return _dot(xc, win_ref[:, lo:lo + D_MODEL]) + bin_ref[:, lo:lo + D_MODEL]

    def gated(half_gz, y_k, k):
        return (jnp.tanh(half_gz) + 1.0) * _dot(y_k.astype(BF16), bp_ref[k])

    za = jnp.where(valid, _dot(xe, win_ref[:, OFF_A:OFF_A + BRANCH_W]) + bin_ref[:, OFF_A:OFF_A + BRANCH_W], 0.0)
    zc = _dot(xe, win_ref[:, OFF_CH:OFF_DU]) + bin_ref[:, OFF_CH:OFF_DU]
    zd = _dot(xc, win_ref[:, OFF_DU:OFF_G]) + bin_ref[:, OFF_DU:OFF_G]
    hg0 = half_gate_logits(0)

    xh, xl = _tail_norm(pre_ref[...], ln1g_ref, ln1b_ref, x1e_ref)

    s2 = za + pltpu.roll(za, 1, 0)
    s4 = s2 + pltpu.roll(s2, 2, 0)
    s8 = s4 + pltpu.roll(s4, 4, 0)
    s16 = s8 + pltpu.roll(s8, 8, 0)
    c4 = pltpu.roll(s4, ext - 1, 0)
    c8 = pltpu.roll(s8, ext - 3, 0)
    c16 = pltpu.roll(s16, ext - 7, 0)
    wsum = jnp.where(grp == 0, s2, jnp.where(grp == 1, c4, jnp.where(grp == 2, c8, c16)))
    half = jnp.where(grp == 0, 1, jnp.where(grp == 1, 2, jnp.where(grp == 2, 4, 8)))
    cnt = jnp.minimum(pos + half, seq) - jnp.maximum(pos - half, 0)
    pooled = (wsum / jnp.maximum(cnt, 1).astype(F32) - za)[HALO:HALO + tile]
    y_a = _dot(pooled.astype(BF16), poolw_ref[...]) * pools_ref[...]
    hg1 = half_gate_logits(1)

    h = zc[:, 0:BRANCH_W]
    gate_b = zc[:, BRANCH_W:2 * BRANCH_W]
    gate_c = zc[:, 2 * BRANCH_W:3 * BRANCH_W]
    q = jnp.where(valid, gate_c * h, 0.0)
    q_prev = pltpu.roll(q, 1, 0)[HALO:HALO + tile]
    q_next = pltpu.roll(q, ext - 1, 0)[HALO:HALO + tile]
    conv = (q_prev * convw_ref[0:1, :] + q[HALO:HALO + tile] * convw_ref[1:2, :]
            + q_next * convw_ref[2:3, :] + convb_ref[...])
    y_c = gate_b[HALO:HALO + tile] * conv
    merged = gated(hg0, y_a, 0)
    hg2 = half_gate_logits(2)

    w_lo, w_hi, cls = _tail_route(xh, xl, wr_ref, br_ref)

    u = _gelu_tanh(zd[:, 0:BRANCH_W])
    v = _layer_norm(_gelu_tanh(zd[:, BRANCH_W:]), lng_ref[...], lnb_ref[...]).astype(BF16)
    grp_c = lax.broadcasted_iota(jnp.int32, (CHUNK, BRANCH_W), 1) // GROUP_W
    sps = []
    for c in range(tile // CHUNK):
        vch = v[c * CHUNK:(c + 1) * CHUNK]
        sp = None
        for g in range(N_GROUPS):
            r = _dot(sguw_ref[g], vch)
            sp = r if sp is None else jnp.where(grp_c == g, r, sp)
        sps.append(sp + sgub_ref[...])
    y_d = u * jnp.concatenate(sps, axis=0)
    merged = merged + gated(hg1, yb_ref[...], 1)
    hg3 = half_gate_logits(3)

    _tail_rank(w_lo, w_hi, cls, live, tri_ref, x1e_ref, mrow_ref, cout_ref, base_ref)

    merged = merged + gated(hg2, y_c, 2)
    merged = merged + gated(hg3, y_d, 3)
    pre_ref[...] = DN_ALPHA * x + _dot(merged.astype(BF16), wout_ref[...])


def _tail_norm(pre, ln1g_ref, ln1b_ref, x1e_ref):
    x1 = _layer_norm(pre, ln1g_ref[...], ln1b_ref[...])
    x1e_ref[:, 0:D_MODEL] = x1
    xh = x1.astype(BF16)
    return xh, (x1 - xh.astype(F32)).astype(BF16)


def _tail_route(xh, xl, wr_ref, br_ref):
    tile = TILE
    wr = wr_ref[...]
    wh = wr.astype(BF16)
    wl = (wr - wh.astype(F32)).astype(BF16)
    nt = (((1,), (1,)), ((), ()))
    dot_nt = lambda a, b: lax.dot_general(a, b, nt, preferred_element_type=F32)
    logits = dot_nt(wh, xh) + dot_nt(wh, xl) + dot_nt(wl, xh) + br_ref[...]
    row8 = lax.broadcasted_iota(jnp.int32, (EXPERTS_PER_GROUP, tile), 0)
    lg = logits[0:8]
    m = jnp.max(lg, axis=0, keepdims=True)
    g_idx = jnp.min(jnp.where(lg == m, row8, 8), axis=0, keepdims=True)
    p_group = 1.0 / jnp.sum(jnp.exp(lg - m), axis=0, keepdims=True)
    le = logits[8:16]
    for g in range(1, N_EXPERT_GROUPS):
        le = jnp.where(g_idx == g, logits[8 + 8 * g:16 + 8 * g], le)
    ex = jnp.exp(le - jnp.max(le, axis=0, keepdims=True))
    pe = ex / jnp.sum(ex, axis=0, keepdims=True)
    p1 = jnp.max(pe, axis=0, keepdims=True)
    i1 = jnp.min(jnp.where(pe == p1, row8, 8), axis=0, keepdims=True)
    pe2 = jnp.where(row8 == i1, -1.0, pe)
    p2 = jnp.max(pe2, axis=0, keepdims=True)
    i2 = jnp.min(jnp.where(pe2 == p2, row8, 8), axis=0, keepdims=True)
    first_lo = i1 < i2
    w_lo = p_group * jnp.where(first_lo, p1, p2)
    w_hi = p_group * jnp.where(first_lo, p2, p1)
    cls = g_idx * 64 + jnp.minimum(i1, i2) * EXPERTS_PER_GROUP + jnp.maximum(i1, i2)
    return w_lo, w_hi, cls


def _tail_rank(w_lo, w_hi, cls, live, tri_ref, x1e_ref, mrow_ref, cout_ref, base_ref):
    tile = TILE
    onehot = jnp.logical_and(lax.broadcasted_iota(jnp.int32, (N_CLASS_IDS, tile), 0) == cls, live)
    ohb = onehot.astype(BF16)
    before = _dot(ohb, tri_ref[...])
    base = base_ref[...]
    base_t = jnp.concatenate([base] * (tile // META_W), axis=1)
    rank = jnp.sum(jnp.where(onehot, before + base_t, 0.0), axis=0, keepdims=True)
    new_base = base + _dot(ohb, jnp.ones((tile, META_W), BF16))
    base_ref[...] = new_base
    cout_ref[...] = new_base

    meta_t = jnp.concatenate([w_lo, w_hi, cls.astype(F32), rank, jnp.zeros((META_W - 4, tile), F32)], axis=0)
    x1e_ref[:, D_MODEL:ROW_W] = meta_t.T
    mrow_ref[...] = meta_t[0:8]


def _const_spec(shape):
    nd = len(shape)
    return pl.BlockSpec(shape, lambda i, _nd=nd: (0,) * _nd, pipeline_mode=pl.Buffered(1))


def _mixer(x, y_b, counts_in, lw, seq):
    n_tok = x.shape[0]
    n_tiles = n_tok // TILE
    hb = TILE // HALO
    n_hblk = n_tok // HALO
    consts = [lw["w_in"], lw["b_in"], lw["pool_w"], lw["pool_scale"], lw["conv_w"], lw["conv_b"],
              lw["sgu_ln_g"], lw["sgu_ln_b"], lw["sgu_w"], lw["sgu_b"], lw["branch_proj"], lw["w_out"],
              lw["ln1_g"], lw["ln1_b"], lw["w_r"], lw["b_r"], lw["tri"]]
    cur = lambda i: jnp.minimum(i, n_tiles - 1)
    return pl.pallas_call(
        functools.partial(_mixer_kernel, seq=seq, n_tiles=n_tiles),
        grid=(n_tiles + 1,),
        in_specs=[
            pl.BlockSpec((TILE, D_MODEL), lambda i: (cur(i), 0)),
            pl.BlockSpec((HALO, D_MODEL), lambda i: (jnp.maximum(cur(i) * hb - 1, 0), 0)),
            pl.BlockSpec((HALO, D_MODEL), lambda i: (jnp.minimum((cur(i) + 1) * hb, n_hblk - 1), 0)),
            pl.BlockSpec((TILE, BRANCH_W), lambda i: (cur(i), 0)),
            pl.BlockSpec(COUNT_SHAPE, lambda i: (0, 0)),
        ] + [_const_spec(c.shape) for c in consts],
        out_specs=[
            pl.BlockSpec((TILE, ROW_W), lambda i: (jnp.maximum(i - 1, 0), 0)),
            pl.BlockSpec((None, 8, TILE), lambda i: (jnp.maximum(i - 1, 0), 0, 0)),
            pl.BlockSpec(COUNT_SHAPE, lambda i: (0, 0)),
        ],
        out_shape=[
            jax.ShapeDtypeStruct((n_tok, ROW_W), F32),
            jax.ShapeDtypeStruct((n_tiles, 8, TILE), F32),
            jax.ShapeDtypeStruct(COUNT_SHAPE, F32),
        ],
        scratch_shapes=[pltpu.VMEM(COUNT_SHAPE, F32), pltpu.VMEM((TILE, D_MODEL), F32)],
        compiler_params=pltpu.CompilerParams(
            dimension_semantics=("arbitrary",), vmem_limit_bytes=VMEM_LIMIT),
        name="mixer",
    )(x, x, x, y_b, counts_in, *consts)


def _cumsum_sublanes(a):
    n = a.shape[0]
    row = lax.broadcasted_iota(jnp.int32, a.shape, 0)
    s = 1
    while s < n:
        a = a + jnp.where(row >= s, pltpu.roll(a, s, 0), 0.0)
        s *= 2
    return a


def _tables_kernel(cnt_ref, tab_ref, start_ref, *, nbp):
    cnt = cnt_ref[...]
    nb = jnp.floor((cnt + (MOE_BLOCK - 1)) * (1.0 / MOE_BLOCK))
    end = _cumsum_sublanes(nb)
    start_ref[...] = (end - nb) * MOE_BLOCK
    end_t = jnp.concatenate([end] * (nbp // META_W), axis=1)
    j = lax.broadcasted_iota(jnp.int32, (N_CLASS_IDS, nbp), 1).astype(F32)
    blk_cls = jnp.sum((end_t <= j).astype(F32), axis=0, keepdims=True)
    total = end_t[N_CLASS_IDS - 1:N_CLASS_IDS, :]
    jr = j[0:1, :]
    active = jr < total
    last_cls = jnp.max(jnp.where(active, blk_cls, 0.0), axis=-1, keepdims=True)
    cls_i = jnp.where(active, blk_cls, last_cls).astype(jnp.int32)
    grp = cls_i >> 6
    e_lo = (cls_i >> 3) & 7
    e_hi = cls_i & 7
    bidx = jnp.minimum(jr, total - 1.0).astype(jnp.int32)
    match = lax.broadcasted_iota(jnp.int32, (N_CLASS_IDS, nbp), 0) == cls_i
    reps = nbp // META_W
    cnt_j = jnp.sum(jnp.where(match, jnp.concatenate([cnt] * reps, axis=1), 0.0), axis=0, keepdims=True)
    first_j = jnp.sum(jnp.where(match, jnp.concatenate([end - nb] * reps, axis=1), 0.0), axis=0, keepdims=True)
    n_valid = jnp.where(active, jnp.clip(cnt_j - (jr - first_j) * MOE_BLOCK, 0.0, MOE_BLOCK), 0.0).astype(jnp.int32)
    zero = jnp.zeros_like(bidx)
    tab_ref[...] = jnp.concatenate([e_lo, e_hi, bidx, total.astype(jnp.int32), n_valid, grp, zero, zero], axis=0)


def _tables(counts, nbp):
    return pl.pallas_call(
        functools.partial(_tables_kernel, nbp=nbp),
        out_shape=[jax.ShapeDtypeStruct((8, nbp), jnp.int32),
                   jax.ShapeDtypeStruct(COUNT_SHAPE, F32)],
        name="moe_tables",
    )(counts)


def _dest_kernel(mrow_ref, start_ref, dest_ref, *, tiles):
    start = jnp.concatenate([start_ref[...]] * (TILE // META_W), axis=1)
    cls_ids = lax.broadcasted_iota(jnp.int32, (N_CLASS_IDS, TILE), 0)
    for t in range(tiles):
        cls = mrow_ref[t, 2:3, :].astype(jnp.int32)
        first = jnp.sum(jnp.where(cls_ids == cls, start, 0.0), axis=0, keepdims=True)
        dest_ref[t] = (first + mrow_ref[t, 3:4, :]).astype(jnp.int32)


def _dest(mrow, start):
    n_tiles = mrow.shape[0]
    tiles = math.gcd(n_tiles, DEST_TILES)
    return pl.pallas_call(
        functools.partial(_dest_kernel, tiles=tiles),
        grid=(n_tiles // tiles,),
        in_specs=[pl.BlockSpec((tiles, 8, TILE), lambda i: (i, 0, 0)),
                  pl.BlockSpec(COUNT_SHAPE, lambda i: (0, 0))],
        out_specs=pl.BlockSpec((tiles, 1, TILE), lambda i: (i, 0, 0)),
        out_shape=jax.ShapeDtypeStruct((n_tiles, 1, TILE), jnp.int32),
        compiler_params=pltpu.CompilerParams(dimension_semantics=("arbitrary",)),
        name="moe_dest",
    )(mrow, start)


def _sc_mesh():
    return plsc.VectorSubcoreMesh(core_axis_name="core", subcore_axis_name="subcore")


def _col_chunks(width):
    chunks, c = [], 0
    while c < width:
        w = min(SC_COLS, width - c)
        assert c % w == 0
        chunks.append((c // w, w))
        c += w
    return chunks


_SC_PARAMS = dict(core_axis_name=("core", "subcore"), dimension_semantics=(pltpu.PARALLEL,))


def _sc_scatter_rows(idx, src, n_out):
    n, d = src.shape

    @pl.kernel(out_type=jax.ShapeDtypeStruct((n_out, d), src.dtype), mesh=_sc_mesh(), scratch_types=[],
               compiler_params=pltpu.CompilerParams(use_tc_tiling_on_sc=True))
    def scatter(src_hbm, idx_hbm, out_hbm):
        for cb, cw in _col_chunks(d):
            def body(rows_vmem, idx_vmem, cb=cb, cw=cw):
                pltpu.sync_copy(rows_vmem, out_hbm.at[:, pl.ds(cb * cw, cw)].at[idx_vmem.at[0]])

            pltpu.emit_pipeline(
                body, grid=(n // SC_WINDOW,),
                in_specs=[pl.BlockSpec((SC_WINDOW, cw), lambda i, cb=cb: (i, cb)),
                          pl.BlockSpec((1, SC_WINDOW), lambda i: (0, i))],
                out_specs=[], **_SC_PARAMS)(src_hbm, idx_hbm)

    return scatter(src, idx)


def _sc_gather_rows(idx, src):
    n, d = idx.shape[1], src.shape[1]

    @pl.kernel(out_type=jax.ShapeDtypeStruct((n, d), src.dtype), mesh=_sc_mesh(), scratch_types=[],
               compiler_params=pltpu.CompilerParams(use_tc_tiling_on_sc=True))
    def gather(src_hbm, idx_hbm, out_hbm):
        for cb, cw in _col_chunks(d):
            def body(idx_vmem, rows_vmem, cb=cb, cw=cw):
                pltpu.sync_copy(src_hbm.at[:, pl.ds(cb * cw, cw)].at[idx_vmem.at[0]], rows_vmem)

            pltpu.emit_pipeline(
                body, grid=(n // SC_WINDOW,),
                in_specs=[pl.BlockSpec((1, SC_WINDOW), lambda i: (0, i))],
                out_specs=[pl.BlockSpec((SC_WINDOW, cw), lambda i, cb=cb: (i, cb))],
                **_SC_PARAMS)(idx_hbm, out_hbm)

    return gather(src, idx)


def _expert_kernel(grp_ref, elo_ref, ehi_ref, bidx_ref, nvalid_ref, tot_ref, xb_ref, *refs):
    del grp_ref, bidx_ref
    w1_refs, w3_refs, w2_refs = (refs[k * DMA_SPLIT:(k + 1) * DMA_SPLIT] for k in range(3))
    g_ref, b_ref, o_ref, pre_ref = refs[3 * DMA_SPLIT:]
    cw = D_MODEL // DMA_SPLIT
    j = pl.program_id(0)
    total = tot_ref[0]

    @pl.when(j == 0)
    def _():
        pre_ref[...] = jnp.zeros_like(pre_ref)

    def finish_previous():
        o_ref[...] = _layer_norm(pre_ref[...], g_ref[...], b_ref[...])

    def hidden(half_a, b):
        return ((half_a * (jnp.tanh(half_a) + 1.0)) * b).astype(BF16)

    @pl.when(j < total)
    def _():
        n_valid = nvalid_ref[j]
        x = jnp.where(lax.broadcasted_iota(jnp.int32, (MOE_BLOCK, D_MODEL), 0) < n_valid, xb_ref[:, 0:D_MODEL], 0.0)
        xh = x.astype(BF16)
        e_lo = elo_ref[j]
        e_hi = ehi_ref[j]
        xh_cols = [xh[:, c * cw:(c + 1) * cw] for c in range(DMA_SPLIT)]

        def up(w_refs, e):
            acc = _dot(xh_cols[0], w_refs[0][e])
            for xc, w_ref in zip(xh_cols[1:], w_refs[1:]):
                acc = acc + _dot(xc, w_ref[e])
            return acc

        def down(hid, e):
            return jnp.concatenate([_dot(hid, w_ref[e]) for w_ref in w2_refs], axis=1)

        half_a_lo = up(w1_refs, e_lo)
        b_lo = up(w3_refs, e_lo)
        finish_previous()
        half_a_hi = up(w1_refs, e_hi)
        b_hi = up(w3_refs, e_hi)
        y_lo = down(hidden(half_a_lo, b_lo), e_lo)
        y_hi = down(hidden(half_a_hi, b_hi), e_hi)
        live = lax.broadcasted_iota(jnp.int32, (MOE_BLOCK, 1), 0) < n_valid
        w_lo = jnp.where(live, xb_ref[:, D_MODEL:D_MODEL + 1], 0.0)
        w_hi = jnp.where(live, xb_ref[:, D_MODEL + 1:D_MODEL + 2], 0.0)
        pre_ref[...] = DN_ALPHA * x + (w_lo * y_lo + w_hi * y_hi)

    @pl.when(j == total)
    def _():
        finish_previous()

    @pl.when(j > total)
    def _():
        o_ref[...] = jnp.zeros_like(o_ref)


def _experts(tab, xb, lw):
    n_blocks = xb.shape[0] // MOE_BLOCK
    cw = D_MODEL // DMA_SPLIT
    up_w = [pl.BlockSpec((EXPERTS_PER_GROUP, cw, D_EXPERT), lambda j, g, lo, hi, bi, nv, tot, c=c: (g[j], c, 0),
                         pipeline_mode=pl.Buffered(1)) for c in range(DMA_SPLIT)]
    down_w = [pl.BlockSpec((EXPERTS_PER_GROUP, D_EXPERT, cw), lambda j, g, lo, hi, bi, nv, tot, c=c: (g[j], 0, c),
                           pipeline_mode=pl.Buffered(1)) for c in range(DMA_SPLIT)]
    vec = pl.BlockSpec((1, D_MODEL), lambda j, g, lo, hi, bi, nv, tot: (0, 0))
    grid_spec = pltpu.PrefetchScalarGridSpec(
        num_scalar_prefetch=6,
        grid=(n_blocks + 1,),
        in_specs=[pl.BlockSpec((MOE_BLOCK, ROW_W), lambda j, g, lo, hi, bi, nv, tot: (bi[j], 0)),
                  *up_w, *up_w, *down_w, vec, vec],
        out_specs=pl.BlockSpec((MOE_BLOCK, D_MODEL), lambda j, g, lo, hi, bi, nv, tot: (jnp.maximum(j - 1, 0), 0)),
        scratch_shapes=[pltpu.VMEM((MOE_BLOCK, D_MODEL), F32)],
    )
    return pl.pallas_call(
        _expert_kernel,
        grid_spec=grid_spec,
        out_shape=jax.ShapeDtypeStruct((n_blocks * MOE_BLOCK, D_MODEL), F32),
        compiler_params=pltpu.CompilerParams(
            dimension_semantics=("arbitrary",), vmem_limit_bytes=VMEM_LIMIT),
        name="moe_experts",
    )(tab[5], tab[0], tab[1], tab[2], tab[4], tab[3, 0:1], xb,
      *([lw["w1"]] * DMA_SPLIT), *([lw["w3"]] * DMA_SPLIT), *([lw["w2"]] * DMA_SPLIT), lw["ln2_g"], lw["ln2_b"])


def _layer_weights(p, l):
    row = lambda a: a[l].reshape(1, -1).astype(F32)
    w_r = jnp.zeros((META_W, D_MODEL), F32)
    w_r = w_r.at[0:N_EXPERT_GROUPS].set(p["w_rg"][l].T).at[8:8 + N_EXPERTS].set(p["w_re"][l].T)
    b_r = jnp.zeros((META_W, 1), F32)
    b_r = b_r.at[0:N_EXPERT_GROUPS, 0].set(p["b_rg"][l]).at[N_EXPERT_GROUPS:8, 0].set(NEG_BIG)
    b_r = b_r.at[8:8 + N_EXPERTS, 0].set(p["b_re"][l])
    t = np.arange(TILE)
    gate_half = jnp.asarray(np.where(np.arange(p["w_in"].shape[-1]) >= OFF_G, 0.5, 1.0), F32)
    return {
        "w_in": (p["w_in"][l] * gate_half).astype(BF16),
        "b_in": row(p["b_in"]) * gate_half,
        "w_b": p["w_in"][l][:, OFF_B:OFF_B + BRANCH_W].astype(BF16),
        "b_b": p["b_in"][l][OFF_B:OFF_B + BRANCH_W].reshape(1, -1),
        "pool_w": jax.scipy.linalg.block_diag(*[p["pool_w"][l][g] for g in range(N_GROUPS)]).astype(BF16),
        "pool_scale": row(p["pool_scale"]),
        "conv_w": p["conv_w"][l],
        "conv_b": row(p["conv_b"]),
        "sgu_ln_g": row(p["sgu_ln_g"]),
        "sgu_ln_b": row(p["sgu_ln_b"]),
        "sgu_w": p["sgu_w"][l].astype(BF16),
        "sgu_b": jnp.repeat(p["sgu_b"][l].T, GROUP_W, axis=1),
        "branch_proj": (0.5 * p["branch_proj"][l]).astype(BF16),
        "w_out": p["w_out"][l].astype(BF16),
        "ln1_g": row(p["ln1_g"]),
        "ln1_b": row(p["ln1_b"]),
        "w_r": w_r,
        "b_r": b_r,
        "tri": jnp.asarray((t[:, None] < t[None, :]).astype(np.float32)).astype(BF16),
        "w1": (0.5 * p["w1"][l]).astype(BF16),
        "w3": p["w3"][l].astype(BF16),
        "w2": p["w2"][l].astype(BF16),
        "ln2_g": row(p["ln2_g"]),
        "ln2_b": row(p["ln2_b"]),
    }


def _encoder_layer(xs, seqs, lw):
    zero_counts = jnp.zeros(COUNT_SHAPE, F32)
    routed = []
    for x, seq in zip(xs, seqs):
        n_blocks = x.shape[0] // MOE_BLOCK + N_REACHABLE_CLASSES
        nbp = -(-(n_blocks + 1) // META_W) * META_W
        y_b = _fourier_mixer(x, lw["w_b"], lw["b_b"], seq)
        x1e, mrow, counts = _mixer(x, y_b, zero_counts, lw, seq)
        tab, start = _tables(counts, nbp)
        dest = _dest(mrow, start).reshape(1, -1)
        routed.append((tab, dest, _sc_scatter_rows(dest, x1e, n_blocks * MOE_BLOCK)))
    return [_sc_gather_rows(dest, _experts(tab, xb, lw)) for tab, dest, xb in routed]


def kernel(x_prompt, x_sample, w_in, b_in, pool_w, pool_scale, conv_w, conv_b, sgu_ln_g, sgu_ln_b,
           sgu_w, sgu_b, branch_proj, w_out, ln1_g, ln1_b, w_rg, b_rg, w_re, b_re, w1, w3, w2,
           ln2_g, ln2_b):
    p = dict(w_in=w_in, b_in=b_in, pool_w=pool_w, pool_scale=pool_scale, conv_w=conv_w, conv_b=conv_b,
             sgu_ln_g=sgu_ln_g, sgu_ln_b=sgu_ln_b, sgu_w=sgu_w, sgu_b=sgu_b, branch_proj=branch_proj,
             w_out=w_out, ln1_g=ln1_g, ln1_b=ln1_b, w_rg=w_rg, b_rg=b_rg, w_re=w_re, b_re=b_re,
             w1=w1, w3=w3, w2=w2, ln2_g=ln2_g, ln2_b=ln2_b)
    shapes = (x_prompt.shape, x_sample.shape)
    seqs = [s[1] for s in shapes]
    xs = [x_prompt.reshape(-1, D_MODEL), x_sample.reshape(-1, D_MODEL)]
    for l in range(w_in.shape[0]):
        xs = _encoder_layer(xs, seqs, _layer_weights(p, l))
    return tuple(x.reshape(s) for x, s in zip(xs, shapes))
```

```python
import functools
import math

import numpy as np
import jax
import jax.numpy as jnp
from jax import lax
from jax.experimental import pallas as pl
from jax.experimental.pallas import tpu as pltpu
from jax.experimental.pallas import tpu_sc as plsc

F32 = jnp.float32
BF16 = jnp.bfloat16

D_MODEL = 1024
DEPTH = 2
BRANCH_W = 256
N_GROUPS = 4
GROUP_W = 64
POOL_WINDOWS = (2, 4, 8, 16)
CHUNK = 128
OFF_A = 0
OFF_B = 256
OFF_CH = 512
OFF_DU = 1280
OFF_G = 1792
N_EXPERT_GROUPS = 4
EXPERTS_PER_GROUP = 8
N_EXPERTS = 32
D_EXPERT = 512
MOE_BLOCK = 256
DN_ALPHA = (2 * DEPTH) ** 0.25
LN_EPS = 1e-5

N_CLASS_IDS = 256
N_REACHABLE_CLASSES = N_EXPERT_GROUPS * (EXPERTS_PER_GROUP * (EXPERTS_PER_GROUP - 1) // 2)
META_W = 128
ROW_W = D_MODEL + META_W
COUNT_SHAPE = (N_CLASS_IDS, META_W)
HALO = 16
TILE = 512
DEST_TILES = 8
DMA_SPLIT = 4
SC_WINDOW = 128
SC_COLS = 256
NEG_BIG = -1e30
VMEM_LIMIT = 56 * 1024 * 1024


def _dot(a, b):
    return jnp.dot(a, b, preferred_element_type=F32)


def _layer_norm(x, g, b):
    mu = jnp.mean(x, axis=-1, keepdims=True)
    xc = x - mu
    var = jnp.mean(xc * xc, axis=-1, keepdims=True)
    return xc * lax.rsqrt(var + LN_EPS) * g + b


def _gelu_tanh(x):
    return 0.5 * x * (1.0 + jnp.tanh(math.sqrt(2.0 / math.pi) * (x + 0.044715 * (x * x * x))))


def _fft_factors(seq):
    n1 = 1 << (int(math.log2(seq)) // 2)
    return n1, seq // n1


@functools.lru_cache(maxsize=None)
def _fft_tables(seq):
    n1, n2 = _fft_factors(seq)
    c = np.arange(GROUP_W)
    ang = 2.0 * np.pi * ((c[:, None] * c[None, :]) % GROUP_W) / GROUP_W
    eye = np.eye(N_GROUPS)
    cs = np.concatenate([np.kron(eye, np.cos(ang)), np.kron(eye, np.sin(ang))], axis=1)
    k1 = np.arange(n1)
    t1 = np.arange(n1)
    t2 = np.arange(n2)
    t = t1[None, None, :] * n2 + t2[:, None, None]
    ang1 = 2.0 * np.pi * ((k1[None, :, None] * t) % seq) / seq
    gc, gs = np.cos(ang1), np.sin(ang1)
    lt = np.concatenate([np.concatenate([gc, -gs], axis=2), np.concatenate([-gs, -gc], axis=2)], axis=1)
    k2 = np.arange(n2)
    ang2 = 2.0 * np.pi * ((k2[:, None] * t2[None, :]) % n2) / n2
    scale = 1.0 / math.sqrt(seq * GROUP_W)
    to16 = lambda a: jnp.asarray(a, dtype=F32).astype(BF16)
    return to16(cs), to16(lt), to16(np.cos(ang2) * scale), to16(np.sin(ang2) * scale)


def _fft_stage1_kernel(*refs, n1, tb):
    x_refs, (wb_ref, bb_ref, cs_ref, lt_ref, o_ref, u_ref) = refs[:DMA_SPLIT], refs[DMA_SPLIT:]
    x = jnp.concatenate([r[...].reshape(n1 * tb, D_MODEL // DMA_SPLIT).astype(BF16) for r in x_refs], axis=1)
    zb = _dot(x, wb_ref[...]) + bb_ref[...]
    u = _dot(zb.astype(BF16), cs_ref[...])
    for c in range(4):
        u_ref[c] = u[:, c * 128:(c + 1) * 128]
    for j in range(tb):
        q = [u_ref[c, pl.ds(j, n1, stride=tb), :] for c in range(4)]
        stacked = jnp.concatenate([jnp.concatenate(q[0:2], axis=1),
                                   jnp.concatenate(q[2:4], axis=1)], axis=0).astype(BF16)
        b = _dot(lt_ref[j], stacked)
        o_ref[:, j * 512:j * 512 + BRANCH_W] = b[:n1].astype(BF16)
        o_ref[:, j * 512 + BRANCH_W:(j + 1) * 512] = b[n1:].astype(BF16)


def _fft_stage2_kernel(b_ref, c2_ref, s2_ref, o_ref, *, kb):
    for k in range(kb):
        slab = b_ref[k]
        y = _dot(c2_ref[...], slab[:, :BRANCH_W]) + _dot(s2_ref[...], slab[:, BRANCH_W:])
        o_ref[:, k * BRANCH_W:(k + 1) * BRANCH_W] = y.astype(BF16)


def _fourier_mixer(x, w_b, b_b, seq):
    n_tok = x.shape[0]
    bsz = n_tok // seq
    n1, n2 = _fft_factors(seq)
    tb = max(8, min(16, (4 << 20) // (n1 * D_MODEL * 4)))
    kb = min(n1, (2 << 20) // (n2 * 512 * 2))
    cs, lt, c2, s2 = _fft_tables(seq)
    x4 = x.reshape(bsz, n1, n2, D_MODEL)
    xw = D_MODEL // DMA_SPLIT
    bt = pl.pallas_call(
        functools.partial(_fft_stage1_kernel, n1=n1, tb=tb),
        grid=(bsz, n2 // tb),
        in_specs=[pl.BlockSpec((None, n1, tb, xw), lambda b, j, c=c: (b, 0, j, c)) for c in range(DMA_SPLIT)] + [
            pl.BlockSpec((D_MODEL, BRANCH_W), lambda b, j: (0, 0)),
            pl.BlockSpec((1, BRANCH_W), lambda b, j: (0, 0)),
            pl.BlockSpec((BRANCH_W, 2 * BRANCH_W), lambda b, j: (0, 0)),
            pl.BlockSpec((tb, 2 * n1, 2 * n1), lambda b, j: (j, 0, 0)),
        ],
        out_specs=pl.BlockSpec((None, n1, tb * 512), lambda b, j: (b, 0, j)),
        out_shape=jax.ShapeDtypeStruct((bsz, n1, n2 * 512), BF16),
        scratch_shapes=[pltpu.VMEM((4, n1 * tb, 128), F32)],
        compiler_params=pltpu.CompilerParams(
            dimension_semantics=("arbitrary", "arbitrary"), vmem_limit_bytes=VMEM_LIMIT),
        name="fft_stage1",
    )(*([x4] * DMA_SPLIT), w_b, b_b, cs, lt)
    bt4 = bt.reshape(bsz, n1, n2, 512)
    y = pl.pallas_call(
        functools.partial(_fft_stage2_kernel, kb=kb),
        grid=(bsz, n1 // kb),
        in_specs=[
            pl.BlockSpec((None, kb, n2, 512), lambda b, i: (b, i, 0, 0)),
            pl.BlockSpec((n2, n2), lambda b, i: (0, 0)),
            pl.BlockSpec((n2, n2), lambda b, i: (0, 0)),
        ],
        out_specs=pl.BlockSpec((None, n2, kb * BRANCH_W), lambda b, i: (b, 0, i)),
        out_shape=jax.ShapeDtypeStruct((bsz, n2, n1 * BRANCH_W), BF16),
        compiler_params=pltpu.CompilerParams(
            dimension_semantics=("arbitrary", "arbitrary"), vmem_limit_bytes=VMEM_LIMIT),
        name="fft_stage2",
    )(bt4, c2, s2)
    return y.reshape(n_tok, BRANCH_W)


def _mixer_kernel(x_ref, xp_ref, xn_ref, yb_ref, cin_ref,
                  win_ref, bin_ref, poolw_ref, pools_ref, convw_ref, convb_ref,
                  lng_ref, lnb_ref, sguw_ref, sgub_ref, bp_ref, wout_ref,
                  ln1g_ref, ln1b_ref, wr_ref, br_ref, tri_ref,
                  x1e_ref, mrow_ref, cout_ref, base_ref, pre_ref, *, seq, n_tiles):
    i = pl.program_id(0)
    tile = TILE
    ext = tile + 2 * HALO
    p0 = (jnp.minimum(i, n_tiles - 1) % (seq // tile)) * tile

    @pl.when(i == 0)
    def _():
        base_ref[...] = cin_ref[...]
        pre_ref[...] = jnp.zeros_like(pre_ref)

    live = i > 0

    x = x_ref[...]
    xc = x.astype(BF16)
    xe = jnp.concatenate([xp_ref[...].astype(BF16), xc, xn_ref[...].astype(BF16)], axis=0)
    pos = p0 - HALO + lax.broadcasted_iota(jnp.int32, (ext, BRANCH_W), 0)
    valid = (pos >= 0) & (pos < seq)
    lane = lax.broadcasted_iota(jnp.int32, (ext, BRANCH_W), 1)
    grp = lane // GROUP_W

    def half_gate_logits(k):
        lo = OFF_G + k * D_MODEL
        return _dot(xc, win_ref[:, lo:lo + D_MODEL]) + bin_ref[:, lo:lo + D_MODEL]

    def gated(half_gz, y_k, k):
        return (jnp.tanh(half_gz) + 1.0) * _dot(y_k.astype(BF16), bp_ref[k])

    za = jnp.where(valid, _dot(xe, win_ref[:, OFF_A:OFF_A + BRANCH_W]) + bin_ref[:, OFF_A:OFF_A + BRANCH_W], 0.0)
    zc = _dot(xe, win_ref[:, OFF_CH:OFF_DU]) + bin_ref[:, OFF_CH:OFF_DU]
    zd = _dot(xc, win_ref[:, OFF_DU:OFF_G]) + bin_ref[:, OFF_DU:OFF_G]
    hg0 = half_gate_logits(0)

    xh, xl = _tail_norm(pre_ref[...], ln1g_ref, ln1b_ref, x1e_ref)

    s2 = za + pltpu.roll(za, 1, 0)
    s4 = s2 + pltpu.roll(s2, 2, 0)
    s8 = s4 + pltpu.roll(s4, 4, 0)
    s16 = s8 + pltpu.roll(s8, 8, 0)
    c4 = pltpu.roll(s4, ext - 1, 0)
    c8 = pltpu.roll(s8, ext - 3, 0)
    c16 = pltpu.roll(s16, ext - 7, 0)
    wsum = jnp.where(grp == 0, s2, jnp.where(grp == 1, c4, jnp.where(grp == 2, c8, c16)))
    half = jnp.where(grp == 0, 1, jnp.where(grp == 1, 2, jnp.where(grp == 2, 4, 8)))
    cnt = jnp.minimum(pos + half, seq) - jnp.maximum(pos - half, 0)
    pooled = (wsum / jnp.maximum(cnt, 1).astype(F32) - za)[HALO:HALO + tile]
    y_a = _dot(pooled.astype(BF16), poolw_ref[...]) * pools_ref[...]
    hg1 = half_gate_logits(1)

    h = zc[:, 0:BRANCH_W]
    gate_b = zc[:, BRANCH_W:2 * BRANCH_W]
    gate_c = zc[:, 2 * BRANCH_W:3 * BRANCH_W]
    q = jnp.where(valid, gate_c * h, 0.0)
    q_prev = pltpu.roll(q, 1, 0)[HALO:HALO + tile]
    q_next = pltpu.roll(q, ext - 1, 0)[HALO:HALO + tile]
    conv = (q_prev * convw_ref[0:1, :] + q[HALO:HALO + tile] * convw_ref[1:2, :]
            + q_next * convw_ref[2:3, :] + convb_ref[...])
    y_c = gate_b[HALO:HALO + tile] * conv
    merged = gated(hg0, y_a, 0)
    hg2 = half_gate_logits(2)

    w_lo, w_hi, cls = _tail_route(xh, xl, wr_ref, br_ref)

    u = _gelu_tanh(zd[:, 0:BRANCH_W])
    v = _layer_norm(_gelu_tanh(zd[:, BRANCH_W:]), lng_ref[...], lnb_ref[...]).astype(BF16)
    grp_c = lax.broadcasted_iota(jnp.int32, (CHUNK, BRANCH_W), 1) // GROUP_W
    sps = []
    for c in range(tile // CHUNK):
        vch = v[c * CHUNK:(c + 1) * CHUNK]
        sp = None
        for g in range(N_GROUPS):
            r = _dot(sguw_ref[g], vch)
            sp = r if sp is None else jnp.where(grp_c == g, r, sp)
        sps.append(sp + sgub_ref[...])
    y_d = u * jnp.concatenate(sps, axis=0)
    merged = merged + gated(hg1, yb_ref[...], 1)
    hg3 = half_gate_logits(3)

    _tail_rank(w_lo, w_hi, cls, live, tri_ref, x1e_ref, mrow_ref, cout_ref, base_ref)

    merged = merged + gated(hg2, y_c, 2)
    merged = merged + gated(hg3, y_d, 3)
    pre_ref[...] = DN_ALPHA * x + _dot(merged.astype(BF16), wout_ref[...])


def _tail_norm(pre, ln1g_ref, ln1b_ref, x1e_ref):
    x1 = _layer_norm(pre, ln1g_ref[...], ln1b_ref[...])
    x1e_ref[:, 0:D_MODEL] = x1
    xh = x1.astype(BF16)
    return xh, (x1 - xh.astype(F32)).astype(BF16)


def _tail_route(xh, xl, wr_ref, br_ref):
    tile = TILE
    wr = wr_ref[...]
    wh = wr.astype(BF16)
    wl = (wr - wh.astype(F32)).astype(BF16)
    nt = (((1,), (1,)), ((), ()))
    dot_nt = lambda a, b: lax.dot_general(a, b, nt, preferred_element_type=F32)
    logits = dot_nt(wh, xh) + dot_nt(wh, xl) + dot_nt(wl, xh) + br_ref[...]
    row8 = lax.broadcasted_iota(jnp.int32, (EXPERTS_PER_GROUP, tile), 0)
    lg = logits[0:8]
    m = jnp.max(lg, axis=0, keepdims=True)
    g_idx = jnp.min(jnp.where(lg == m, row8, 8), axis=0, keepdims=True)
    p_group = 1.0 / jnp.sum(jnp.exp(lg - m), axis=0, keepdims=True)
    le = logits[8:16]
    for g in range(1, N_EXPERT_GROUPS):
        le = jnp.where(g_idx == g, logits[8 + 8 * g:16 + 8 * g], le)
    ex = jnp.exp(le - jnp.max(le, axis=0, keepdims=True))
    pe = ex / jnp.sum(ex, axis=0, keepdims=True)
    p1 = jnp.max(pe, axis=0, keepdims=True)
    i1 = jnp.min(jnp.where(pe == p1, row8, 8), axis=0, keepdims=True)
    pe2 = jnp.where(row8 == i1, -1.0, pe)
    p2 = jnp.max(pe2, axis=0, keepdims=True)
    i2 = jnp.min(jnp.where(pe2 == p2, row8, 8), axis=0, keepdims=True)
    first_lo = i1 < i2
    w_lo = p_group * jnp.where(first_lo, p1, p2)
    w_hi = p_group * jnp.where(first_lo, p2, p1)
    cls = g_idx * 64 + jnp.minimum(i1, i2) * EXPERTS_PER_GROUP + jnp.maximum(i1, i2)
    return w_lo, w_hi, cls


def _tail_rank(w_lo, w_hi, cls, live, tri_ref, x1e_ref, mrow_ref, cout_ref, base_ref):
    tile = TILE
    onehot = jnp.logical_and(lax.broadcasted_iota(jnp.int32, (N_CLASS_IDS, tile), 0) == cls, live)
    ohb = onehot.astype(BF16)
    before = _dot(ohb, tri_ref[...])
    base = base_ref[...]
    base_t = jnp.concatenate([base] * (tile // META_W), axis=1)
    rank = jnp.sum(jnp.where(onehot, before + base_t, 0.0), axis=0, keepdims=True)
    new_base = base + _dot(ohb, jnp.ones((tile, META_W), BF16))
    base_ref[...] = new_base
    cout_ref[...] = new_base

    meta_t = jnp.concatenate([w_lo, w_hi, cls.astype(F32), rank, jnp.zeros((META_W - 4, tile), F32)], axis=0)
    x1e_ref[:, D_MODEL:ROW_W] = meta_t.T
    mrow_ref[...] = meta_t[0:8]


def _const_spec(shape):
    nd = len(shape)
    return pl.BlockSpec(shape, lambda i, _nd=nd: (0,) * _nd, pipeline_mode=pl.Buffered(1))


def _mixer(x, y_b, counts_in, lw, seq):
    n_tok = x.shape[0]
    n_tiles = n_tok // TILE
    hb = TILE // HALO
    n_hblk = n_tok // HALO
    consts = [lw["w_in"], lw["b_in"], lw["pool_w"], lw["pool_scale"], lw["conv_w"], lw["conv_b"],
              lw["sgu_ln_g"], lw["sgu_ln_b"], lw["sgu_w"], lw["sgu_b"], lw["branch_proj"], lw["w_out"],
              lw["ln1_g"], lw["ln1_b"], lw["w_r"], lw["b_r"], lw["tri"]]
    cur = lambda i: jnp.minimum(i, n_tiles - 1)
    return pl.pallas_call(
        functools.partial(_mixer_kernel, seq=seq, n_tiles=n_tiles),
        grid=(n_tiles + 1,),
        in_specs=[
            pl.BlockSpec((TILE, D_MODEL), lambda i: (cur(i), 0)),
            pl.BlockSpec((HALO, D_MODEL), lambda i: (jnp.maximum(cur(i) * hb - 1, 0), 0)),
            pl.BlockSpec((HALO, D_MODEL), lambda i: (jnp.minimum((cur(i) + 1) * hb, n_hblk - 1), 0)),
            pl.BlockSpec((TILE, BRANCH_W), lambda i: (cur(i), 0)),
            pl.BlockSpec(COUNT_SHAPE, lambda i: (0, 0)),
        ] + [_const_spec(c.shape) for c in consts],
        out_specs=[
            pl.BlockSpec((TILE, ROW_W), lambda i: (jnp.maximum(i - 1, 0), 0)),
            pl.BlockSpec((None, 8, TILE), lambda i: (jnp.maximum(i - 1, 0), 0, 0)),
            pl.BlockSpec(COUNT_SHAPE, lambda i: (0, 0)),
        ],
        out_shape=[
            jax.ShapeDtypeStruct((n_tok, ROW_W), F32),
            jax.ShapeDtypeStruct((n_tiles, 8, TILE), F32),
            jax.ShapeDtypeStruct(COUNT_SHAPE, F32),
        ],
        scratch_shapes=[pltpu.VMEM(COUNT_SHAPE, F32), pltpu.VMEM((TILE, D_MODEL), F32)],
        compiler_params=pltpu.CompilerParams(
            dimension_semantics=("arbitrary",), vmem_limit_bytes=VMEM_LIMIT),
        name="mixer",
    )(x, x, x, y_b, counts_in, *consts)


def _cumsum_sublanes(a):
    n = a.shape[0]
    row = lax.broadcasted_iota(jnp.int32, a.shape, 0)
    s = 1
    while s < n:
        a = a + jnp.where(row >= s, pltpu.roll(a, s, 0), 0.0)
        s *= 2
    return a


def _tables_kernel(cnt_ref, tab_ref, start_ref, *, nbp):
    cnt = cnt_ref[...]
    nb = jnp.floor((cnt + (MOE_BLOCK - 1)) * (1.0 / MOE_BLOCK))
    end = _cumsum_sublanes(nb)
    start_ref[...] = (end - nb) * MOE_BLOCK
    end_t = jnp.concatenate([end] * (nbp // META_W), axis=1)
    j = lax.broadcasted_iota(jnp.int32, (N_CLASS_IDS, nbp), 1).astype(F32)
    blk_cls = jnp.sum((end_t <= j).astype(F32), axis=0, keepdims=True)
    total = end_t[N_CLASS_IDS - 1:N_CLASS_IDS, :]
    jr = j[0:1, :]
    active = jr < total
    last_cls = jnp.max(jnp.where(active, blk_cls, 0.0), axis=-1, keepdims=True)
    cls_i = jnp.where(active, blk_cls, last_cls).astype(jnp.int32)
    grp = cls_i >> 6
    e_lo = (cls_i >> 3) & 7
    e_hi = cls_i & 7
    bidx = jnp.minimum(jr, total - 1.0).astype(jnp.int32)
    match = lax.broadcasted_iota(jnp.int32, (N_CLASS_IDS, nbp), 0) == cls_i
    reps = nbp // META_W
    cnt_j = jnp.sum(jnp.where(match, jnp.concatenate([cnt] * reps, axis=1), 0.0), axis=0, keepdims=True)
    first_j = jnp.sum(jnp.where(match, jnp.concatenate([end - nb] * reps, axis=1), 0.0), axis=0, keepdims=True)
    n_valid = jnp.where(active, jnp.clip(cnt_j - (jr - first_j) * MOE_BLOCK, 0.0, MOE_BLOCK), 0.0).astype(jnp.int32)
    zero = jnp.zeros_like(bidx)
    tab_ref[...] = jnp.concatenate([e_lo, e_hi, bidx, total.astype(jnp.int32), n_valid, grp, zero, zero], axis=0)


def _tables(counts, nbp):
    return pl.pallas_call(
        functools.partial(_tables_kernel, nbp=nbp),
        out_shape=[jax.ShapeDtypeStruct((8, nbp), jnp.int32),
                   jax.ShapeDtypeStruct(COUNT_SHAPE, F32)],
        name="moe_tables",
    )(counts)


def _dest_kernel(mrow_ref, start_ref, dest_ref, *, tiles):
    start = jnp.concatenate([start_ref[...]] * (TILE // META_W), axis=1)
    cls_ids = lax.broadcasted_iota(jnp.int32, (N_CLASS_IDS, TILE), 0)
    for t in range(tiles):
        cls = mrow_ref[t, 2:3, :].astype(jnp.int32)
        first = jnp.sum(jnp.where(cls_ids == cls, start, 0.0), axis=0, keepdims=True)
        dest_ref[t] = (first + mrow_ref[t, 3:4, :]).astype(jnp.int32)


def _dest(mrow, start):
    n_tiles = mrow.shape[0]
    tiles = math.gcd(n_tiles, DEST_TILES)
    return pl.pallas_call(
        functools.partial(_dest_kernel, tiles=tiles),
        grid=(n_tiles // tiles,),
        in_specs=[pl.BlockSpec((tiles, 8, TILE), lambda i: (i, 0, 0)),
                  pl.BlockSpec(COUNT_SHAPE, lambda i: (0, 0))],
        out_specs=pl.BlockSpec((tiles, 1, TILE), lambda i: (i, 0, 0)),
        out_shape=jax.ShapeDtypeStruct((n_tiles, 1, TILE), jnp.int32),
        compiler_params=pltpu.CompilerParams(dimension_semantics=("arbitrary",)),
        name="moe_dest",
    )(mrow, start)


def _sc_mesh():
    return plsc.VectorSubcoreMesh(core_axis_name="core", subcore_axis_name="subcore")


def _col_chunks(width):
    chunks, c = [], 0
    while c < width:
        w = min(SC_COLS, width - c)
        assert c % w == 0
        chunks.append((c // w, w))
        c += w
    return chunks


_SC_PARAMS = dict(core_axis_name=("core", "subcore"), dimension_semantics=(pltpu.PARALLEL,))


def _sc_scatter_rows(idx, src, n_out):
    n, d = src.shape

    @pl.kernel(out_type=jax.ShapeDtypeStruct((n_out, d), src.dtype), mesh=_sc_mesh(), scratch_types=[],
               compiler_params=pltpu.CompilerParams(use_tc_tiling_on_sc=True))
    def scatter(src_hbm, idx_hbm, out_hbm):
        for cb, cw in _col_chunks(d):
            def body(rows_vmem, idx_vmem, cb=cb, cw=cw):
                pltpu.sync_copy(rows_vmem, out_hbm.at[:, pl.ds(cb * cw, cw)].at[idx_vmem.at[0]])

            pltpu.emit_pipeline(
                body, grid=(n // SC_WINDOW,),
                in_specs=[pl.BlockSpec((SC_WINDOW, cw), lambda i, cb=cb: (i, cb)),
                          pl.BlockSpec((1, SC_WINDOW), lambda i: (0, i))],
                out_specs=[], **_SC_PARAMS)(src_hbm, idx_hbm)

    return scatter(src, idx)


def _sc_gather_rows(idx, src):
    n, d = idx.shape[1], src.shape[1]

    @pl.kernel(out_type=jax.ShapeDtypeStruct((n, d), src.dtype), mesh=_sc_mesh(), scratch_types=[],
               compiler_params=pltpu.CompilerParams(use_tc_tiling_on_sc=True))
    def gather(src_hbm, idx_hbm, out_hbm):
        for cb, cw in _col_chunks(d):
            def body(idx_vmem, rows_vmem, cb=cb, cw=cw):
                pltpu.sync_copy(src_hbm.at[:, pl.ds(cb * cw, cw)].at[idx_vmem.at[0]], rows_vmem)

            pltpu.emit_pipeline(
                body, grid=(n // SC_WINDOW,),
                in_specs=[pl.BlockSpec((1, SC_WINDOW), lambda i: (0, i))],
                out_specs=[pl.BlockSpec((SC_WINDOW, cw), lambda i, cb=cb: (i, cb))],
                **_SC_PARAMS)(idx_hbm, out_hbm)

    return gather(src, idx)


def _expert_kernel(grp_ref, elo_ref, ehi_ref, bidx_ref, nvalid_ref, tot_ref,
                   xb_ref, w1_ref, w3_ref, w2_ref, g_ref, b_ref, o_ref, pre_ref):
    del grp_ref, bidx_ref
    j = pl.program_id(0)
    total = tot_ref[0]

    @pl.when(j == 0)
    def _():
        pre_ref[...] = jnp.zeros_like(pre_ref)

    def finish_previous():
        o_ref[...] = _layer_norm(pre_ref[...], g_ref[...], b_ref[...])

    def hidden(half_a, b):
        return ((half_a * (jnp.tanh(half_a) + 1.0)) * b).astype(BF16)

    @pl.when(j < total)
    def _():
        n_valid = nvalid_ref[j]
        x = jnp.where(lax.broadcasted_iota(jnp.int32, (MOE_BLOCK, D_MODEL), 0) < n_valid, xb_ref[:, 0:D_MODEL], 0.0)
        xh = x.astype(BF16)
        e_lo = elo_ref[j]
        e_hi = ehi_ref[j]
        half_a_lo = _dot(xh, w1_ref[e_lo])
        b_lo = _dot(xh, w3_ref[e_lo])
        finish_previous()
        half_a_hi = _dot(xh, w1_ref[e_hi])
        b_hi = _dot(xh, w3_ref[e_hi])
        y_lo = _dot(hidden(half_a_lo, b_lo), w2_ref[e_lo])
        y_hi = _dot(hidden(half_a_hi, b_hi), w2_ref[e_hi])
        live = lax.broadcasted_iota(jnp.int32, (MOE_BLOCK, 1), 0) < n_valid
        w_lo = jnp.where(live, xb_ref[:, D_MODEL:D_MODEL + 1], 0.0)
        w_hi = jnp.where(live, xb_ref[:, D_MODEL + 1:D_MODEL + 2], 0.0)
        pre_ref[...] = DN_ALPHA * x + (w_lo * y_lo + w_hi * y_hi)

    @pl.when(j == total)
    def _():
        finish_previous()


def _experts(tab, xb, lw):
    n_blocks = xb.shape[0] // MOE_BLOCK
    grp_w = lambda shape: pl.BlockSpec((EXPERTS_PER_GROUP,) + shape, lambda j, g, lo, hi, bi, nv, tot: (g[j], 0, 0),
                                       pipeline_mode=pl.Buffered(1))
    vec = pl.BlockSpec((1, D_MODEL), lambda j, g, lo, hi, bi, nv, tot: (0, 0))
    grid_spec = pltpu.PrefetchScalarGridSpec(
        num_scalar_prefetch=6,
        grid=(n_blocks + 1,),
        in_specs=[pl.BlockSpec((MOE_BLOCK, ROW_W), lambda j, g, lo, hi, bi, nv, tot: (bi[j], 0)),
                  grp_w((D_MODEL, D_EXPERT)), grp_w((D_MODEL, D_EXPERT)), grp_w((D_EXPERT, D_MODEL)), vec, vec],
        out_specs=pl.BlockSpec((MOE_BLOCK, D_MODEL),
                               lambda j, g, lo, hi, bi, nv, tot: (jnp.maximum(jnp.minimum(j, tot[0]) - 1, 0), 0)),
        scratch_shapes=[pltpu.VMEM((MOE_BLOCK, D_MODEL), F32)],
    )
    return pl.pallas_call(
        _expert_kernel,
        grid_spec=grid_spec,
        out_shape=jax.ShapeDtypeStruct((n_blocks * MOE_BLOCK, D_MODEL), F32),
        compiler_params=pltpu.CompilerParams(
            dimension_semantics=("arbitrary",), vmem_limit_bytes=VMEM_LIMIT),
        name="moe_experts",
    )(tab[5], tab[0], tab[1], tab[2], tab[4], tab[3, 0:1], xb,
      lw["w1"], lw["w3"], lw["w2"], lw["ln2_g"], lw["ln2_b"])


def _layer_weights(p, l):
    row = lambda a: a[l].reshape(1, -1).astype(F32)
    w_r = jnp.zeros((META_W, D_MODEL), F32)
    w_r = w_r.at[0:N_EXPERT_GROUPS].set(p["w_rg"][l].T).at[8:8 + N_EXPERTS].set(p["w_re"][l].T)
    b_r = jnp.zeros((META_W, 1), F32)
    b_r = b_r.at[0:N_EXPERT_GROUPS, 0].set(p["b_rg"][l]).at[N_EXPERT_GROUPS:8, 0].set(NEG_BIG)
    b_r = b_r.at[8:8 + N_EXPERTS, 0].set(p["b_re"][l])
    t = np.arange(TILE)
    gate_half = jnp.asarray(np.where(np.arange(p["w_in"].shape[-1]) >= OFF_G, 0.5, 1.0), F32)
    return {
        "w_in": (p["w_in"][l] * gate_half).astype(BF16),
        "b_in": row(p["b_in"]) * gate_half,
        "w_b": p["w_in"][l][:, OFF_B:OFF_B + BRANCH_W].astype(BF16),
        "b_b": p["b_in"][l][OFF_B:OFF_B + BRANCH_W].reshape(1, -1),
        "pool_w": jax.scipy.linalg.block_diag(*[p["pool_w"][l][g] for g in range(N_GROUPS)]).astype(BF16),
        "pool_scale": row(p["pool_scale"]),
        "conv_w": p["conv_w"][l],
        "conv_b": row(p["conv_b"]),
        "sgu_ln_g": row(p["sgu_ln_g"]),
        "sgu_ln_b": row(p["sgu_ln_b"]),
        "sgu_w": p["sgu_w"][l].astype(BF16),
        "sgu_b": jnp.repeat(p["sgu_b"][l].T, GROUP_W, axis=1),
        "branch_proj": (0.5 * p["branch_proj"][l]).astype(BF16),
        "w_out": p["w_out"][l].astype(BF16),
        "ln1_g": row(p["ln1_g"]),
        "ln1_b": row(p["ln1_b"]),
        "w_r": w_r,
        "b_r": b_r,
        "tri": jnp.asarray((t[:, None] < t[None, :]).astype(np.float32)).astype(BF16),
        "w1": (0.5 * p["w1"][l]).astype(BF16),
        "w3": p["w3"][l].astype(BF16),
        "w2": p["w2"][l].astype(BF16),
        "ln2_g": row(p["ln2_g"]),
        "ln2_b": row(p["ln2_b"]),
    }


def _encoder_layer(xs, seqs, lw):
    zero_counts = jnp.zeros(COUNT_SHAPE, F32)
    routed = []
    for x, seq in zip(xs, seqs):
        n_blocks = x.shape[0] // MOE_BLOCK + N_REACHABLE_CLASSES
        nbp = -(-(n_blocks + 1) // META_W) * META_W
        y_b = _fourier_mixer(x, lw["w_b"], lw["b_b"], seq)
        x1e, mrow, counts = _mixer(x, y_b, zero_counts, lw, seq)
        tab, start = _tables(counts, nbp)
        dest = _dest(mrow, start).reshape(1, -1)
        routed.append((tab, dest, _sc_scatter_rows(dest, x1e, n_blocks * MOE_BLOCK)))
    return [_sc_gather_rows(dest, _experts(tab, xb, lw)) for tab, dest, xb in routed]


def kernel(x_prompt, x_sample, w_in, b_in, pool_w, pool_scale, conv_w, conv_b, sgu_ln_g, sgu_ln_b,
           sgu_w, sgu_b, branch_proj, w_out, ln1_g, ln1_b, w_rg, b_rg, w_re, b_re, w1, w3, w2,
           ln2_g, ln2_b):
    p = dict(w_in=w_in, b_in=b_in, pool_w=pool_w, pool_scale=pool_scale, conv_w=conv_w, conv_b=conv_b,
             sgu_ln_g=sgu_ln_g, sgu_ln_b=sgu_ln_b, sgu_w=sgu_w, sgu_b=sgu_b, branch_proj=branch_proj,
             w_out=w_out, ln1_g=ln1_g, ln1_b=ln1_b, w_rg=w_rg, b_rg=b_rg, w_re=w_re, b_re=b_re,
             w1=w1, w3=w3, w2=w2, ln2_g=ln2_g, ln2_b=ln2_b)
    shapes = (x_prompt.shape, x_sample.shape)
    seqs = [s[1] for s in shapes]
    xs = [x_prompt.reshape(-1, D_MODEL), x_sample.reshape(-1, D_MODEL)]
    for l in range(w_in.shape[0]):
        xs = _encoder_layer(xs, seqs, _layer_weights(p, l))
    return tuple(x.reshape(s) for x, s in zip(xs, shapes))
```

```python
import functools
import math

import numpy as np
import jax
import jax.numpy as jnp
from jax import lax
from jax.experimental import pallas as pl
from jax.experimental.pallas import tpu as pltpu
from jax.experimental.pallas import tpu_sc as plsc

F32 = jnp.float32
BF16 = jnp.bfloat16

D_MODEL = 1024
DEPTH = 2
BRANCH_W = 256
N_GROUPS = 4
GROUP_W = 64
POOL_WINDOWS = (2, 4, 8, 16)
CHUNK = 128
OFF_A = 0
OFF_B = 256
OFF_CH = 512
OFF_DU = 1280
OFF_G = 1792
N_EXPERT_GROUPS = 4
EXPERTS_PER_GROUP = 8
N_EXPERTS = 32
D_EXPERT = 512
MOE_BLOCK = 256
DN_ALPHA = (2 * DEPTH) ** 0.25
LN_EPS = 1e-5

N_CLASS_IDS = 256
N_REACHABLE_CLASSES = N_EXPERT_GROUPS * (EXPERTS_PER_GROUP * (EXPERTS_PER_GROUP - 1) // 2)
META_W = 128
ROW_W = D_MODEL + META_W
COUNT_SHAPE = (N_CLASS_IDS, META_W)
HALO = 16
TILE = 512
DEST_TILES = 8
DMA_SPLIT = 4
SC_WINDOW = 128
SC_COLS = 256
NEG_BIG = -1e30
VMEM_LIMIT = 56 * 1024 * 1024
EXPERT_VMEM_LIMIT = 62 * 1024 * 1024


def _dot(a, b):
    return jnp.dot(a, b, preferred_element_type=F32)


def _layer_norm(x, g, b):
    mu = jnp.mean(x, axis=-1, keepdims=True)
    xc = x - mu
    var = jnp.mean(xc * xc, axis=-1, keepdims=True)
    return xc * lax.rsqrt(var + LN_EPS) * g + b


def _gelu_tanh(x):
    return 0.5 * x * (1.0 + jnp.tanh(math.sqrt(2.0 / math.pi) * (x + 0.044715 * (x * x * x))))


def _fft_factors(seq):
    n1 = 1 << (int(math.log2(seq)) // 2)
    return n1, seq // n1


@functools.lru_cache(maxsize=None)
def _fft_tables(seq):
    n1, n2 = _fft_factors(seq)
    c = np.arange(GROUP_W)
    ang = 2.0 * np.pi * ((c[:, None] * c[None, :]) % GROUP_W) / GROUP_W
    eye = np.eye(N_GROUPS)
    cs = np.concatenate([np.kron(eye, np.cos(ang)), np.kron(eye, np.sin(ang))], axis=1)
    k1 = np.arange(n1)
    t1 = np.arange(n1)
    t2 = np.arange(n2)
    t = t1[None, None, :] * n2 + t2[:, None, None]
    ang1 = 2.0 * np.pi * ((k1[None, :, None] * t) % seq) / seq
    gc, gs = np.cos(ang1), np.sin(ang1)
    lt = np.concatenate([np.concatenate([gc, -gs], axis=2), np.concatenate([-gs, -gc], axis=2)], axis=1)
    k2 = np.arange(n2)
    ang2 = 2.0 * np.pi * ((k2[:, None] * t2[None, :]) % n2) / n2
    scale = 1.0 / math.sqrt(seq * GROUP_W)
    to16 = lambda a: jnp.asarray(a, dtype=F32).astype(BF16)
    return to16(cs), to16(lt), to16(np.cos(ang2) * scale), to16(np.sin(ang2) * scale)


def _fft_stage1_kernel(*refs, n1, tb):
    x_refs, (wb_ref, bb_ref, cs_ref, lt_ref, o_ref, u_ref) = refs[:DMA_SPLIT], refs[DMA_SPLIT:]
    x = jnp.concatenate([r[...].reshape(n1 * tb, D_MODEL // DMA_SPLIT).astype(BF16) for r in x_refs], axis=1)
    zb = _dot(x, wb_ref[...]) + bb_ref[...]
    u = _dot(zb.astype(BF16), cs_ref[...])
    for c in range(4):
        u_ref[c] = u[:, c * 128:(c + 1) * 128]
    for j in range(tb):
        q = [u_ref[c, pl.ds(j, n1, stride=tb), :] for c in range(4)]
        stacked = jnp.concatenate([jnp.concatenate(q[0:2], axis=1),
                                   jnp.concatenate(q[2:4], axis=1)], axis=0).astype(BF16)
        b = _dot(lt_ref[j], stacked)
        o_ref[:, j * 512:j * 512 + BRANCH_W] = b[:n1].astype(BF16)
        o_ref[:, j * 512 + BRANCH_W:(j + 1) * 512] = b[n1:].astype(BF16)


def _fft_stage2_kernel(b_ref, c2_ref, s2_ref, o_ref, *, kb):
    for k in range(kb):
        slab = b_ref[k]
        y = _dot(c2_ref[...], slab[:, :BRANCH_W]) + _dot(s2_ref[...], slab[:, BRANCH_W:])
        o_ref[:, k * BRANCH_W:(k + 1) * BRANCH_W] = y.astype(BF16)


def _fourier_mixer(x, w_b, b_b, seq):
    n_tok = x.shape[0]
    bsz = n_tok // seq
    n1, n2 = _fft_factors(seq)
    tb = max(8, min(16, (4 << 20) // (n1 * D_MODEL * 4)))
    kb = min(n1, (2 << 20) // (n2 * 512 * 2))
    cs, lt, c2, s2 = _fft_tables(seq)
    x4 = x.reshape(bsz, n1, n2, D_MODEL)
    xw = D_MODEL // DMA_SPLIT
    bt = pl.pallas_call(
        functools.partial(_fft_stage1_kernel, n1=n1, tb=tb),
        grid=(bsz, n2 // tb),
        in_specs=[pl.BlockSpec((None, n1, tb, xw), lambda b, j, c=c: (b, 0, j, c)) for c in range(DMA_SPLIT)] + [
            pl.BlockSpec((D_MODEL, BRANCH_W), lambda b, j: (0, 0)),
            pl.BlockSpec((1, BRANCH_W), lambda b, j: (0, 0)),
            pl.BlockSpec((BRANCH_W, 2 * BRANCH_W), lambda b, j: (0, 0)),
            pl.BlockSpec((tb, 2 * n1, 2 * n1), lambda b, j: (j, 0, 0)),
        ],
        out_specs=pl.BlockSpec((None, n1, tb * 512), lambda b, j: (b, 0, j)),
        out_shape=jax.ShapeDtypeStruct((bsz, n1, n2 * 512), BF16),
        scratch_shapes=[pltpu.VMEM((4, n1 * tb, 128), F32)],
        compiler_params=pltpu.CompilerParams(
            dimension_semantics=("arbitrary", "arbitrary"), vmem_limit_bytes=VMEM_LIMIT),
        name="fft_stage1",
    )(*([x4] * DMA_SPLIT), w_b, b_b, cs, lt)
    bt4 = bt.reshape(bsz, n1, n2, 512)
    y = pl.pallas_call(
        functools.partial(_fft_stage2_kernel, kb=kb),
        grid=(bsz, n1 // kb),
        in_specs=[
            pl.BlockSpec((None, kb, n2, 512), lambda b, i: (b, i, 0, 0)),
            pl.BlockSpec((n2, n2), lambda b, i: (0, 0)),
            pl.BlockSpec((n2, n2), lambda b, i: (0, 0)),
        ],
        out_specs=pl.BlockSpec((None, n2, kb * BRANCH_W), lambda b, i: (b, 0, i)),
        out_shape=jax.ShapeDtypeStruct((bsz, n2, n1 * BRANCH_W), BF16),
        compiler_params=pltpu.CompilerParams(
            dimension_semantics=("arbitrary", "arbitrary"), vmem_limit_bytes=VMEM_LIMIT),
        name="fft_stage2",
    )(bt4, c2, s2)
    return y.reshape(n_tok, BRANCH_W)


def _mixer_kernel(x_ref, xp_ref, xn_ref, yb_ref, cin_ref,
                  win_ref, bin_ref, poolw_ref, pools_ref, convw_ref, convb_ref,
                  lng_ref, lnb_ref, sguw_ref, sgub_ref, bp_ref, wout_ref,
                  ln1g_ref, ln1b_ref, wr_ref, br_ref, tri_ref,
                  x1e_ref, mrow_ref, cout_ref, base_ref, pre_ref, *, seq, n_tiles):
    i = pl.program_id(0)
    tile = TILE
    ext = tile + 2 * HALO
    p0 = (jnp.minimum(i, n_tiles - 1) % (seq // tile)) * tile

    @pl.when(i == 0)
    def _():
        base_ref[...] = cin_ref[...]
        pre_ref[...] = jnp.zeros_like(pre_ref)

    live = i > 0

    x = x_ref[...]
    xc = x.astype(BF16)
    xe = jnp.concatenate([xp_ref[...].astype(BF16), xc, xn_ref[...].astype(BF16)], axis=0)
    pos = p0 - HALO + lax.broadcasted_iota(jnp.int32, (ext, BRANCH_W), 0)
    valid = (pos >= 0) & (pos < seq)
    lane = lax.broadcasted_iota(jnp.int32, (ext, BRANCH_W), 1)
    grp = lane // GROUP_W

    def half_gate_logits(k):
        lo = OFF_G + k * D_MODEL
        return _dot(xc, win_ref[:, lo:lo + D_MODEL]) + bin_ref[:, lo:lo + D_MODEL]

    def gated(half_gz, y_k, k):
        return (jnp.tanh(half_gz) + 1.0) * _dot(y_k.astype(BF16), bp_ref[k])

    za = jnp.where(valid, _dot(xe, win_ref[:, OFF_A:OFF_A + BRANCH_W]) + bin_ref[:, OFF_A:OFF_A + BRANCH_W], 0.0)
    zc = _dot(xe, win_ref[:, OFF_CH:OFF_DU]) + bin_ref[:, OFF_CH:OFF_DU]
    zd = _dot(xc, win_ref[:, OFF_DU:OFF_G]) + bin_ref[:, OFF_DU:OFF_G]
    hg0 = half_gate_logits(0)

    xh, xl = _tail_norm(pre_ref[...], ln1g_ref, ln1b_ref, x1e_ref)

    s2 = za + pltpu.roll(za, 1, 0)
    s4 = s2 + pltpu.roll(s2, 2, 0)
    s8 = s4 + pltpu.roll(s4, 4, 0)
    s16 = s8 + pltpu.roll(s8, 8, 0)
    c4 = pltpu.roll(s4, ext - 1, 0)
    c8 = pltpu.roll(s8, ext - 3, 0)
    c16 = pltpu.roll(s16, ext - 7, 0)
    wsum = jnp.where(grp == 0, s2, jnp.where(grp == 1, c4, jnp.where(grp == 2, c8, c16)))
    half = jnp.where(grp == 0, 1, jnp.where(grp == 1, 2, jnp.where(grp == 2, 4, 8)))
    cnt = jnp.minimum(pos + half, seq) - jnp.maximum(pos - half, 0)
    pooled = (wsum / jnp.maximum(cnt, 1).astype(F32) - za)[HALO:HALO + tile]
    y_a = _dot(pooled.astype(BF16), poolw_ref[...]) * pools_ref[...]
    hg1 = half_gate_logits(1)

    h = zc[:, 0:BRANCH_W]
    gate_b = zc[:, BRANCH_W:2 * BRANCH_W]
    gate_c = zc[:, 2 * BRANCH_W:3 * BRANCH_W]
    q = jnp.where(valid, gate_c * h, 0.0)
    q_prev = pltpu.roll(q, 1, 0)[HALO:HALO + tile]
    q_next = pltpu.roll(q, ext - 1, 0)[HALO:HALO + tile]
    conv = (q_prev * convw_ref[0:1, :] + q[HALO:HALO + tile] * convw_ref[1:2, :]
            + q_next * convw_ref[2:3, :] + convb_ref[...])
    y_c = gate_b[HALO:HALO + tile] * conv
    merged = gated(hg0, y_a, 0)
    hg2 = half_gate_logits(2)

    w_lo, w_hi, cls = _tail_route(xh, xl, wr_ref, br_ref)

    u = _gelu_tanh(zd[:, 0:BRANCH_W])
    v = _layer_norm(_gelu_tanh(zd[:, BRANCH_W:]), lng_ref[...], lnb_ref[...]).astype(BF16)
    grp_c = lax.broadcasted_iota(jnp.int32, (CHUNK, BRANCH_W), 1) // GROUP_W
    sps = []
    for c in range(tile // CHUNK):
        vch = v[c * CHUNK:(c + 1) * CHUNK]
        sp = None
        for g in range(N_GROUPS):
            r = _dot(sguw_ref[g], vch)
            sp = r if sp is None else jnp.where(grp_c == g, r, sp)
        sps.append(sp + sgub_ref[...])
    y_d = u * jnp.concatenate(sps, axis=0)
    merged = merged + gated(hg1, yb_ref[...], 1)
    hg3 = half_gate_logits(3)

    _tail_rank(w_lo, w_hi, cls, live, tri_ref, x1e_ref, mrow_ref, cout_ref, base_ref)

    merged = merged + gated(hg2, y_c, 2)
    merged = merged + gated(hg3, y_d, 3)
    pre_ref[...] = DN_ALPHA * x + _dot(merged.astype(BF16), wout_ref[...])


def _tail_norm(pre, ln1g_ref, ln1b_ref, x1e_ref):
    x1 = _layer_norm(pre, ln1g_ref[...], ln1b_ref[...])
    x1e_ref[:, 0:D_MODEL] = x1
    xh = x1.astype(BF16)
    return xh, (x1 - xh.astype(F32)).astype(BF16)


def _tail_route(xh, xl, wr_ref, br_ref):
    tile = TILE
    wr = wr_ref[...]
    wh = wr.astype(BF16)
    wl = (wr - wh.astype(F32)).astype(BF16)
    nt = (((1,), (1,)), ((), ()))
    dot_nt = lambda a, b: lax.dot_general(a, b, nt, preferred_element_type=F32)
    logits = dot_nt(wh, xh) + dot_nt(wh, xl) + dot_nt(wl, xh) + br_ref[...]
    row8 = lax.broadcasted_iota(jnp.int32, (EXPERTS_PER_GROUP, tile), 0)
    lg = logits[0:8]
    m = jnp.max(lg, axis=0, keepdims=True)
    g_idx = jnp.min(jnp.where(lg == m, row8, 8), axis=0, keepdims=True)
    p_group = 1.0 / jnp.sum(jnp.exp(lg - m), axis=0, keepdims=True)
    le = logits[8:16]
    for g in range(1, N_EXPERT_GROUPS):
        le = jnp.where(g_idx == g, logits[8 + 8 * g:16 + 8 * g], le)
    ex = jnp.exp(le - jnp.max(le, axis=0, keepdims=True))
    pe = ex / jnp.sum(ex, axis=0, keepdims=True)
    p1 = jnp.max(pe, axis=0, keepdims=True)
    i1 = jnp.min(jnp.where(pe == p1, row8, 8), axis=0, keepdims=True)
    pe2 = jnp.where(row8 == i1, -1.0, pe)
    p2 = jnp.max(pe2, axis=0, keepdims=True)
    i2 = jnp.min(jnp.where(pe2 == p2, row8, 8), axis=0, keepdims=True)
    first_lo = i1 < i2
    w_lo = p_group * jnp.where(first_lo, p1, p2)
    w_hi = p_group * jnp.where(first_lo, p2, p1)
    cls = g_idx * 64 + jnp.minimum(i1, i2) * EXPERTS_PER_GROUP + jnp.maximum(i1, i2)
    return w_lo, w_hi, cls


def _tail_rank(w_lo, w_hi, cls, live, tri_ref, x1e_ref, mrow_ref, cout_ref, base_ref):
    tile = TILE
    onehot = jnp.logical_and(lax.broadcasted_iota(jnp.int32, (N_CLASS_IDS, tile), 0) == cls, live)
    ohb = onehot.astype(BF16)
    before = _dot(ohb, tri_ref[...])
    base = base_ref[...]
    base_t = jnp.concatenate([base] * (tile // META_W), axis=1)
    rank = jnp.sum(jnp.where(onehot, before + base_t, 0.0), axis=0, keepdims=True)
    new_base = base + _dot(ohb, jnp.ones((tile, META_W), BF16))
    base_ref[...] = new_base
    cout_ref[...] = new_base

    meta_t = jnp.concatenate([w_lo, w_hi, cls.astype(F32), rank, jnp.zeros((META_W - 4, tile), F32)], axis=0)
    x1e_ref[:, D_MODEL:ROW_W] = meta_t.T
    mrow_ref[...] = meta_t[0:8]


def _const_spec(shape):
    nd = len(shape)
    return pl.BlockSpec(shape, lambda i, _nd=nd: (0,) * _nd, pipeline_mode=pl.Buffered(1))


def _mixer(x, y_b, counts_in, lw, seq):
    n_tok = x.shape[0]
    n_tiles = n_tok // TILE
    hb = TILE // HALO
    n_hblk = n_tok // HALO
    consts = [lw["w_in"], lw["b_in"], lw["pool_w"], lw["pool_scale"], lw["conv_w"], lw["conv_b"],
              lw["sgu_ln_g"], lw["sgu_ln_b"], lw["sgu_w"], lw["sgu_b"], lw["branch_proj"], lw["w_out"],
              lw["ln1_g"], lw["ln1_b"], lw["w_r"], lw["b_r"], lw["tri"]]
    cur = lambda i: jnp.minimum(i, n_tiles - 1)
    return pl.pallas_call(
        functools.partial(_mixer_kernel, seq=seq, n_tiles=n_tiles),
        grid=(n_tiles + 1,),
        in_specs=[
            pl.BlockSpec((TILE, D_MODEL), lambda i: (cur(i), 0)),
            pl.BlockSpec((HALO, D_MODEL), lambda i: (jnp.maximum(cur(i) * hb - 1, 0), 0)),
            pl.BlockSpec((HALO, D_MODEL), lambda i: (jnp.minimum((cur(i) + 1) * hb, n_hblk - 1), 0)),
            pl.BlockSpec((TILE, BRANCH_W), lambda i: (cur(i), 0)),
            pl.BlockSpec(COUNT_SHAPE, lambda i: (0, 0)),
        ] + [_const_spec(c.shape) for c in consts],
        out_specs=[
            pl.BlockSpec((TILE, ROW_W), lambda i: (jnp.maximum(i - 1, 0), 0)),
            pl.BlockSpec((None, 8, TILE), lambda i: (jnp.maximum(i - 1, 0), 0, 0)),
            pl.BlockSpec(COUNT_SHAPE, lambda i: (0, 0)),
        ],
        out_shape=[
            jax.ShapeDtypeStruct((n_tok, ROW_W), F32),
            jax.ShapeDtypeStruct((n_tiles, 8, TILE), F32),
            jax.ShapeDtypeStruct(COUNT_SHAPE, F32),
        ],
        scratch_shapes=[pltpu.VMEM(COUNT_SHAPE, F32), pltpu.VMEM((TILE, D_MODEL), F32)],
        compiler_params=pltpu.CompilerParams(
            dimension_semantics=("arbitrary",), vmem_limit_bytes=VMEM_LIMIT),
        name="mixer",
    )(x, x, x, y_b, counts_in, *consts)


def _cumsum_sublanes(a):
    n = a.shape[0]
    row = lax.broadcasted_iota(jnp.int32, a.shape, 0)
    s = 1
    while s < n:
        a = a + jnp.where(row >= s, pltpu.roll(a, s, 0), 0.0)
        s *= 2
    return a


def _tables_kernel(cnt_ref, tab_ref, start_ref, *, nbp):
    cnt = cnt_ref[...]
    nb = jnp.floor((cnt + (MOE_BLOCK - 1)) * (1.0 / MOE_BLOCK))
    end = _cumsum_sublanes(nb)
    start_ref[...] = (end - nb) * MOE_BLOCK
    end_t = jnp.concatenate([end] * (nbp // META_W), axis=1)
    j = lax.broadcasted_iota(jnp.int32, (N_CLASS_IDS, nbp), 1).astype(F32)
    blk_cls = jnp.sum((end_t <= j).astype(F32), axis=0, keepdims=True)
    total = end_t[N_CLASS_IDS - 1:N_CLASS_IDS, :]
    jr = j[0:1, :]
    active = jr < total
    last_cls = jnp.max(jnp.where(active, blk_cls, 0.0), axis=-1, keepdims=True)
    cls_i = jnp.where(active, blk_cls, last_cls).astype(jnp.int32)
    grp = cls_i >> 6
    e_lo = (cls_i >> 3) & 7
    e_hi = cls_i & 7
    bidx = jnp.minimum(jr, total - 1.0).astype(jnp.int32)
    match = lax.broadcasted_iota(jnp.int32, (N_CLASS_IDS, nbp), 0) == cls_i
    reps = nbp // META_W
    cnt_j = jnp.sum(jnp.where(match, jnp.concatenate([cnt] * reps, axis=1), 0.0), axis=0, keepdims=True)
    first_j = jnp.sum(jnp.where(match, jnp.concatenate([end - nb] * reps, axis=1), 0.0), axis=0, keepdims=True)
    n_valid = jnp.where(active, jnp.clip(cnt_j - (jr - first_j) * MOE_BLOCK, 0.0, MOE_BLOCK), 0.0).astype(jnp.int32)
    zero = jnp.zeros_like(bidx)
    tab_ref[...] = jnp.concatenate([e_lo, e_hi, bidx, total.astype(jnp.int32), n_valid, grp, zero, zero], axis=0)


def _tables(counts, nbp):
    return pl.pallas_call(
        functools.partial(_tables_kernel, nbp=nbp),
        out_shape=[jax.ShapeDtypeStruct((8, nbp), jnp.int32),
                   jax.ShapeDtypeStruct(COUNT_SHAPE, F32)],
        name="moe_tables",
    )(counts)


def _dest_kernel(mrow_ref, start_ref, dest_ref, *, tiles):
    start = jnp.concatenate([start_ref[...]] * (TILE // META_W), axis=1)
    cls_ids = lax.broadcasted_iota(jnp.int32, (N_CLASS_IDS, TILE), 0)
    for t in range(tiles):
        cls = mrow_ref[t, 2:3, :].astype(jnp.int32)
        first = jnp.sum(jnp.where(cls_ids == cls, start, 0.0), axis=0, keepdims=True)
        dest_ref[t] = (first + mrow_ref[t, 3:4, :]).astype(jnp.int32)


def _dest(mrow, start):
    n_tiles = mrow.shape[0]
    tiles = math.gcd(n_tiles, DEST_TILES)
    return pl.pallas_call(
        functools.partial(_dest_kernel, tiles=tiles),
        grid=(n_tiles // tiles,),
        in_specs=[pl.BlockSpec((tiles, 8, TILE), lambda i: (i, 0, 0)),
                  pl.BlockSpec(COUNT_SHAPE, lambda i: (0, 0))],
        out_specs=pl.BlockSpec((tiles, 1, TILE), lambda i: (i, 0, 0)),
        out_shape=jax.ShapeDtypeStruct((n_tiles, 1, TILE), jnp.int32),
        compiler_params=pltpu.CompilerParams(dimension_semantics=("arbitrary",)),
        name="moe_dest",
    )(mrow, start)


def _sc_mesh():
    return plsc.VectorSubcoreMesh(core_axis_name="core", subcore_axis_name="subcore")


def _col_chunks(width):
    chunks, c = [], 0
    while c < width:
        w = min(SC_COLS, width - c)
        assert c % w == 0
        chunks.append((c // w, w))
        c += w
    return chunks


_SC_PARAMS = dict(core_axis_name=("core", "subcore"), dimension_semantics=(pltpu.PARALLEL,))


def _sc_scatter_rows(idx, src, n_out):
    n, d = src.shape

    @pl.kernel(out_type=jax.ShapeDtypeStruct((n_out, d), src.dtype), mesh=_sc_mesh(), scratch_types=[],
               compiler_params=pltpu.CompilerParams(use_tc_tiling_on_sc=True))
    def scatter(src_hbm, idx_hbm, out_hbm):
        for cb, cw in _col_chunks(d):
            def body(rows_vmem, idx_vmem, cb=cb, cw=cw):
                pltpu.sync_copy(rows_vmem, out_hbm.at[:, pl.ds(cb * cw, cw)].at[idx_vmem.at[0]])

            pltpu.emit_pipeline(
                body, grid=(n // SC_WINDOW,),
                in_specs=[pl.BlockSpec((SC_WINDOW, cw), lambda i, cb=cb: (i, cb)),
                          pl.BlockSpec((1, SC_WINDOW), lambda i: (0, i))],
                out_specs=[], **_SC_PARAMS)(src_hbm, idx_hbm)

    return scatter(src, idx)


def _sc_gather_rows(idx, src):
    n, d = idx.shape[1], src.shape[1]

    @pl.kernel(out_type=jax.ShapeDtypeStruct((n, d), src.dtype), mesh=_sc_mesh(), scratch_types=[],
               compiler_params=pltpu.CompilerParams(use_tc_tiling_on_sc=True))
    def gather(src_hbm, idx_hbm, out_hbm):
        for cb, cw in _col_chunks(d):
            def body(idx_vmem, rows_vmem, cb=cb, cw=cw):
                pltpu.sync_copy(src_hbm.at[:, pl.ds(cb * cw, cw)].at[idx_vmem.at[0]], rows_vmem)

            pltpu.emit_pipeline(
                body, grid=(n // SC_WINDOW,),
                in_specs=[pl.BlockSpec((1, SC_WINDOW), lambda i: (0, i))],
                out_specs=[pl.BlockSpec((SC_WINDOW, cw), lambda i, cb=cb: (i, cb))],
                **_SC_PARAMS)(idx_hbm, out_hbm)

    return gather(src, idx)


def _expert_kernel(grp_ref, elo_ref, ehi_ref, bidx_ref, nvalid_ref, tot_ref,
                   xb_ref, w1_hbm, w3_hbm, w2_hbm, g_ref, b_ref, o_ref,
                   pre_ref, w1_buf, w3_buf, w2_buf, w_sem, state_ref):
    del bidx_ref
    j = pl.program_id(0)
    total = tot_ref[0]

    def weight_copies(grp, slot):
        experts = pl.ds(grp * EXPERTS_PER_GROUP, EXPERTS_PER_GROUP)
        return [pltpu.make_async_copy(hbm.at[experts], buf.at[slot], w_sem.at[slot])
                for hbm, buf in ((w1_hbm, w1_buf), (w3_hbm, w3_buf), (w2_hbm, w2_buf))]

    def start_weights(grp, slot):
        for c in weight_copies(grp, slot):
            c.start()

    def wait_weights(slot):
        for c in weight_copies(0, slot):
            c.wait()

    @pl.when(j == 0)
    def _():
        pre_ref[...] = jnp.zeros_like(pre_ref)
        state_ref[0] = 1
        state_ref[1] = -1

    def finish_previous():
        o_ref[...] = _layer_norm(pre_ref[...], g_ref[...], b_ref[...])

    def hidden(half_a, b):
        return ((half_a * (jnp.tanh(half_a) + 1.0)) * b).astype(BF16)

    @pl.when(j < total)
    def _():
        grp = grp_ref[j]

        @pl.when(jnp.logical_or(j == 0, grp != grp_ref[jnp.maximum(j - 1, 0)]))
        def _():
            slot = 1 - state_ref[0]
            pending = state_ref[1]

            @pl.when(pending != grp)
            def _():
                @pl.when(pending >= 0)
                def _():
                    wait_weights(slot)
                start_weights(grp, slot)

            wait_weights(slot)
            state_ref[0] = slot
            following = grp + 1

            @pl.when(following < N_EXPERT_GROUPS)
            def _():
                start_weights(following, 1 - slot)

            state_ref[1] = jnp.where(following < N_EXPERT_GROUPS, following, -1)

        slot = state_ref[0]
        n_valid = nvalid_ref[j]
        x = jnp.where(lax.broadcasted_iota(jnp.int32, (MOE_BLOCK, D_MODEL), 0) < n_valid, xb_ref[:, 0:D_MODEL], 0.0)
        xh = x.astype(BF16)
        e_lo = elo_ref[j]
        e_hi = ehi_ref[j]
        half_a_lo = _dot(xh, w1_buf[slot, e_lo])
        b_lo = _dot(xh, w3_buf[slot, e_lo])
        finish_previous()
        half_a_hi = _dot(xh, w1_buf[slot, e_hi])
        b_hi = _dot(xh, w3_buf[slot, e_hi])
        y_lo = _dot(hidden(half_a_lo, b_lo), w2_buf[slot, e_lo])
        y_hi = _dot(hidden(half_a_hi, b_hi), w2_buf[slot, e_hi])
        live = lax.broadcasted_iota(jnp.int32, (MOE_BLOCK, 1), 0) < n_valid
        w_lo = jnp.where(live, xb_ref[:, D_MODEL:D_MODEL + 1], 0.0)
        w_hi = jnp.where(live, xb_ref[:, D_MODEL + 1:D_MODEL + 2], 0.0)
        pre_ref[...] = DN_ALPHA * x + (w_lo * y_lo + w_hi * y_hi)

    @pl.when(j == total)
    def _():
        finish_previous()

        @pl.when(state_ref[1] >= 0)
        def _():
            wait_weights(1 - state_ref[0])
            state_ref[1] = -1


def _experts(tab, xb, lw):
    n_blocks = xb.shape[0] // MOE_BLOCK
    vec = pl.BlockSpec((1, D_MODEL), lambda j, g, lo, hi, bi, nv, tot: (0, 0))
    hbm = pl.BlockSpec(memory_space=pl.ANY)
    group_up = (2, EXPERTS_PER_GROUP, D_MODEL, D_EXPERT)
    group_down = (2, EXPERTS_PER_GROUP, D_EXPERT, D_MODEL)
    grid_spec = pltpu.PrefetchScalarGridSpec(
        num_scalar_prefetch=6,
        grid=(n_blocks + 1,),
        in_specs=[pl.BlockSpec((MOE_BLOCK, ROW_W), lambda j, g, lo, hi, bi, nv, tot: (bi[j], 0)),
                  hbm, hbm, hbm, vec, vec],
        out_specs=pl.BlockSpec((MOE_BLOCK, D_MODEL),
                               lambda j, g, lo, hi, bi, nv, tot: (jnp.maximum(jnp.minimum(j, tot[0]) - 1, 0), 0)),
        scratch_shapes=[pltpu.VMEM((MOE_BLOCK, D_MODEL), F32),
                        pltpu.VMEM(group_up, BF16), pltpu.VMEM(group_up, BF16), pltpu.VMEM(group_down, BF16),
                        pltpu.SemaphoreType.DMA((2,)), pltpu.SMEM((2,), jnp.int32)],
    )
    return pl.pallas_call(
        _expert_kernel,
        grid_spec=grid_spec,
        out_shape=jax.ShapeDtypeStruct((n_blocks * MOE_BLOCK, D_MODEL), F32),
        compiler_params=pltpu.CompilerParams(
            dimension_semantics=("arbitrary",), vmem_limit_bytes=EXPERT_VMEM_LIMIT),
        name="moe_experts",
    )(tab[5], tab[0], tab[1], tab[2], tab[4], tab[3, 0:1], xb,
      lw["w1"], lw["w3"], lw["w2"], lw["ln2_g"], lw["ln2_b"])


def _layer_weights(p, l):
    row = lambda a: a[l].reshape(1, -1).astype(F32)
    w_r = jnp.zeros((META_W, D_MODEL), F32)
    w_r = w_r.at[0:N_EXPERT_GROUPS].set(p["w_rg"][l].T).at[8:8 + N_EXPERTS].set(p["w_re"][l].T)
    b_r = jnp.zeros((META_W, 1), F32)
    b_r = b_r.at[0:N_EXPERT_GROUPS, 0].set(p["b_rg"][l]).at[N_EXPERT_GROUPS:8, 0].set(NEG_BIG)
    b_r = b_r.at[8:8 + N_EXPERTS, 0].set(p["b_re"][l])
    t = np.arange(TILE)
    gate_half = jnp.asarray(np.where(np.arange(p["w_in"].shape[-1]) >= OFF_G, 0.5, 1.0), F32)
    return {
        "w_in": (p["w_in"][l] * gate_half).astype(BF16),
        "b_in": row(p["b_in"]) * gate_half,
        "w_b": p["w_in"][l][:, OFF_B:OFF_B + BRANCH_W].astype(BF16),
        "b_b": p["b_in"][l][OFF_B:OFF_B + BRANCH_W].reshape(1, -1),
        "pool_w": jax.scipy.linalg.block_diag(*[p["pool_w"][l][g] for g in range(N_GROUPS)]).astype(BF16),
        "pool_scale": row(p["pool_scale"]),
        "conv_w": p["conv_w"][l],
        "conv_b": row(p["conv_b"]),
        "sgu_ln_g": row(p["sgu_ln_g"]),
        "sgu_ln_b": row(p["sgu_ln_b"]),
        "sgu_w": p["sgu_w"][l].astype(BF16),
        "sgu_b": jnp.repeat(p["sgu_b"][l].T, GROUP_W, axis=1),
        "branch_proj": (0.5 * p["branch_proj"][l]).astype(BF16),
        "w_out": p["w_out"][l].astype(BF16),
        "ln1_g": row(p["ln1_g"]),
        "ln1_b": row(p["ln1_b"]),
        "w_r": w_r,
        "b_r": b_r,
        "tri": jnp.asarray((t[:, None] < t[None, :]).astype(np.float32)).astype(BF16),
        "w1": (0.5 * p["w1"][l]).astype(BF16),
        "w3": p["w3"][l].astype(BF16),
        "w2": p["w2"][l].astype(BF16),
        "ln2_g": row(p["ln2_g"]),
        "ln2_b": row(p["ln2_b"]),
    }


def _encoder_layer(xs, seqs, lw):
    zero_counts = jnp.zeros(COUNT_SHAPE, F32)
    routed = []
    for x, seq in zip(xs, seqs):
        n_blocks = x.shape[0] // MOE_BLOCK + N_REACHABLE_CLASSES
        nbp = -(-(n_blocks + 1) // META_W) * META_W
        y_b = _fourier_mixer(x, lw["w_b"], lw["b_b"], seq)
        x1e, mrow, counts = _mixer(x, y_b, zero_counts, lw, seq)
        tab, start = _tables(counts, nbp)
        dest = _dest(mrow, start).reshape(1, -1)
        routed.append((tab, dest, _sc_scatter_rows(dest, x1e, n_blocks * MOE_BLOCK)))
    return [_sc_gather_rows(dest, _experts(tab, xb, lw)) for tab, dest, xb in routed]


def kernel(x_prompt, x_sample, w_in, b_in, pool_w, pool_scale, conv_w, conv_b, sgu_ln_g, sgu_ln_b,
           sgu_w, sgu_b, branch_proj, w_out, ln1_g, ln1_b, w_rg, b_rg, w_re, b_re, w1, w3, w2,
           ln2_g, ln2_b):
    p = dict(w_in=w_in, b_in=b_in, pool_w=pool_w, pool_scale=pool_scale, conv_w=conv_w, conv_b=conv_b,
             sgu_ln_g=sgu_ln_g, sgu_ln_b=sgu_ln_b, sgu_w=sgu_w, sgu_b=sgu_b, branch_proj=branch_proj,
             w_out=w_out, ln1_g=ln1_g, ln1_b=ln1_b, w_rg=w_rg, b_rg=b_rg, w_re=w_re, b_re=b_re,
             w1=w1, w3=w3, w2=w2, ln2_g=ln2_g, ln2_b=ln2_b)
    shapes = (x_prompt.shape, x_sample.shape)
    seqs = [s[1] for s in shapes]
    xs = [x_prompt.reshape(-1, D_MODEL), x_sample.reshape(-1, D_MODEL)]
    for l in range(w_in.shape[0]):
        xs = _encoder_layer(xs, seqs, _layer_weights(p, l))
    return tuple(x.reshape(s) for x, s in zip(xs, shapes))
```

```python
import functools
import math

import numpy as np
import jax
import jax.numpy as jnp
from jax import lax
from jax.experimental import pallas as pl
from jax.experimental.pallas import tpu as pltpu
from jax.experimental.pallas import tpu_sc as plsc

F32 = jnp.float32
BF16 = jnp.bfloat16

D_MODEL = 1024
DEPTH = 2
BRANCH_W = 256
N_GROUPS = 4
GROUP_W = 64
POOL_WINDOWS = (2, 4, 8, 16)
CHUNK = 128
OFF_A = 0
OFF_B = 256
OFF_CH = 512
OFF_DU = 1280
OFF_G = 1792
N_EXPERT_GROUPS = 4
EXPERTS_PER_GROUP = 8
N_EXPERTS = 32
D_EXPERT = 512
MOE_BLOCK = 256
DN_ALPHA = (2 * DEPTH) ** 0.25
LN_EPS = 1e-5

N_CLASS_IDS = 256
N_REACHABLE_CLASSES = N_EXPERT_GROUPS * (EXPERTS_PER_GROUP * (EXPERTS_PER_GROUP - 1) // 2)
META_W = 128
ROW_W = D_MODEL + META_W
COUNT_SHAPE = (N_CLASS_IDS, META_W)
HALO = 16
TILE = 512
DEST_TILES = 8
DMA_SPLIT = 4
SC_WINDOW = 128
SC_COLS = 256
NEG_BIG = -1e30
VMEM_LIMIT = 56 * 1024 * 1024


def _dot(a, b):
    return jnp.dot(a, b, preferred_element_type=F32)


def _layer_norm(x, g, b):
    mu = jnp.mean(x, axis=-1, keepdims=True)
    xc = x - mu
    var = jnp.mean(xc * xc, axis=-1, keepdims=True)
    return xc * lax.rsqrt(var + LN_EPS) * g + b


def _gelu_tanh(x):
    return 0.5 * x * (1.0 + jnp.tanh(math.sqrt(2.0 / math.pi) * (x + 0.044715 * (x * x * x))))


def _fft_factors(seq):
    n1 = 1 << (int(math.log2(seq)) // 2)
    return n1, seq // n1


@functools.lru_cache(maxsize=None)
def _fft_tables(seq):
    n1, n2 = _fft_factors(seq)
    c = np.arange(GROUP_W)
    ang = 2.0 * np.pi * ((c[:, None] * c[None, :]) % GROUP_W) / GROUP_W
    eye = np.eye(N_GROUPS)
    cs = np.concatenate([np.kron(eye, np.cos(ang)), np.kron(eye, np.sin(ang))], axis=1)
    k1 = np.arange(n1)
    t1 = np.arange(n1)
    t2 = np.arange(n2)
    t = t1[None, None, :] * n2 + t2[:, None, None]
    ang1 = 2.0 * np.pi * ((k1[None, :, None] * t) % seq) / seq
    gc, gs = np.cos(ang1), np.sin(ang1)
    lt = np.concatenate([np.concatenate([gc, -gs], axis=2), np.concatenate([-gs, -gc], axis=2)], axis=1)
    k2 = np.arange(n2)
    ang2 = 2.0 * np.pi * ((k2[:, None] * t2[None, :]) % n2) / n2
    scale = 1.0 / math.sqrt(seq * GROUP_W)
    to16 = lambda a: jnp.asarray(a, dtype=F32).astype(BF16)
    return to16(cs), to16(lt), to16(np.cos(ang2) * scale), to16(np.sin(ang2) * scale)


def _fft_stage1_kernel(*refs, n1, tb):
    x_refs, (wb_ref, bb_ref, cs_ref, lt_ref, o_ref, u_ref) = refs[:DMA_SPLIT], refs[DMA_SPLIT:]
    x = jnp.concatenate([r[...].reshape(n1 * tb, D_MODEL // DMA_SPLIT).astype(BF16) for r in x_refs], axis=1)
    zb = _dot(x, wb_ref[...]) + bb_ref[...]
    u = _dot(zb.astype(BF16), cs_ref[...])
    for c in range(4):
        u_ref[c] = u[:, c * 128:(c + 1) * 128]
    for j in range(tb):
        q = [u_ref[c, pl.ds(j, n1, stride=tb), :] for c in range(4)]
        stacked = jnp.concatenate([jnp.concatenate(q[0:2], axis=1),
                                   jnp.concatenate(q[2:4], axis=1)], axis=0).astype(BF16)
        b = _dot(lt_ref[j], stacked)
        o_ref[:, j * 512:j * 512 + BRANCH_W] = b[:n1].astype(BF16)
        o_ref[:, j * 512 + BRANCH_W:(j + 1) * 512] = b[n1:].astype(BF16)


def _fft_stage2_kernel(b_ref, c2_ref, s2_ref, o_ref, *, kb):
    for k in range(kb):
        slab = b_ref[k]
        y = _dot(c2_ref[...], slab[:, :BRANCH_W]) + _dot(s2_ref[...], slab[:, BRANCH_W:])
        o_ref[:, k * BRANCH_W:(k + 1) * BRANCH_W] = y.astype(BF16)


def _fourier_mixer(x, w_b, b_b, seq):
    n_tok = x.shape[0]
    bsz = n_tok // seq
    n1, n2 = _fft_factors(seq)
    tb = max(8, min(16, (4 << 20) // (n1 * D_MODEL * 4)))
    kb = min(n1, (2 << 20) // (n2 * 512 * 2))
    cs, lt, c2, s2 = _fft_tables(seq)
    x4 = x.reshape(bsz, n1, n2, D_MODEL)
    xw = D_MODEL // DMA_SPLIT
    bt = pl.pallas_call(
        functools.partial(_fft_stage1_kernel, n1=n1, tb=tb),
        grid=(bsz, n2 // tb),
        in_specs=[pl.BlockSpec((None, n1, tb, xw), lambda b, j, c=c: (b, 0, j, c)) for c in range(DMA_SPLIT)] + [
            pl.BlockSpec((D_MODEL, BRANCH_W), lambda b, j: (0, 0)),
            pl.BlockSpec((1, BRANCH_W), lambda b, j: (0, 0)),
            pl.BlockSpec((BRANCH_W, 2 * BRANCH_W), lambda b, j: (0, 0)),
            pl.BlockSpec((tb, 2 * n1, 2 * n1), lambda b, j: (j, 0, 0)),
        ],
        out_specs=pl.BlockSpec((None, n1, tb * 512), lambda b, j: (b, 0, j)),
        out_shape=jax.ShapeDtypeStruct((bsz, n1, n2 * 512), BF16),
        scratch_shapes=[pltpu.VMEM((4, n1 * tb, 128), F32)],
        compiler_params=pltpu.CompilerParams(
            dimension_semantics=("arbitrary", "arbitrary"), vmem_limit_bytes=VMEM_LIMIT),
        name="fft_stage1",
    )(*([x4] * DMA_SPLIT), w_b, b_b, cs, lt)
    bt4 = bt.reshape(bsz, n1, n2, 512)
    y = pl.pallas_call(
        functools.partial(_fft_stage2_kernel, kb=kb),
        grid=(bsz, n1 // kb),
        in_specs=[
            pl.BlockSpec((None, kb, n2, 512), lambda b, i: (b, i, 0, 0)),
            pl.BlockSpec((n2, n2), lambda b, i: (0, 0)),
            pl.BlockSpec((n2, n2), lambda b, i: (0, 0)),
        ],
        out_specs=pl.BlockSpec((None, n2, kb * BRANCH_W), lambda b, i: (b, 0, i)),
        out_shape=jax.ShapeDtypeStruct((bsz, n2, n1 * BRANCH_W), BF16),
        compiler_params=pltpu.CompilerParams(
            dimension_semantics=("arbitrary", "arbitrary"), vmem_limit_bytes=VMEM_LIMIT),
        name="fft_stage2",
    )(bt4, c2, s2)
    return y.reshape(n_tok, BRANCH_W)


def _mixer_kernel(x_ref, xp_ref, xn_ref, yb_ref, cin_ref,
                  win_ref, bin_ref, poolw_ref, pools_ref, convw_ref, convb_ref,
                  lng_ref, lnb_ref, sguw_ref, sgub_ref, bp_ref, wout_ref,
                  ln1g_ref, ln1b_ref, wr_ref, br_ref, tri_ref,
                  x1e_ref, mrow_ref, cout_ref, base_ref, pre_ref, *, seq, n_tiles):
    i = pl.program_id(0)
    tile = TILE
    ext = tile + 2 * HALO
    p0 = (jnp.minimum(i, n_tiles - 1) % (seq // tile)) * tile

    @pl.when(i == 0)
    def _():
        base_ref[...] = cin_ref[...]
        pre_ref[...] = jnp.zeros_like(pre_ref)

    live = i > 0

    x = x_ref[...]
    xc = x.astype(BF16)
    xe = jnp.concatenate([xp_ref[...].astype(BF16), xc, xn_ref[...].astype(BF16)], axis=0)
    pos = p0 - HALO + lax.broadcasted_iota(jnp.int32, (ext, BRANCH_W), 0)
    valid = (pos >= 0) & (pos < seq)
    lane = lax.broadcasted_iota(jnp.int32, (ext, BRANCH_W), 1)
    grp = lane // GROUP_W

    def half_gate_logits(k):
        lo = OFF_G + k * D_MODEL
        return _dot(xc, win_ref[:, lo:lo + D_MODEL]) + bin_ref[:, lo:lo + D_MODEL]

    def gated(half_gz, y_k, k):
        return (jnp.tanh(half_gz) + 1.0) * _dot(y_k.astype(BF16), bp_ref[k])

    za = jnp.where(valid, _dot(xe, win_ref[:, OFF_A:OFF_A + BRANCH_W]) + bin_ref[:, OFF_A:OFF_A + BRANCH_W], 0.0)
    zc = _dot(xe, win_ref[:, OFF_CH:OFF_DU]) + bin_ref[:, OFF_CH:OFF_DU]
    zd = _dot(xc, win_ref[:, OFF_DU:OFF_G]) + bin_ref[:, OFF_DU:OFF_G]
    hg0 = half_gate_logits(0)

    xh, xl = _tail_norm(pre_ref[...], ln1g_ref, ln1b_ref, x1e_ref)

    s2 = za + pltpu.roll(za, 1, 0)
    s4 = s2 + pltpu.roll(s2, 2, 0)
    s8 = s4 + pltpu.roll(s4, 4, 0)
    s16 = s8 + pltpu.roll(s8, 8, 0)
    c4 = pltpu.roll(s4, ext - 1, 0)
    c8 = pltpu.roll(s8, ext - 3, 0)
    c16 = pltpu.roll(s16, ext - 7, 0)
    wsum = jnp.where(grp == 0, s2, jnp.where(grp == 1, c4, jnp.where(grp == 2, c8, c16)))
    half = jnp.where(grp == 0, 1, jnp.where(grp == 1, 2, jnp.where(grp == 2, 4, 8)))
    cnt = jnp.minimum(pos + half, seq) - jnp.maximum(pos - half, 0)
    pooled = (wsum / jnp.maximum(cnt, 1).astype(F32) - za)[HALO:HALO + tile]
    y_a = _dot(pooled.astype(BF16), poolw_ref[...]) * pools_ref[...]
    hg1 = half_gate_logits(1)

    h = zc[:, 0:BRANCH_W]
    gate_b = zc[:, BRANCH_W:2 * BRANCH_W]
    gate_c = zc[:, 2 * BRANCH_W:3 * BRANCH_W]
    q = jnp.where(valid, gate_c * h, 0.0)
    q_prev = pltpu.roll(q, 1, 0)[HALO:HALO + tile]
    q_next = pltpu.roll(q, ext - 1, 0)[HALO:HALO + tile]
    conv = (q_prev * convw_ref[0:1, :] + q[HALO:HALO + tile] * convw_ref[1:2, :]
            + q_next * convw_ref[2:3, :] + convb_ref[...])
    y_c = gate_b[HALO:HALO + tile] * conv
    merged = gated(hg0, y_a, 0)
    hg2 = half_gate_logits(2)

    w_lo, w_hi, cls = _tail_route(xh, xl, wr_ref, br_ref)

    u = _gelu_tanh(zd[:, 0:BRANCH_W])
    v = _layer_norm(_gelu_tanh(zd[:, BRANCH_W:]), lng_ref[...], lnb_ref[...]).astype(BF16)
    grp_c = lax.broadcasted_iota(jnp.int32, (CHUNK, BRANCH_W), 1) // GROUP_W
    sps = []
    for c in range(tile // CHUNK):
        vch = v[c * CHUNK:(c + 1) * CHUNK]
        sp = None
        for g in range(N_GROUPS):
            r = _dot(sguw_ref[g], vch)
            sp = r if sp is None else jnp.where(grp_c == g, r, sp)
        sps.append(sp + sgub_ref[...])
    y_d = u * jnp.concatenate(sps, axis=0)
    merged = merged + gated(hg1, yb_ref[...], 1)
    hg3 = half_gate_logits(3)

    _tail_rank(w_lo, w_hi, cls, live, tri_ref, x1e_ref, mrow_ref, cout_ref, base_ref)

    merged = merged + gated(hg2, y_c, 2)
    merged = merged + gated(hg3, y_d, 3)
    pre_ref[...] = DN_ALPHA * x + _dot(merged.astype(BF16), wout_ref[...])


def _tail_norm(pre, ln1g_ref, ln1b_ref, x1e_ref):
    x1 = _layer_norm(pre, ln1g_ref[...], ln1b_ref[...])
    x1e_ref[:, 0:D_MODEL] = x1
    xh = x1.astype(BF16)
    return xh, (x1 - xh.astype(F32)).astype(BF16)


def _tail_route(xh, xl, wr_ref, br_ref):
    tile = TILE
    wr = wr_ref[...]
    wh = wr.astype(BF16)
    wl = (wr - wh.astype(F32)).astype(BF16)
    nt = (((1,), (1,)), ((), ()))
    dot_nt = lambda a, b: lax.dot_general(a, b, nt, preferred_element_type=F32)
    both = dot_nt(jnp.concatenate([wh, wl], axis=0), xh)
    logits = both[0:META_W] + both[META_W:] + dot_nt(wh, xl) + br_ref[...]
    row8 = lax.broadcasted_iota(jnp.int32, (EXPERTS_PER_GROUP, tile), 0)
    lg = logits[0:8]
    m = jnp.max(lg, axis=0, keepdims=True)
    g_idx = jnp.min(jnp.where(lg == m, row8, 8), axis=0, keepdims=True)
    p_group = 1.0 / jnp.sum(jnp.exp(lg - m), axis=0, keepdims=True)
    le = logits[8:16]
    for g in range(1, N_EXPERT_GROUPS):
        le = jnp.where(g_idx == g, logits[8 + 8 * g:16 + 8 * g], le)
    ex = jnp.exp(le - jnp.max(le, axis=0, keepdims=True))
    pe = ex / jnp.sum(ex, axis=0, keepdims=True)
    p1 = jnp.max(pe, axis=0, keepdims=True)
    i1 = jnp.min(jnp.where(pe == p1, row8, 8), axis=0, keepdims=True)
    pe2 = jnp.where(row8 == i1, -1.0, pe)
    p2 = jnp.max(pe2, axis=0, keepdims=True)
    i2 = jnp.min(jnp.where(pe2 == p2, row8, 8), axis=0, keepdims=True)
    first_lo = i1 < i2
    w_lo = p_group * jnp.where(first_lo, p1, p2)
    w_hi = p_group * jnp.where(first_lo, p2, p1)
    cls = g_idx * 64 + jnp.minimum(i1, i2) * EXPERTS_PER_GROUP + jnp.maximum(i1, i2)
    return w_lo, w_hi, cls


def _tail_rank(w_lo, w_hi, cls, live, tri_ref, x1e_ref, mrow_ref, cout_ref, base_ref):
    tile = TILE
    onehot = jnp.logical_and(lax.broadcasted_iota(jnp.int32, (N_CLASS_IDS, tile), 0) == cls, live)
    ohb = onehot.astype(BF16)
    before = _dot(ohb, tri_ref[...])
    base = base_ref[...]
    base_t = jnp.concatenate([base] * (tile // META_W), axis=1)
    rank = jnp.sum(jnp.where(onehot, before + base_t, 0.0), axis=0, keepdims=True)
    new_base = base + _dot(ohb, jnp.ones((tile, META_W), BF16))
    base_ref[...] = new_base
    cout_ref[...] = new_base

    meta_t = jnp.concatenate([w_lo, w_hi, cls.astype(F32), rank, jnp.zeros((META_W - 4, tile), F32)], axis=0)
    x1e_ref[:, D_MODEL:ROW_W] = meta_t.T
    mrow_ref[...] = meta_t[0:8]


def _const_spec(shape):
    nd = len(shape)
    return pl.BlockSpec(shape, lambda i, _nd=nd: (0,) * _nd, pipeline_mode=pl.Buffered(1))


def _mixer(x, y_b, counts_in, lw, seq):
    n_tok = x.shape[0]
    n_tiles = n_tok // TILE
    hb = TILE // HALO
    n_hblk = n_tok // HALO
    consts = [lw["w_in"], lw["b_in"], lw["pool_w"], lw["pool_scale"], lw["conv_w"], lw["conv_b"],
              lw["sgu_ln_g"], lw["sgu_ln_b"], lw["sgu_w"], lw["sgu_b"], lw["branch_proj"], lw["w_out"],
              lw["ln1_g"], lw["ln1_b"], lw["w_r"], lw["b_r"], lw["tri"]]
    cur = lambda i: jnp.minimum(i, n_tiles - 1)
    return pl.pallas_call(
        functools.partial(_mixer_kernel, seq=seq, n_tiles=n_tiles),
        grid=(n_tiles + 1,),
        in_specs=[
            pl.BlockSpec((TILE, D_MODEL), lambda i: (cur(i), 0)),
            pl.BlockSpec((HALO, D_MODEL), lambda i: (jnp.maximum(cur(i) * hb - 1, 0), 0)),
            pl.BlockSpec((HALO, D_MODEL), lambda i: (jnp.minimum((cur(i) + 1) * hb, n_hblk - 1), 0)),
            pl.BlockSpec((TILE, BRANCH_W), lambda i: (cur(i), 0)),
            pl.BlockSpec(COUNT_SHAPE, lambda i: (0, 0)),
        ] + [_const_spec(c.shape) for c in consts],
        out_specs=[
            pl.BlockSpec((TILE, ROW_W), lambda i: (jnp.maximum(i - 1, 0), 0)),
            pl.BlockSpec((None, 8, TILE), lambda i: (jnp.maximum(i - 1, 0), 0, 0)),
            pl.BlockSpec(COUNT_SHAPE, lambda i: (0, 0)),
        ],
        out_shape=[
            jax.ShapeDtypeStruct((n_tok, ROW_W), F32),
            jax.ShapeDtypeStruct((n_tiles, 8, TILE), F32),
            jax.ShapeDtypeStruct(COUNT_SHAPE, F32),
        ],
        scratch_shapes=[pltpu.VMEM(COUNT_SHAPE, F32), pltpu.VMEM((TILE, D_MODEL), F32)],
        compiler_params=pltpu.CompilerParams(
            dimension_semantics=("arbitrary",), vmem_limit_bytes=VMEM_LIMIT),
        name="mixer",
    )(x, x, x, y_b, counts_in, *consts)


def _cumsum_sublanes(a):
    n = a.shape[0]
    row = lax.broadcasted_iota(jnp.int32, a.shape, 0)
    s = 1
    while s < n:
        a = a + jnp.where(row >= s, pltpu.roll(a, s, 0), 0.0)
        s *= 2
    return a


def _tables_kernel(cnt_ref, tab_ref, start_ref, *, nbp):
    cnt = cnt_ref[...]
    nb = jnp.floor((cnt + (MOE_BLOCK - 1)) * (1.0 / MOE_BLOCK))
    end = _cumsum_sublanes(nb)
    start_ref[...] = (end - nb) * MOE_BLOCK
    end_t = jnp.concatenate([end] * (nbp // META_W), axis=1)
    j = lax.broadcasted_iota(jnp.int32, (N_CLASS_IDS, nbp), 1).astype(F32)
    blk_cls = jnp.sum((end_t <= j).astype(F32), axis=0, keepdims=True)
    total = end_t[N_CLASS_IDS - 1:N_CLASS_IDS, :]
    jr = j[0:1, :]
    active = jr < total
    last_cls = jnp.max(jnp.where(active, blk_cls, 0.0), axis=-1, keepdims=True)
    cls_i = jnp.where(active, blk_cls, last_cls).astype(jnp.int32)
    grp = cls_i >> 6
    e_lo = (cls_i >> 3) & 7
    e_hi = cls_i & 7
    bidx = jnp.minimum(jr, total - 1.0).astype(jnp.int32)
    match = lax.broadcasted_iota(jnp.int32, (N_CLASS_IDS, nbp), 0) == cls_i
    reps = nbp // META_W
    cnt_j = jnp.sum(jnp.where(match, jnp.concatenate([cnt] * reps, axis=1), 0.0), axis=0, keepdims=True)
    first_j = jnp.sum(jnp.where(match, jnp.concatenate([end - nb] * reps, axis=1), 0.0), axis=0, keepdims=True)
    n_valid = jnp.where(active, jnp.clip(cnt_j - (jr - first_j) * MOE_BLOCK, 0.0, MOE_BLOCK), 0.0).astype(jnp.int32)
    zero = jnp.zeros_like(bidx)
    tab_ref[...] = jnp.concatenate([e_lo, e_hi, bidx, total.astype(jnp.int32), n_valid, grp, zero, zero], axis=0)


def _tables(counts, nbp):
    return pl.pallas_call(
        functools.partial(_tables_kernel, nbp=nbp),
        out_shape=[jax.ShapeDtypeStruct((8, nbp), jnp.int32),
                   jax.ShapeDtypeStruct(COUNT_SHAPE, F32)],
        name="moe_tables",
    )(counts)


def _dest_kernel(mrow_ref, start_ref, dest_ref, *, tiles):
    start = jnp.concatenate([start_ref[...]] * (TILE // META_W), axis=1)
    cls_ids = lax.broadcasted_iota(jnp.int32, (N_CLASS_IDS, TILE), 0)
    for t in range(tiles):
        cls = mrow_ref[t, 2:3, :].astype(jnp.int32)
        first = jnp.sum(jnp.where(cls_ids == cls, start, 0.0), axis=0, keepdims=True)
        dest_ref[t] = (first + mrow_ref[t, 3:4, :]).astype(jnp.int32)


def _dest(mrow, start):
    n_tiles = mrow.shape[0]
    tiles = math.gcd(n_tiles, DEST_TILES)
    return pl.pallas_call(
        functools.partial(_dest_kernel, tiles=tiles),
        grid=(n_tiles // tiles,),
        in_specs=[pl.BlockSpec((tiles, 8, TILE), lambda i: (i, 0, 0)),
                  pl.BlockSpec(COUNT_SHAPE, lambda i: (0, 0))],
        out_specs=pl.BlockSpec((tiles, 1, TILE), lambda i: (i, 0, 0)),
        out_shape=jax.ShapeDtypeStruct((n_tiles, 1, TILE), jnp.int32),
        compiler_params=pltpu.CompilerParams(dimension_semantics=("arbitrary",)),
        name="moe_dest",
    )(mrow, start)


def _sc_mesh():
    return plsc.VectorSubcoreMesh(core_axis_name="core", subcore_axis_name="subcore")


def _col_chunks(width):
    chunks, c = [], 0
    while c < width:
        w = min(SC_COLS, width - c)
        assert c % w == 0
        chunks.append((c // w, w))
        c += w
    return chunks


_SC_PARAMS = dict(core_axis_name=("core", "subcore"), dimension_semantics=(pltpu.PARALLEL,))


def _sc_scatter_rows(idx, src, n_out):
    n, d = src.shape

    @pl.kernel(out_type=jax.ShapeDtypeStruct((n_out, d), src.dtype), mesh=_sc_mesh(), scratch_types=[],
               compiler_params=pltpu.CompilerParams(use_tc_tiling_on_sc=True))
    def scatter(src_hbm, idx_hbm, out_hbm):
        for cb, cw in _col_chunks(d):
            def body(rows_vmem, idx_vmem, cb=cb, cw=cw):
                pltpu.sync_copy(rows_vmem, out_hbm.at[:, pl.ds(cb * cw, cw)].at[idx_vmem.at[0]])

            pltpu.emit_pipeline(
                body, grid=(n // SC_WINDOW,),
                in_specs=[pl.BlockSpec((SC_WINDOW, cw), lambda i, cb=cb: (i, cb)),
                          pl.BlockSpec((1, SC_WINDOW), lambda i: (0, i))],
                out_specs=[], **_SC_PARAMS)(src_hbm, idx_hbm)

    return scatter(src, idx)


def _sc_gather_rows(idx, src):
    n, d = idx.shape[1], src.shape[1]

    @pl.kernel(out_type=jax.ShapeDtypeStruct((n, d), src.dtype), mesh=_sc_mesh(), scratch_types=[],
               compiler_params=pltpu.CompilerParams(use_tc_tiling_on_sc=True))
    def gather(src_hbm, idx_hbm, out_hbm):
        for cb, cw in _col_chunks(d):
            def body(idx_vmem, rows_vmem, cb=cb, cw=cw):
                pltpu.sync_copy(src_hbm.at[:, pl.ds(cb * cw, cw)].at[idx_vmem.at[0]], rows_vmem)

            pltpu.emit_pipeline(
                body, grid=(n // SC_WINDOW,),
                in_specs=[pl.BlockSpec((1, SC_WINDOW), lambda i: (0, i))],
                out_specs=[pl.BlockSpec((SC_WINDOW, cw), lambda i, cb=cb: (i, cb))],
                **_SC_PARAMS)(idx_hbm, out_hbm)

    return gather(src, idx)


def _expert_kernel(grp_ref, elo_ref, ehi_ref, bidx_ref, nvalid_ref, tot_ref,
                   xb_ref, w1_ref, w3_ref, w2_ref, g_ref, b_ref, o_ref, pre_ref):
    del grp_ref, bidx_ref
    j = pl.program_id(0)
    total = tot_ref[0]

    @pl.when(j == 0)
    def _():
        pre_ref[...] = jnp.zeros_like(pre_ref)

    def finish_previous():
        o_ref[...] = _layer_norm(pre_ref[...], g_ref[...], b_ref[...])

    def hidden(half_a, b):
        return ((half_a * (jnp.tanh(half_a) + 1.0)) * b).astype(BF16)

    def run_block(rows):
        n_valid = nvalid_ref[j]
        x = jnp.where(lax.broadcasted_iota(jnp.int32, (rows, D_MODEL), 0) < n_valid, xb_ref[0:rows, 0:D_MODEL], 0.0)
        xh = x.astype(BF16)
        e_lo = elo_ref[j]
        e_hi = ehi_ref[j]
        live = lax.broadcasted_iota(jnp.int32, (rows, 1), 0) < n_valid
        w_lo = jnp.where(live, xb_ref[0:rows, D_MODEL:D_MODEL + 1], 0.0)
        w_hi = jnp.where(live, xb_ref[0:rows, D_MODEL + 1:D_MODEL + 2], 0.0)
        half_a_lo = _dot(xh, w1_ref[e_lo])
        b_lo = _dot(xh, w3_ref[e_lo])
        finish_previous()
        half_a_hi = _dot(xh, w1_ref[e_hi])
        b_hi = _dot(xh, w3_ref[e_hi])
        y_lo = _dot(hidden(half_a_lo, b_lo), w2_ref[e_lo])
        partial = DN_ALPHA * x + w_lo * y_lo
        y_hi = _dot(hidden(half_a_hi, b_hi), w2_ref[e_hi])
        pre_ref[0:rows, :] = partial + w_hi * y_hi

    half_full = nvalid_ref[jnp.minimum(j, total - 1)] <= MOE_BLOCK // 2

    @pl.when(jnp.logical_and(j < total, jnp.logical_not(half_full)))
    def _():
        run_block(MOE_BLOCK)

    @pl.when(jnp.logical_and(j < total, half_full))
    def _():
        run_block(MOE_BLOCK // 2)

    @pl.when(j == total)
    def _():
        finish_previous()


def _experts(tab, xb, lw):
    n_blocks = xb.shape[0] // MOE_BLOCK
    grp_w = lambda shape: pl.BlockSpec((EXPERTS_PER_GROUP,) + shape, lambda j, g, lo, hi, bi, nv, tot: (g[j], 0, 0),
                                       pipeline_mode=pl.Buffered(1))
    vec = pl.BlockSpec((1, D_MODEL), lambda j, g, lo, hi, bi, nv, tot: (0, 0))
    grid_spec = pltpu.PrefetchScalarGridSpec(
        num_scalar_prefetch=6,
        grid=(n_blocks + 1,),
        in_specs=[pl.BlockSpec((MOE_BLOCK, ROW_W), lambda j, g, lo, hi, bi, nv, tot: (bi[j], 0)),
                  grp_w((D_MODEL, D_EXPERT)), grp_w((D_MODEL, D_EXPERT)), grp_w((D_EXPERT, D_MODEL)), vec, vec],
        out_specs=pl.BlockSpec((MOE_BLOCK, D_MODEL),
                               lambda j, g, lo, hi, bi, nv, tot: (jnp.maximum(jnp.minimum(j, tot[0]) - 1, 0), 0)),
        scratch_shapes=[pltpu.VMEM((MOE_BLOCK, D_MODEL), F32)],
    )
    return pl.pallas_call(
        _expert_kernel,
        grid_spec=grid_spec,
        out_shape=jax.ShapeDtypeStruct((n_blocks * MOE_BLOCK, D_MODEL), F32),
        compiler_params=pltpu.CompilerParams(
            dimension_semantics=("arbitrary",), vmem_limit_bytes=VMEM_LIMIT),
        name="moe_experts",
    )(tab[5], tab[0], tab[1], tab[2], tab[4], tab[3, 0:1], xb,
      lw["w1"], lw["w3"], lw["w2"], lw["ln2_g"], lw["ln2_b"])


def _layer_weights(p, l):
    row = lambda a: a[l].reshape(1, -1).astype(F32)
    w_r = jnp.zeros((META_W, D_MODEL), F32)
    w_r = w_r.at[0:N_EXPERT_GROUPS].set(p["w_rg"][l].T).at[8:8 + N_EXPERTS].set(p["w_re"][l].T)
    b_r = jnp.zeros((META_W, 1), F32)
    b_r = b_r.at[0:N_EXPERT_GROUPS, 0].set(p["b_rg"][l]).at[N_EXPERT_GROUPS:8, 0].set(NEG_BIG)
    b_r = b_r.at[8:8 + N_EXPERTS, 0].set(p["b_re"][l])
    t = np.arange(TILE)
    gate_half = jnp.asarray(np.where(np.arange(p["w_in"].shape[-1]) >= OFF_G, 0.5, 1.0), F32)
    return {
        "w_in": (p["w_in"][l] * gate_half).astype(BF16),
        "b_in": row(p["b_in"]) * gate_half,
        "w_b": p["w_in"][l][:, OFF_B:OFF_B + BRANCH_W].astype(BF16),
        "b_b": p["b_in"][l][OFF_B:OFF_B + BRANCH_W].reshape(1, -1),
        "pool_w": jax.scipy.linalg.block_diag(*[p["pool_w"][l][g] for g in range(N_GROUPS)]).astype(BF16),
        "pool_scale": row(p["pool_scale"]),
        "conv_w": p["conv_w"][l],
        "conv_b": row(p["conv_b"]),
        "sgu_ln_g": row(p["sgu_ln_g"]),
        "sgu_ln_b": row(p["sgu_ln_b"]),
        "sgu_w": p["sgu_w"][l].astype(BF16),
        "sgu_b": jnp.repeat(p["sgu_b"][l].T, GROUP_W, axis=1),
        "branch_proj": (0.5 * p["branch_proj"][l]).astype(BF16),
        "w_out": p["w_out"][l].astype(BF16),
        "ln1_g": row(p["ln1_g"]),
        "ln1_b": row(p["ln1_b"]),
        "w_r": w_r,
        "b_r": b_r,
        "tri": jnp.asarray((t[:, None] < t[None, :]).astype(np.float32)).astype(BF16),
        "w1": (0.5 * p["w1"][l]).astype(BF16),
        "w3": p["w3"][l].astype(BF16),
        "w2": p["w2"][l].astype(BF16),
        "ln2_g": row(p["ln2_g"]),
        "ln2_b": row(p["ln2_b"]),
    }


def _encoder_layer(xs, seqs, lw):
    zero_counts = jnp.zeros(COUNT_SHAPE, F32)
    routed = []
    for x, seq in zip(xs, seqs):
        n_blocks = x.shape[0] // MOE_BLOCK + N_REACHABLE_CLASSES
        nbp = -(-(n_blocks + 1) // META_W) * META_W
        y_b = _fourier_mixer(x, lw["w_b"], lw["b_b"], seq)
        x1e, mrow, counts = _mixer(x, y_b, zero_counts, lw, seq)
        tab, start = _tables(counts, nbp)
        dest = _dest(mrow, start).reshape(1, -1)
        routed.append((tab, dest, _sc_scatter_rows(dest, x1e, n_blocks * MOE_BLOCK)))
    return [_sc_gather_rows(dest, _experts(tab, xb, lw)) for tab, dest, xb in routed]


def kernel(x_prompt, x_sample, w_in, b_in, pool_w, pool_scale, conv_w, conv_b, sgu_ln_g, sgu_ln_b,
           sgu_w, sgu_b, branch_proj, w_out, ln1_g, ln1_b, w_rg, b_rg, w_re, b_re, w1, w3, w2,
           ln2_g, ln2_b):
    p = dict(w_in=w_in, b_in=b_in, pool_w=pool_w, pool_scale=pool_scale, conv_w=conv_w, conv_b=conv_b,
             sgu_ln_g=sgu_ln_g, sgu_ln_b=sgu_ln_b, sgu_w=sgu_w, sgu_b=sgu_b, branch_proj=branch_proj,
             w_out=w_out, ln1_g=ln1_g, ln1_b=ln1_b, w_rg=w_rg, b_rg=b_rg, w_re=w_re, b_re=b_re,
             w1=w1, w3=w3, w2=w2, ln2_g=ln2_g, ln2_b=ln2_b)
    shapes = (x_prompt.shape, x_sample.shape)
    seqs = [s[1] for s in shapes]
    xs = [x_prompt.reshape(-1, D_MODEL), x_sample.reshape(-1, D_MODEL)]
    for l in range(w_in.shape[0]):
        xs = _encoder_layer(xs, seqs, _layer_weights(p, l))
    return tuple(x.reshape(s) for x, s in zip(xs, shapes))
```

```python
import functools
import math

import numpy as np
import jax
import jax.numpy as jnp
from jax import lax
from jax.experimental import pallas as pl
from jax.experimental.pallas import tpu as pltpu
from jax.experimental.pallas import tpu_sc as plsc

F32 = jnp.float32
BF16 = jnp.bfloat16

D_MODEL = 1024
DEPTH = 2
BRANCH_W = 256
N_GROUPS = 4
GROUP_W = 64
POOL_WINDOWS = (2, 4, 8, 16)
CHUNK = 128
OFF_A = 0
OFF_B = 256
OFF_CH = 512
OFF_DU = 1280
OFF_G = 1792
N_EXPERT_GROUPS = 4
EXPERTS_PER_GROUP = 8
N_EXPERTS = 32
D_EXPERT = 512
MOE_BLOCK = 256
DN_ALPHA = (2 * DEPTH) ** 0.25
LN_EPS = 1e-5

N_CLASS_IDS = 256
N_REACHABLE_CLASSES = N_EXPERT_GROUPS * (EXPERTS_PER_GROUP * (EXPERTS_PER_GROUP - 1) // 2)
META_W = 128
ROW_W = D_MODEL + META_W
COUNT_SHAPE = (N_CLASS_IDS, META_W)
HALO = 16
TILE = 512
DEST_TILES = 8
DMA_SPLIT = 4
SC_WINDOW = 128
SC_COLS = 256
NEG_BIG = -1e30
VMEM_LIMIT = 56 * 1024 * 1024
EXPERT_VMEM_LIMIT = 62 * 1024 * 1024


def _dot(a, b):
    return jnp.dot(a, b, preferred_element_type=F32)


def _layer_norm(x, g, b):
    mu = jnp.mean(x, axis=-1, keepdims=True)
    xc = x - mu
    var = jnp.mean(xc * xc, axis=-1, keepdims=True)
    return xc * lax.rsqrt(var + LN_EPS) * g + b


def _gelu_tanh(x):
    return 0.5 * x * (1.0 + jnp.tanh(math.sqrt(2.0 / math.pi) * (x + 0.044715 * (x * x * x))))


def _fft_factors(seq):
    n1 = 1 << (int(math.log2(seq)) // 2)
    return n1, seq // n1


@functools.lru_cache(maxsize=None)
def _fft_tables(seq):
    n1, n2 = _fft_factors(seq)
    c = np.arange(GROUP_W)
    ang = 2.0 * np.pi * ((c[:, None] * c[None, :]) % GROUP_W) / GROUP_W
    eye = np.eye(N_GROUPS)
    cs = np.concatenate([np.kron(eye, np.cos(ang)), np.kron(eye, np.sin(ang))], axis=1)
    k1 = np.arange(n1)
    t1 = np.arange(n1)
    t2 = np.arange(n2)
    t = t1[None, None, :] * n2 + t2[:, None, None]
    ang1 = 2.0 * np.pi * ((k1[None, :, None] * t) % seq) / seq
    gc, gs = np.cos(ang1), np.sin(ang1)
    lt = np.concatenate([np.concatenate([gc, -gs], axis=2), np.concatenate([-gs, -gc], axis=2)], axis=1)
    k2 = np.arange(n2)
    ang2 = 2.0 * np.pi * ((k2[:, None] * t2[None, :]) % n2) / n2
    scale = 1.0 / math.sqrt(seq * GROUP_W)
    to16 = lambda a: jnp.asarray(a, dtype=F32).astype(BF16)
    return to16(cs), to16(lt), to16(np.cos(ang2) * scale), to16(np.sin(ang2) * scale)


def _fft_stage1_kernel(*refs, n1, tb):
    x_refs, (wb_ref, bb_ref, cs_ref, lt_ref, o_ref, u_ref) = refs[:DMA_SPLIT], refs[DMA_SPLIT:]
    x = jnp.concatenate([r[...].reshape(n1 * tb, D_MODEL // DMA_SPLIT).astype(BF16) for r in x_refs], axis=1)
    zb = _dot(x, wb_ref[...]) + bb_ref[...]
    u = _dot(zb.astype(BF16), cs_ref[...])
    for c in range(4):
        u_ref[c] = u[:, c * 128:(c + 1) * 128]
    for j in range(tb):
        q = [u_ref[c, pl.ds(j, n1, stride=tb), :] for c in range(4)]
        stacked = jnp.concatenate([jnp.concatenate(q[0:2], axis=1),
                                   jnp.concatenate(q[2:4], axis=1)], axis=0).astype(BF16)
        b = _dot(lt_ref[j], stacked)
        o_ref[:, j * 512:j * 512 + BRANCH_W] = b[:n1].astype(BF16)
        o_ref[:, j * 512 + BRANCH_W:(j + 1) * 512] = b[n1:].astype(BF16)


def _fft_stage2_kernel(b_ref, c2_ref, s2_ref, o_ref, *, kb):
    for k in range(kb):
        slab = b_ref[k]
        y = _dot(c2_ref[...], slab[:, :BRANCH_W]) + _dot(s2_ref[...], slab[:, BRANCH_W:])
        o_ref[:, k * BRANCH_W:(k + 1) * BRANCH_W] = y.astype(BF16)


def _fourier_mixer(x, w_b, b_b, seq):
    n_tok = x.shape[0]
    bsz = n_tok // seq
    n1, n2 = _fft_factors(seq)
    tb = max(8, min(16, (4 << 20) // (n1 * D_MODEL * 4)))
    kb = min(n1, (2 << 20) // (n2 * 512 * 2))
    cs, lt, c2, s2 = _fft_tables(seq)
    x4 = x.reshape(bsz, n1, n2, D_MODEL)
    xw = D_MODEL // DMA_SPLIT
    bt = pl.pallas_call(
        functools.partial(_fft_stage1_kernel, n1=n1, tb=tb),
        grid=(bsz, n2 // tb),
        in_specs=[pl.BlockSpec((None, n1, tb, xw), lambda b, j, c=c: (b, 0, j, c)) for c in range(DMA_SPLIT)] + [
            pl.BlockSpec((D_MODEL, BRANCH_W), lambda b, j: (0, 0)),
            pl.BlockSpec((1, BRANCH_W), lambda b, j: (0, 0)),
            pl.BlockSpec((BRANCH_W, 2 * BRANCH_W), lambda b, j: (0, 0)),
            pl.BlockSpec((tb, 2 * n1, 2 * n1), lambda b, j: (j, 0, 0)),
        ],
        out_specs=pl.BlockSpec((None, n1, tb * 512), lambda b, j: (b, 0, j)),
        out_shape=jax.ShapeDtypeStruct((bsz, n1, n2 * 512), BF16),
        scratch_shapes=[pltpu.VMEM((4, n1 * tb, 128), F32)],
        compiler_params=pltpu.CompilerParams(
            dimension_semantics=("arbitrary", "arbitrary"), vmem_limit_bytes=VMEM_LIMIT),
        name="fft_stage1",
    )(*([x4] * DMA_SPLIT), w_b, b_b, cs, lt)
    bt4 = bt.reshape(bsz, n1, n2, 512)
    y = pl.pallas_call(
        functools.partial(_fft_stage2_kernel, kb=kb),
        grid=(bsz, n1 // kb),
        in_specs=[
            pl.BlockSpec((None, kb, n2, 512), lambda b, i: (b, i, 0, 0)),
            pl.BlockSpec((n2, n2), lambda b, i: (0, 0)),
            pl.BlockSpec((n2, n2), lambda b, i: (0, 0)),
        ],
        out_specs=pl.BlockSpec((None, n2, kb * BRANCH_W), lambda b, i: (b, 0, i)),
        out_shape=jax.ShapeDtypeStruct((bsz, n2, n1 * BRANCH_W), BF16),
        compiler_params=pltpu.CompilerParams(
            dimension_semantics=("arbitrary", "arbitrary"), vmem_limit_bytes=VMEM_LIMIT),
        name="fft_stage2",
    )(bt4, c2, s2)
    return y.reshape(n_tok, BRANCH_W)


def _mixer_kernel(x_ref, xp_ref, xn_ref, yb_ref, cin_ref,
                  win_ref, bin_ref, poolw_ref, pools_ref, convw_ref, convb_ref,
                  lng_ref, lnb_ref, sguw_ref, sgub_ref, bp_ref, wout_ref,
                  ln1g_ref, ln1b_ref, wr_ref, br_ref, tri_ref,
                  x1e_ref, mrow_ref, cout_ref, base_ref, pre_ref, *, seq, n_tiles):
    i = pl.program_id(0)
    tile = TILE
    ext = tile + 2 * HALO
    p0 = (jnp.minimum(i, n_tiles - 1) % (seq // tile)) * tile

    @pl.when(i == 0)
    def _():
        base_ref[...] = cin_ref[...]
        pre_ref[...] = jnp.zeros_like(pre_ref)

    live = i > 0

    x = x_ref[...]
    xc = x.astype(BF16)
    xe = jnp.concatenate([xp_ref[...].astype(BF16), xc, xn_ref[...].astype(BF16)], axis=0)
    pos = p0 - HALO + lax.broadcasted_iota(jnp.int32, (ext, BRANCH_W), 0)
    valid = (pos >= 0) & (pos < seq)
    lane = lax.broadcasted_iota(jnp.int32, (ext, BRANCH_W), 1)
    grp = lane // GROUP_W

    def half_gate_logits(k):
        lo = OFF_G + k * D_MODEL
        return _dot(xc, win_ref[:, lo:lo + D_MODEL]) + bin_ref[:, lo:lo + D_MODEL]

    def gated(half_gz, y_k, k):
        return (jnp.tanh(half_gz) + 1.0) * _dot(y_k.astype(BF16), bp_ref[k])

    za = jnp.where(valid, _dot(xe, win_ref[:, OFF_A:OFF_A + BRANCH_W]) + bin_ref[:, OFF_A:OFF_A + BRANCH_W], 0.0)
    zc = _dot(xe, win_ref[:, OFF_CH:OFF_DU]) + bin_ref[:, OFF_CH:OFF_DU]
    zd = _dot(xc, win_ref[:, OFF_DU:OFF_G]) + bin_ref[:, OFF_DU:OFF_G]
    hg0 = half_gate_logits(0)

    xh, xl = _tail_norm(pre_ref[...], ln1g_ref, ln1b_ref, x1e_ref)

    s2 = za + pltpu.roll(za, 1, 0)
    s4 = s2 + pltpu.roll(s2, 2, 0)
    s8 = s4 + pltpu.roll(s4, 4, 0)
    s16 = s8 + pltpu.roll(s8, 8, 0)
    c4 = pltpu.roll(s4, ext - 1, 0)
    c8 = pltpu.roll(s8, ext - 3, 0)
    c16 = pltpu.roll(s16, ext - 7, 0)
    wsum = jnp.where(grp == 0, s2, jnp.where(grp == 1, c4, jnp.where(grp == 2, c8, c16)))
    half = jnp.where(grp == 0, 1, jnp.where(grp == 1, 2, jnp.where(grp == 2, 4, 8)))
    cnt = jnp.minimum(pos + half, seq) - jnp.maximum(pos - half, 0)
    pooled = (wsum / jnp.maximum(cnt, 1).astype(F32) - za)[HALO:HALO + tile]
    y_a = _dot(pooled.astype(BF16), poolw_ref[...]) * pools_ref[...]
    hg1 = half_gate_logits(1)

    h = zc[:, 0:BRANCH_W]
    gate_b = zc[:, BRANCH_W:2 * BRANCH_W]
    gate_c = zc[:, 2 * BRANCH_W:3 * BRANCH_W]
    q = jnp.where(valid, gate_c * h, 0.0)
    q_prev = pltpu.roll(q, 1, 0)[HALO:HALO + tile]
    q_next = pltpu.roll(q, ext - 1, 0)[HALO:HALO + tile]
    conv = (q_prev * convw_ref[0:1, :] + q[HALO:HALO + tile] * convw_ref[1:2, :]
            + q_next * convw_ref[2:3, :] + convb_ref[...])
    y_c = gate_b[HALO:HALO + tile] * conv
    merged = gated(hg0, y_a, 0)
    hg2 = half_gate_logits(2)

    w_lo, w_hi, cls = _tail_route(xh, xl, wr_ref, br_ref)

    u = _gelu_tanh(zd[:, 0:BRANCH_W])
    v = _layer_norm(_gelu_tanh(zd[:, BRANCH_W:]), lng_ref[...], lnb_ref[...]).astype(BF16)
    grp_c = lax.broadcasted_iota(jnp.int32, (CHUNK, BRANCH_W), 1) // GROUP_W
    sps = []
    for c in range(tile // CHUNK):
        vch = v[c * CHUNK:(c + 1) * CHUNK]
        sp = None
        for g in range(N_GROUPS):
            r = _dot(sguw_ref[g], vch)
            sp = r if sp is None else jnp.where(grp_c == g, r, sp)
        sps.append(sp + sgub_ref[...])
    y_d = u * jnp.concatenate(sps, axis=0)
    merged = merged + gated(hg1, yb_ref[...], 1)
    hg3 = half_gate_logits(3)

    _tail_rank(w_lo, w_hi, cls, live, tri_ref, x1e_ref, mrow_ref, cout_ref, base_ref)

    merged = merged + gated(hg2, y_c, 2)
    merged = merged + gated(hg3, y_d, 3)
    pre_ref[...] = DN_ALPHA * x + _dot(merged.astype(BF16), wout_ref[...])


def _tail_norm(pre, ln1g_ref, ln1b_ref, x1e_ref):
    x1 = _layer_norm(pre, ln1g_ref[...], ln1b_ref[...])
    x1e_ref[:, 0:D_MODEL] = x1
    xh = x1.astype(BF16)
    return xh, (x1 - xh.astype(F32)).astype(BF16)


def _tail_route(xh, xl, wr_ref, br_ref):
    tile = TILE
    wr = wr_ref[...]
    wh = wr.astype(BF16)
    wl = (wr - wh.astype(F32)).astype(BF16)
    nt = (((1,), (1,)), ((), ()))
    dot_nt = lambda a, b: lax.dot_general(a, b, nt, preferred_element_type=F32)
    both = dot_nt(jnp.concatenate([wh, wl], axis=0), xh)
    logits = both[0:META_W] + both[META_W:] + dot_nt(wh, xl) + br_ref[...]
    row8 = lax.broadcasted_iota(jnp.int32, (EXPERTS_PER_GROUP, tile), 0)
    lg = logits[0:8]
    m = jnp.max(lg, axis=0, keepdims=True)
    g_idx = jnp.min(jnp.where(lg == m, row8, 8), axis=0, keepdims=True)
    p_group = 1.0 / jnp.sum(jnp.exp(lg - m), axis=0, keepdims=True)
    le = logits[8:16]
    for g in range(1, N_EXPERT_GROUPS):
        le = jnp.where(g_idx == g, logits[8 + 8 * g:16 + 8 * g], le)
    ex = jnp.exp(le - jnp.max(le, axis=0, keepdims=True))
    pe = ex / jnp.sum(ex, axis=0, keepdims=True)
    p1 = jnp.max(pe, axis=0, keepdims=True)
    i1 = jnp.min(jnp.where(pe == p1, row8, 8), axis=0, keepdims=True)
    pe2 = jnp.where(row8 == i1, -1.0, pe)
    p2 = jnp.max(pe2, axis=0, keepdims=True)
    i2 = jnp.min(jnp.where(pe2 == p2, row8, 8), axis=0, keepdims=True)
    first_lo = i1 < i2
    w_lo = p_group * jnp.where(first_lo, p1, p2)
    w_hi = p_group * jnp.where(first_lo, p2, p1)
    cls = g_idx * 64 + jnp.minimum(i1, i2) * EXPERTS_PER_GROUP + jnp.maximum(i1, i2)
    return w_lo, w_hi, cls


def _tail_rank(w_lo, w_hi, cls, live, tri_ref, x1e_ref, mrow_ref, cout_ref, base_ref):
    tile = TILE
    onehot = jnp.logical_and(lax.broadcasted_iota(jnp.int32, (N_CLASS_IDS, tile), 0) == cls, live)
    ohb = onehot.astype(BF16)
    before = _dot(ohb, tri_ref[...])
    base = base_ref[...]
    base_t = jnp.concatenate([base] * (tile // META_W), axis=1)
    rank = jnp.sum(jnp.where(onehot, before + base_t, 0.0), axis=0, keepdims=True)
    new_base = base + _dot(ohb, jnp.ones((tile, META_W), BF16))
    base_ref[...] = new_base
    cout_ref[...] = new_base

    meta_t = jnp.concatenate([w_lo, w_hi, cls.astype(F32), rank, jnp.zeros((META_W - 4, tile), F32)], axis=0)
    x1e_ref[:, D_MODEL:ROW_W] = meta_t.T
    mrow_ref[...] = meta_t[0:8]


def _const_spec(shape):
    nd = len(shape)
    return pl.BlockSpec(shape, lambda i, _nd=nd: (0,) * _nd, pipeline_mode=pl.Buffered(1))


def _mixer(x, y_b, counts_in, lw, seq):
    n_tok = x.shape[0]
    n_tiles = n_tok // TILE
    hb = TILE // HALO
    n_hblk = n_tok // HALO
    consts = [lw["w_in"], lw["b_in"], lw["pool_w"], lw["pool_scale"], lw["conv_w"], lw["conv_b"],
              lw["sgu_ln_g"], lw["sgu_ln_b"], lw["sgu_w"], lw["sgu_b"], lw["branch_proj"], lw["w_out"],
              lw["ln1_g"], lw["ln1_b"], lw["w_r"], lw["b_r"], lw["tri"]]
    cur = lambda i: jnp.minimum(i, n_tiles - 1)
    return pl.pallas_call(
        functools.partial(_mixer_kernel, seq=seq, n_tiles=n_tiles),
        grid=(n_tiles + 1,),
        in_specs=[
            pl.BlockSpec((TILE, D_MODEL), lambda i: (cur(i), 0)),
            pl.BlockSpec((HALO, D_MODEL), lambda i: (jnp.maximum(cur(i) * hb - 1, 0), 0)),
            pl.BlockSpec((HALO, D_MODEL), lambda i: (jnp.minimum((cur(i) + 1) * hb, n_hblk - 1), 0)),
            pl.BlockSpec((TILE, BRANCH_W), lambda i: (cur(i), 0)),
            pl.BlockSpec(COUNT_SHAPE, lambda i: (0, 0)),
        ] + [_const_spec(c.shape) for c in consts],
        out_specs=[
            pl.BlockSpec((TILE, ROW_W), lambda i: (jnp.maximum(i - 1, 0), 0)),
            pl.BlockSpec((None, 8, TILE), lambda i: (jnp.maximum(i - 1, 0), 0, 0)),
            pl.BlockSpec(COUNT_SHAPE, lambda i: (0, 0)),
        ],
        out_shape=[
            jax.ShapeDtypeStruct((n_tok, ROW_W), F32),
            jax.ShapeDtypeStruct((n_tiles, 8, TILE), F32),
            jax.ShapeDtypeStruct(COUNT_SHAPE, F32),
        ],
        scratch_shapes=[pltpu.VMEM(COUNT_SHAPE, F32), pltpu.VMEM((TILE, D_MODEL), F32)],
        compiler_params=pltpu.CompilerParams(
            dimension_semantics=("arbitrary",), vmem_limit_bytes=VMEM_LIMIT),
        name="mixer",
    )(x, x, x, y_b, counts_in, *consts)


def _cumsum_sublanes(a):
    n = a.shape[0]
    row = lax.broadcasted_iota(jnp.int32, a.shape, 0)
    s = 1
    while s < n:
        a = a + jnp.where(row >= s, pltpu.roll(a, s, 0), 0.0)
        s *= 2
    return a


def _tables_kernel(cnt_ref, tab_ref, start_ref, *, nbp):
    cnt = cnt_ref[...]
    nb = jnp.floor((cnt + (MOE_BLOCK - 1)) * (1.0 / MOE_BLOCK))
    end = _cumsum_sublanes(nb)
    start_ref[...] = (end - nb) * MOE_BLOCK
    end_t = jnp.concatenate([end] * (nbp // META_W), axis=1)
    j = lax.broadcasted_iota(jnp.int32, (N_CLASS_IDS, nbp), 1).astype(F32)
    blk_cls = jnp.sum((end_t <= j).astype(F32), axis=0, keepdims=True)
    total = end_t[N_CLASS_IDS - 1:N_CLASS_IDS, :]
    jr = j[0:1, :]
    active = jr < total
    last_cls = jnp.max(jnp.where(active, blk_cls, 0.0), axis=-1, keepdims=True)
    cls_i = jnp.where(active, blk_cls, last_cls).astype(jnp.int32)
    grp = cls_i >> 6
    e_lo = (cls_i >> 3) & 7
    e_hi = cls_i & 7
    bidx = jnp.minimum(jr, total - 1.0).astype(jnp.int32)
    match = lax.broadcasted_iota(jnp.int32, (N_CLASS_IDS, nbp), 0) == cls_i
    reps = nbp // META_W
    cnt_j = jnp.sum(jnp.where(match, jnp.concatenate([cnt] * reps, axis=1), 0.0), axis=0, keepdims=True)
    first_j = jnp.sum(jnp.where(match, jnp.concatenate([end - nb] * reps, axis=1), 0.0), axis=0, keepdims=True)
    n_valid = jnp.where(active, jnp.clip(cnt_j - (jr - first_j) * MOE_BLOCK, 0.0, MOE_BLOCK), 0.0).astype(jnp.int32)
    zero = jnp.zeros_like(bidx)
    tab_ref[...] = jnp.concatenate([e_lo, e_hi, bidx, total.astype(jnp.int32), n_valid, grp, zero, zero], axis=0)


def _tables(counts, nbp):
    return pl.pallas_call(
        functools.partial(_tables_kernel, nbp=nbp),
        out_shape=[jax.ShapeDtypeStruct((8, nbp), jnp.int32),
                   jax.ShapeDtypeStruct(COUNT_SHAPE, F32)],
        name="moe_tables",
    )(counts)


def _dest_kernel(mrow_ref, start_ref, dest_ref, *, tiles):
    start = jnp.concatenate([start_ref[...]] * (TILE // META_W), axis=1)
    cls_ids = lax.broadcasted_iota(jnp.int32, (N_CLASS_IDS, TILE), 0)
    for t in range(tiles):
        cls = mrow_ref[t, 2:3, :].astype(jnp.int32)
        first = jnp.sum(jnp.where(cls_ids == cls, start, 0.0), axis=0, keepdims=True)
        dest_ref[t] = (first + mrow_ref[t, 3:4, :]).astype(jnp.int32)


def _dest(mrow, start):
    n_tiles = mrow.shape[0]
    tiles = math.gcd(n_tiles, DEST_TILES)
    return pl.pallas_call(
        functools.partial(_dest_kernel, tiles=tiles),
        grid=(n_tiles // tiles,),
        in_specs=[pl.BlockSpec((tiles, 8, TILE), lambda i: (i, 0, 0)),
                  pl.BlockSpec(COUNT_SHAPE, lambda i: (0, 0))],
        out_specs=pl.BlockSpec((tiles, 1, TILE), lambda i: (i, 0, 0)),
        out_shape=jax.ShapeDtypeStruct((n_tiles, 1, TILE), jnp.int32),
        compiler_params=pltpu.CompilerParams(dimension_semantics=("arbitrary",)),
        name="moe_dest",
    )(mrow, start)


def _sc_mesh():
    return plsc.VectorSubcoreMesh(core_axis_name="core", subcore_axis_name="subcore")


def _col_chunks(width):
    chunks, c = [], 0
    while c < width:
        w = min(SC_COLS, width - c)
        assert c % w == 0
        chunks.append((c // w, w))
        c += w
    return chunks


_SC_PARAMS = dict(core_axis_name=("core", "subcore"), dimension_semantics=(pltpu.PARALLEL,))


def _sc_scatter_rows(idx, src, n_out):
    n, d = src.shape

    @pl.kernel(out_type=jax.ShapeDtypeStruct((n_out, d), src.dtype), mesh=_sc_mesh(), scratch_types=[],
               compiler_params=pltpu.CompilerParams(use_tc_tiling_on_sc=True))
    def scatter(src_hbm, idx_hbm, out_hbm):
        for cb, cw in _col_chunks(d):
            def body(rows_vmem, idx_vmem, cb=cb, cw=cw):
                pltpu.sync_copy(rows_vmem, out_hbm.at[:, pl.ds(cb * cw, cw)].at[idx_vmem.at[0]])

            pltpu.emit_pipeline(
                body, grid=(n // SC_WINDOW,),
                in_specs=[pl.BlockSpec((SC_WINDOW, cw), lambda i, cb=cb: (i, cb)),
                          pl.BlockSpec((1, SC_WINDOW), lambda i: (0, i))],
                out_specs=[], **_SC_PARAMS)(src_hbm, idx_hbm)

    return scatter(src, idx)


def _sc_gather_rows(idx, src):
    n, d = idx.shape[1], src.shape[1]

    @pl.kernel(out_type=jax.ShapeDtypeStruct((n, d), src.dtype), mesh=_sc_mesh(), scratch_types=[],
               compiler_params=pltpu.CompilerParams(use_tc_tiling_on_sc=True))
    def gather(src_hbm, idx_hbm, out_hbm):
        for cb, cw in _col_chunks(d):
            def body(idx_vmem, rows_vmem, cb=cb, cw=cw):
                pltpu.sync_copy(src_hbm.at[:, pl.ds(cb * cw, cw)].at[idx_vmem.at[0]], rows_vmem)

            pltpu.emit_pipeline(
                body, grid=(n // SC_WINDOW,),
                in_specs=[pl.BlockSpec((1, SC_WINDOW), lambda i: (0, i))],
                out_specs=[pl.BlockSpec((SC_WINDOW, cw), lambda i, cb=cb: (i, cb))],
                **_SC_PARAMS)(idx_hbm, out_hbm)

    return gather(src, idx)


def _expert_kernel(grp_ref, elo_ref, ehi_ref, bidx_ref, nvalid_ref, tot_ref,
                   xb_ref, w1_hbm, w3_hbm, w2_hbm, g_ref, b_ref, o_ref,
                   pre_ref, w1_ref, w3_ref, w2_ref, w_sem, state_ref):
    del bidx_ref
    j = pl.program_id(0)
    total = tot_ref[0]
    n_chunks = 3 * EXPERTS_PER_GROUP

    def start_chunk(grp, slot, c):
        e = c % EXPERTS_PER_GROUP
        for m, (hbm, buf) in enumerate(((w1_hbm, w1_ref), (w3_hbm, w3_ref), (w2_hbm, w2_ref))):
            @pl.when(c // EXPERTS_PER_GROUP == m)
            def _(hbm=hbm, buf=buf):
                pltpu.make_async_copy(hbm.at[grp * EXPERTS_PER_GROUP + e], buf.at[slot, e], w_sem.at[slot]).start()

    def wait_chunks(slot, count):
        def body(_, carry):
            pltpu.make_async_copy(w1_hbm.at[0], w1_ref.at[slot, 0], w_sem.at[slot]).wait()
            return carry
        lax.fori_loop(0, count, body, 0)

    def start_chunks(grp, slot, first, last):
        def body(c, carry):
            start_chunk(grp, slot, c)
            return carry
        lax.fori_loop(first, last, body, 0)

    @pl.when(j == 0)
    def _():
        pre_ref[...] = jnp.zeros_like(pre_ref)
        state_ref[0] = 1
        state_ref[1] = -1
        state_ref[2] = 0

    def finish_previous():
        o_ref[...] = _layer_norm(pre_ref[...], g_ref[...], b_ref[...])

    def hidden(half_a, b):
        return ((half_a * (jnp.tanh(half_a) + 1.0)) * b).astype(BF16)

    def run_block(rows):
        n_valid = nvalid_ref[j]
        x = jnp.where(lax.broadcasted_iota(jnp.int32, (rows, D_MODEL), 0) < n_valid, xb_ref[0:rows, 0:D_MODEL], 0.0)
        xh = x.astype(BF16)
        e_lo = elo_ref[j]
        e_hi = ehi_ref[j]
        live = lax.broadcasted_iota(jnp.int32, (rows, 1), 0) < n_valid
        w_lo = jnp.where(live, xb_ref[0:rows, D_MODEL:D_MODEL + 1], 0.0)
        w_hi = jnp.where(live, xb_ref[0:rows, D_MODEL + 1:D_MODEL + 2], 0.0)
        slot = state_ref[0]
        half_a_lo = _dot(xh, w1_ref[slot, e_lo])
        b_lo = _dot(xh, w3_ref[slot, e_lo])
        finish_previous()
        half_a_hi = _dot(xh, w1_ref[slot, e_hi])
        b_hi = _dot(xh, w3_ref[slot, e_hi])
        y_lo = _dot(hidden(half_a_lo, b_lo), w2_ref[slot, e_lo])
        partial = DN_ALPHA * x + w_lo * y_lo
        y_hi = _dot(hidden(half_a_hi, b_hi), w2_ref[slot, e_hi])
        pre_ref[0:rows, :] = partial + w_hi * y_hi

    @pl.when(j < total)
    def _():
        grp = grp_ref[j]

        @pl.when(jnp.logical_or(j == 0, grp != grp_ref[jnp.maximum(j - 1, 0)]))
        def _():
            slot = 1 - state_ref[0]
            copying = state_ref[1]

            @pl.when(copying != grp)
            def _():
                @pl.when(copying >= 0)
                def _():
                    wait_chunks(slot, state_ref[2])
                state_ref[2] = 0

            start_chunks(grp, slot, state_ref[2], n_chunks)
            wait_chunks(slot, n_chunks)
            state_ref[0] = slot
            state_ref[1] = jnp.where(grp + 1 < N_EXPERT_GROUPS, grp + 1, -1)
            state_ref[2] = 0

        @pl.when(jnp.logical_and(state_ref[1] >= 0, state_ref[2] < n_chunks))
        def _():
            start_chunk(state_ref[1], 1 - state_ref[0], state_ref[2])
            state_ref[2] = state_ref[2] + 1

    half_full = nvalid_ref[jnp.minimum(j, total - 1)] <= MOE_BLOCK // 2

    @pl.when(jnp.logical_and(j < total, jnp.logical_not(half_full)))
    def _():
        run_block(MOE_BLOCK)

    @pl.when(jnp.logical_and(j < total, half_full))
    def _():
        run_block(MOE_BLOCK // 2)

    @pl.when(j == total)
    def _():
        finish_previous()

        @pl.when(state_ref[1] >= 0)
        def _():
            wait_chunks(1 - state_ref[0], state_ref[2])
            state_ref[1] = -1


def _experts(tab, xb, lw):
    n_blocks = xb.shape[0] // MOE_BLOCK
    hbm = pl.BlockSpec(memory_space=pl.ANY)
    group_up = (2, EXPERTS_PER_GROUP, D_MODEL, D_EXPERT)
    group_down = (2, EXPERTS_PER_GROUP, D_EXPERT, D_MODEL)
    vec = pl.BlockSpec((1, D_MODEL), lambda j, g, lo, hi, bi, nv, tot: (0, 0))
    grid_spec = pltpu.PrefetchScalarGridSpec(
        num_scalar_prefetch=6,
        grid=(n_blocks + 1,),
        in_specs=[pl.BlockSpec((MOE_BLOCK, ROW_W), lambda j, g, lo, hi, bi, nv, tot: (bi[j], 0)),
                  hbm, hbm, hbm, vec, vec],
        out_specs=pl.BlockSpec((MOE_BLOCK, D_MODEL),
                               lambda j, g, lo, hi, bi, nv, tot: (jnp.maximum(jnp.minimum(j, tot[0]) - 1, 0), 0)),
        scratch_shapes=[pltpu.VMEM((MOE_BLOCK, D_MODEL), F32),
                        pltpu.VMEM(group_up, BF16), pltpu.VMEM(group_up, BF16), pltpu.VMEM(group_down, BF16),
                        pltpu.SemaphoreType.DMA((2,)), pltpu.SMEM((3,), jnp.int32)],
    )
    return pl.pallas_call(
        _expert_kernel,
        grid_spec=grid_spec,
        out_shape=jax.ShapeDtypeStruct((n_blocks * MOE_BLOCK, D_MODEL), F32),
        compiler_params=pltpu.CompilerParams(
            dimension_semantics=("arbitrary",), vmem_limit_bytes=EXPERT_VMEM_LIMIT),
        name="moe_experts",
    )(tab[5], tab[0], tab[1], tab[2], tab[4], tab[3, 0:1], xb,
      lw["w1"], lw["w3"], lw["w2"], lw["ln2_g"], lw["ln2_b"])


def _layer_weights(p, l):
    row = lambda a: a[l].reshape(1, -1).astype(F32)
    w_r = jnp.zeros((META_W, D_MODEL), F32)
    w_r = w_r.at[0:N_EXPERT_GROUPS].set(p["w_rg"][l].T).at[8:8 + N_EXPERTS].set(p["w_re"][l].T)
    b_r = jnp.zeros((META_W, 1), F32)
    b_r = b_r.at[0:N_EXPERT_GROUPS, 0].set(p["b_rg"][l]).at[N_EXPERT_GROUPS:8, 0].set(NEG_BIG)
    b_r = b_r.at[8:8 + N_EXPERTS, 0].set(p["b_re"][l])
    t = np.arange(TILE)
    gate_half = jnp.asarray(np.where(np.arange(p["w_in"].shape[-1]) >= OFF_G, 0.5, 1.0), F32)
    return {
        "w_in": (p["w_in"][l] * gate_half).astype(BF16),
        "b_in": row(p["b_in"]) * gate_half,
        "w_b": p["w_in"][l][:, OFF_B:OFF_B + BRANCH_W].astype(BF16),
        "b_b": p["b_in"][l][OFF_B:OFF_B + BRANCH_W].reshape(1, -1),
        "pool_w": jax.scipy.linalg.block_diag(*[p["pool_w"][l][g] for g in range(N_GROUPS)]).astype(BF16),
        "pool_scale": row(p["pool_scale"]),
        "conv_w": p["conv_w"][l],
        "conv_b": row(p["conv_b"]),
        "sgu_ln_g": row(p["sgu_ln_g"]),
        "sgu_ln_b": row(p["sgu_ln_b"]),
        "sgu_w": p["sgu_w"][l].astype(BF16),
        "sgu_b": jnp.repeat(p["sgu_b"][l].T, GROUP_W, axis=1),
        "branch_proj": (0.5 * p["branch_proj"][l]).astype(BF16),
        "w_out": p["w_out"][l].astype(BF16),
        "ln1_g": row(p["ln1_g"]),
        "ln1_b": row(p["ln1_b"]),
        "w_r": w_r,
        "b_r": b_r,
        "tri": jnp.asarray((t[:, None] < t[None, :]).astype(np.float32)).astype(BF16),
        "w1": (0.5 * p["w1"][l]).astype(BF16),
        "w3": p["w3"][l].astype(BF16),
        "w2": p["w2"][l].astype(BF16),
        "ln2_g": row(p["ln2_g"]),
        "ln2_b": row(p["ln2_b"]),
    }


def _encoder_layer(xs, seqs, lw):
    zero_counts = jnp.zeros(COUNT_SHAPE, F32)
    routed = []
    for x, seq in zip(xs, seqs):
        n_blocks = x.shape[0] // MOE_BLOCK + N_REACHABLE_CLASSES
        nbp = -(-(n_blocks + 1) // META_W) * META_W
        y_b = _fourier_mixer(x, lw["w_b"], lw["b_b"], seq)
        x1e, mrow, counts = _mixer(x, y_b, zero_counts, lw, seq)
        tab, start = _tables(counts, nbp)
        dest = _dest(mrow, start).reshape(1, -1)
        routed.append((tab, dest, _sc_scatter_rows(dest, x1e, n_blocks * MOE_BLOCK)))
    return [_sc_gather_rows(dest, _experts(tab, xb, lw)) for tab, dest, xb in routed]


def kernel(x_prompt, x_sample, w_in, b_in, pool_w, pool_scale, conv_w, conv_b, sgu_ln_g, sgu_ln_b,
           sgu_w, sgu_b, branch_proj, w_out, ln1_g, ln1_b, w_rg, b_rg, w_re, b_re, w1, w3, w2,
           ln2_g, ln2_b):
    p = dict(w_in=w_in, b_in=b_in, pool_w=pool_w, pool_scale=pool_scale, conv_w=conv_w, conv_b=conv_b,
             sgu_ln_g=sgu_ln_g, sgu_ln_b=sgu_ln_b, sgu_w=sgu_w, sgu_b=sgu_b, branch_proj=branch_proj,
             w_out=w_out, ln1_g=ln1_g, ln1_b=ln1_b, w_rg=w_rg, b_rg=b_rg, w_re=w_re, b_re=b_re,
             w1=w1, w3=w3, w2=w2, ln2_g=ln2_g, ln2_b=ln2_b)
    shapes = (x_prompt.shape, x_sample.shape)
    seqs = [s[1] for s in shapes]
    xs = [x_prompt.reshape(-1, D_MODEL), x_sample.reshape(-1, D_MODEL)]
    for l in range(w_in.shape[0]):
        xs = _encoder_layer(xs, seqs, _layer_weights(p, l))
    return tuple(x.reshape(s) for x, s in zip(xs, shapes))
```

```python
import functools
import math

import numpy as np
import jax
import jax.numpy as jnp
from jax import lax
from jax.experimental import pallas as pl
from jax.experimental.pallas import tpu as pltpu
from jax.experimental.pallas import tpu_sc as plsc

F32 = jnp.float32
BF16 = jnp.bfloat16

D_MODEL = 1024
DEPTH = 2
BRANCH_W = 256
N_GROUPS = 4
GROUP_W = 64
POOL_WINDOWS = (2, 4, 8, 16)
CHUNK = 128
OFF_A = 0
OFF_B = 256
OFF_CH = 512
OFF_DU = 1280
OFF_G = 1792
N_EXPERT_GROUPS = 4
EXPERTS_PER_GROUP = 8
N_EXPERTS = 32
D_EXPERT = 512
MOE_BLOCK = 256
DN_ALPHA = (2 * DEPTH) ** 0.25
LN_EPS = 1e-5

N_CLASS_IDS = 256
N_REACHABLE_CLASSES = N_EXPERT_GROUPS * (EXPERTS_PER_GROUP * (EXPERTS_PER_GROUP - 1) // 2)
META_W = 128
ROW_W = D_MODEL + META_W
COUNT_SHAPE = (N_CLASS_IDS, META_W)
HALO = 16
TILE = 512
DEST_TILES = 8
DMA_SPLIT = 4
SC_WINDOW = 128
SC_COLS = 256
NEG_BIG = -1e30
VMEM_LIMIT = 56 * 1024 * 1024
EXPERT_VMEM_LIMIT = 62 * 1024 * 1024


def _dot(a, b):
    return jnp.dot(a, b, preferred_element_type=F32)


def _layer_norm(x, g, b):
    mu = jnp.mean(x, axis=-1, keepdims=True)
    xc = x - mu
    var = jnp.mean(xc * xc, axis=-1, keepdims=True)
    return xc * lax.rsqrt(var + LN_EPS) * g + b


def _gelu_tanh(x):
    return 0.5 * x * (1.0 + jnp.tanh(math.sqrt(2.0 / math.pi) * (x + 0.044715 * (x * x * x))))


def _fft_factors(seq):
    n1 = 1 << (int(math.log2(seq)) // 2)
    return n1, seq // n1


@functools.lru_cache(maxsize=None)
def _fft_tables(seq):
    n1, n2 = _fft_factors(seq)
    c = np.arange(GROUP_W)
    ang = 2.0 * np.pi * ((c[:, None] * c[None, :]) % GROUP_W) / GROUP_W
    eye = np.eye(N_GROUPS)
    cs = np.concatenate([np.kron(eye, np.cos(ang)), np.kron(eye, np.sin(ang))], axis=1)
    k1 = np.arange(n1)
    t1 = np.arange(n1)
    t2 = np.arange(n2)
    t = t1[None, None, :] * n2 + t2[:, None, None]
    ang1 = 2.0 * np.pi * ((k1[None, :, None] * t) % seq) / seq
    gc, gs = np.cos(ang1), np.sin(ang1)
    lt = np.concatenate([np.concatenate([gc, -gs], axis=2), np.concatenate([-gs, -gc], axis=2)], axis=1)
    k2 = np.arange(n2)
    ang2 = 2.0 * np.pi * ((k2[:, None] * t2[None, :]) % n2) / n2
    scale = 1.0 / math.sqrt(seq * GROUP_W)
    to16 = lambda a: jnp.asarray(a, dtype=F32).astype(BF16)
    return to16(cs), to16(lt), to16(np.cos(ang2) * scale), to16(np.sin(ang2) * scale)


def _fft_stage1_kernel(*refs, n1, tb):
    x_refs, (wb_ref, bb_ref, cs_ref, lt_ref, o_ref, u_ref) = refs[:DMA_SPLIT], refs[DMA_SPLIT:]
    x = jnp.concatenate([r[...].reshape(n1 * tb, D_MODEL // DMA_SPLIT).astype(BF16) for r in x_refs], axis=1)
    zb = _dot(x, wb_ref[...]) + bb_ref[...]
    u = _dot(zb.astype(BF16), cs_ref[...])
    for c in range(4):
        u_ref[c] = u[:, c * 128:(c + 1) * 128]
    for j in range(tb):
        q = [u_ref[c, pl.ds(j, n1, stride=tb), :] for c in range(4)]
        stacked = jnp.concatenate([jnp.concatenate(q[0:2], axis=1),
                                   jnp.concatenate(q[2:4], axis=1)], axis=0).astype(BF16)
        b = _dot(lt_ref[j], stacked)
        o_ref[:, j * 512:j * 512 + BRANCH_W] = b[:n1].astype(BF16)
        o_ref[:, j * 512 + BRANCH_W:(j + 1) * 512] = b[n1:].astype(BF16)


def _fft_stage2_kernel(b_ref, c2_ref, s2_ref, o_ref, *, kb):
    for k in range(kb):
        slab = b_ref[k]
        y = _dot(c2_ref[...], slab[:, :BRANCH_W]) + _dot(s2_ref[...], slab[:, BRANCH_W:])
        o_ref[:, k * BRANCH_W:(k + 1) * BRANCH_W] = y.astype(BF16)


def _fourier_mixer(x, w_b, b_b, seq):
    n_tok = x.shape[0]
    bsz = n_tok // seq
    n1, n2 = _fft_factors(seq)
    tb = max(8, min(16, (4 << 20) // (n1 * D_MODEL * 4)))
    kb = min(n1, (2 << 20) // (n2 * 512 * 2))
    cs, lt, c2, s2 = _fft_tables(seq)
    x4 = x.reshape(bsz, n1, n2, D_MODEL)
    xw = D_MODEL // DMA_SPLIT
    bt = pl.pallas_call(
        functools.partial(_fft_stage1_kernel, n1=n1, tb=tb),
        grid=(bsz, n2 // tb),
        in_specs=[pl.BlockSpec((None, n1, tb, xw), lambda b, j, c=c: (b, 0, j, c)) for c in range(DMA_SPLIT)] + [
            pl.BlockSpec((D_MODEL, BRANCH_W), lambda b, j: (0, 0)),
            pl.BlockSpec((1, BRANCH_W), lambda b, j: (0, 0)),
            pl.BlockSpec((BRANCH_W, 2 * BRANCH_W), lambda b, j: (0, 0)),
            pl.BlockSpec((tb, 2 * n1, 2 * n1), lambda b, j: (j, 0, 0)),
        ],
        out_specs=pl.BlockSpec((None, n1, tb * 512), lambda b, j: (b, 0, j)),
        out_shape=jax.ShapeDtypeStruct((bsz, n1, n2 * 512), BF16),
        scratch_shapes=[pltpu.VMEM((4, n1 * tb, 128), F32)],
        compiler_params=pltpu.CompilerParams(
            dimension_semantics=("arbitrary", "arbitrary"), vmem_limit_bytes=VMEM_LIMIT),
        name="fft_stage1",
    )(*([x4] * DMA_SPLIT), w_b, b_b, cs, lt)
    bt4 = bt.reshape(bsz, n1, n2, 512)
    y = pl.pallas_call(
        functools.partial(_fft_stage2_kernel, kb=kb),
        grid=(bsz, n1 // kb),
        in_specs=[
            pl.BlockSpec((None, kb, n2, 512), lambda b, i: (b, i, 0, 0)),
            pl.BlockSpec((n2, n2), lambda b, i: (0, 0)),
            pl.BlockSpec((n2, n2), lambda b, i: (0, 0)),
        ],
        out_specs=pl.BlockSpec((None, n2, kb * BRANCH_W), lambda b, i: (b, 0, i)),
        out_shape=jax.ShapeDtypeStruct((bsz, n2, n1 * BRANCH_W), BF16),
        compiler_params=pltpu.CompilerParams(
            dimension_semantics=("arbitrary", "arbitrary"), vmem_limit_bytes=VMEM_LIMIT),
        name="fft_stage2",
    )(bt4, c2, s2)
    return y.reshape(n_tok, BRANCH_W)


def _mixer_kernel(x_ref, xp_ref, xn_ref, yb_ref, cin_ref,
                  win_ref, bin_ref, poolw_ref, pools_ref, convw_ref, convb_ref,
                  lng_ref, lnb_ref, sguw_ref, sgub_ref, bp_ref, wout_ref,
                  ln1g_ref, ln1b_ref, wr_ref, br_ref, tri_ref,
                  x1e_ref, mrow_ref, cout_ref, base_ref, pre_ref, *, seq, n_tiles):
    i = pl.program_id(0)
    tile = TILE
    ext = tile + 2 * HALO
    p0 = (jnp.minimum(i, n_tiles - 1) % (seq // tile)) * tile

    @pl.when(i == 0)
    def _():
        base_ref[...] = cin_ref[...]
        pre_ref[...] = jnp.zeros_like(pre_ref)

    live = i > 0

    x = x_ref[...]
    xc = x.astype(BF16)
    xe = jnp.concatenate([xp_ref[...].astype(BF16), xc, xn_ref[...].astype(BF16)], axis=0)
    pos = p0 - HALO + lax.broadcasted_iota(jnp.int32, (ext, BRANCH_W), 0)
    valid = (pos >= 0) & (pos < seq)
    lane = lax.broadcasted_iota(jnp.int32, (ext, BRANCH_W), 1)
    grp = lane // GROUP_W

    def half_gate_logits(k):
        lo = OFF_G + k * D_MODEL
        return _dot(xc, win_ref[:, lo:lo + D_MODEL]) + bin_ref[:, lo:lo + D_MODEL]

    def gated(half_gz, y_k, k):
        return (jnp.tanh(half_gz) + 1.0) * _dot(y_k.astype(BF16), bp_ref[k])

    za = jnp.where(valid, _dot(xe, win_ref[:, OFF_A:OFF_A + BRANCH_W]) + bin_ref[:, OFF_A:OFF_A + BRANCH_W], 0.0)
    zc = _dot(xe, win_ref[:, OFF_CH:OFF_DU]) + bin_ref[:, OFF_CH:OFF_DU]
    zd = _dot(xc, win_ref[:, OFF_DU:OFF_G]) + bin_ref[:, OFF_DU:OFF_G]
    hg0 = half_gate_logits(0)

    xh, xl = _tail_norm(pre_ref[...], ln1g_ref, ln1b_ref, x1e_ref)

    s2 = za + pltpu.roll(za, 1, 0)
    s4 = s2 + pltpu.roll(s2, 2, 0)
    s8 = s4 + pltpu.roll(s4, 4, 0)
    s16 = s8 + pltpu.roll(s8, 8, 0)
    c4 = pltpu.roll(s4, ext - 1, 0)
    c8 = pltpu.roll(s8, ext - 3, 0)
    c16 = pltpu.roll(s16, ext - 7, 0)
    wsum = jnp.where(grp == 0, s2, jnp.where(grp == 1, c4, jnp.where(grp == 2, c8, c16)))
    half = jnp.where(grp == 0, 1, jnp.where(grp == 1, 2, jnp.where(grp == 2, 4, 8)))
    cnt = jnp.minimum(pos + half, seq) - jnp.maximum(pos - half, 0)
    pooled = (wsum / jnp.maximum(cnt, 1).astype(F32) - za)[HALO:HALO + tile]
    y_a = _dot(pooled.astype(BF16), poolw_ref[...]) * pools_ref[...]
    hg1 = half_gate_logits(1)

    h = zc[:, 0:BRANCH_W]
    gate_b = zc[:, BRANCH_W:2 * BRANCH_W]
    gate_c = zc[:, 2 * BRANCH_W:3 * BRANCH_W]
    q = jnp.where(valid, gate_c * h, 0.0)
    q_prev = pltpu.roll(q, 1, 0)[HALO:HALO + tile]
    q_next = pltpu.roll(q, ext - 1, 0)[HALO:HALO + tile]
    conv = (q_prev * convw_ref[0:1, :] + q[HALO:HALO + tile] * convw_ref[1:2, :]
            + q_next * convw_ref[2:3, :] + convb_ref[...])
    y_c = gate_b[HALO:HALO + tile] * conv
    merged = gated(hg0, y_a, 0)
    hg2 = half_gate_logits(2)

    w_lo, w_hi, cls = _tail_route(xh, xl, wr_ref, br_ref)

    u = _gelu_tanh(zd[:, 0:BRANCH_W])
    v = _layer_norm(_gelu_tanh(zd[:, BRANCH_W:]), lng_ref[...], lnb_ref[...]).astype(BF16)
    grp_c = lax.broadcasted_iota(jnp.int32, (CHUNK, BRANCH_W), 1) // GROUP_W
    sps = []
    for c in range(tile // CHUNK):
        vch = v[c * CHUNK:(c + 1) * CHUNK]
        sp = None
        for g in range(N_GROUPS):
            r = _dot(sguw_ref[g], vch)
            sp = r if sp is None else jnp.where(grp_c == g, r, sp)
        sps.append(sp + sgub_ref[...])
    y_d = u * jnp.concatenate(sps, axis=0)
    merged = merged + gated(hg1, yb_ref[...], 1)
    hg3 = half_gate_logits(3)

    _tail_rank(w_lo, w_hi, cls, live, tri_ref, x1e_ref, mrow_ref, cout_ref, base_ref)

    merged = merged + gated(hg2, y_c, 2)
    merged = merged + gated(hg3, y_d, 3)
    pre_ref[...] = DN_ALPHA * x + _dot(merged.astype(BF16), wout_ref[...])


def _tail_norm(pre, ln1g_ref, ln1b_ref, x1e_ref):
    x1 = _layer_norm(pre, ln1g_ref[...], ln1b_ref[...])
    x1e_ref[:, 0:D_MODEL] = x1
    xh = x1.astype(BF16)
    return xh, (x1 - xh.astype(F32)).astype(BF16)


def _tail_route(xh, xl, wr_ref, br_ref):
    tile = TILE
    wr = wr_ref[...]
    wh = wr.astype(BF16)
    wl = (wr - wh.astype(F32)).astype(BF16)
    nt = (((1,), (1,)), ((), ()))
    dot_nt = lambda a, b: lax.dot_general(a, b, nt, preferred_element_type=F32)
    both = dot_nt(jnp.concatenate([wh, wl], axis=0), xh)
    logits = both[0:META_W] + both[META_W:] + dot_nt(wh, xl) + br_ref[...]
    row8 = lax.broadcasted_iota(jnp.int32, (EXPERTS_PER_GROUP, tile), 0)
    lg = logits[0:8]
    m = jnp.max(lg, axis=0, keepdims=True)
    g_idx = jnp.min(jnp.where(lg == m, row8, 8), axis=0, keepdims=True)
    p_group = 1.0 / jnp.sum(jnp.exp(lg - m), axis=0, keepdims=True)
    le = logits[8:16]
    for g in range(1, N_EXPERT_GROUPS):
        le = jnp.where(g_idx == g, logits[8 + 8 * g:16 + 8 * g], le)
    ex = jnp.exp(le - jnp.max(le, axis=0, keepdims=True))
    pe = ex / jnp.sum(ex, axis=0, keepdims=True)
    p1 = jnp.max(pe, axis=0, keepdims=True)
    i1 = jnp.min(jnp.where(pe == p1, row8, 8), axis=0, keepdims=True)
    pe2 = jnp.where(row8 == i1, -1.0, pe)
    p2 = jnp.max(pe2, axis=0, keepdims=True)
    i2 = jnp.min(jnp.where(pe2 == p2, row8, 8), axis=0, keepdims=True)
    first_lo = i1 < i2
    w_lo = p_group * jnp.where(first_lo, p1, p2)
    w_hi = p_group * jnp.where(first_lo, p2, p1)
    cls = g_idx * 64 + jnp.minimum(i1, i2) * EXPERTS_PER_GROUP + jnp.maximum(i1, i2)
    return w_lo, w_hi, cls


def _tail_rank(w_lo, w_hi, cls, live, tri_ref, x1e_ref, mrow_ref, cout_ref, base_ref):
    tile = TILE
    onehot = jnp.logical_and(lax.broadcasted_iota(jnp.int32, (N_CLASS_IDS, tile), 0) == cls, live)
    ohb = onehot.astype(BF16)
    before = _dot(ohb, tri_ref[...])
    base = base_ref[...]
    base_t = jnp.concatenate([base] * (tile // META_W), axis=1)
    rank = jnp.sum(jnp.where(onehot, before + base_t, 0.0), axis=0, keepdims=True)
    new_base = base + _dot(ohb, jnp.ones((tile, META_W), BF16))
    base_ref[...] = new_base
    cout_ref[...] = new_base

    meta_t = jnp.concatenate([w_lo, w_hi, cls.astype(F32), rank, jnp.zeros((META_W - 4, tile), F32)], axis=0)
    x1e_ref[:, D_MODEL:ROW_W] = meta_t.T
    mrow_ref[...] = meta_t[0:8]


def _const_spec(shape):
    nd = len(shape)
    return pl.BlockSpec(shape, lambda i, _nd=nd: (0,) * _nd, pipeline_mode=pl.Buffered(1))


def _mixer(x, y_b, counts_in, lw, seq):
    n_tok = x.shape[0]
    n_tiles = n_tok // TILE
    hb = TILE // HALO
    n_hblk = n_tok // HALO
    consts = [lw["w_in"], lw["b_in"], lw["pool_w"], lw["pool_scale"], lw["conv_w"], lw["conv_b"],
              lw["sgu_ln_g"], lw["sgu_ln_b"], lw["sgu_w"], lw["sgu_b"], lw["branch_proj"], lw["w_out"],
              lw["ln1_g"], lw["ln1_b"], lw["w_r"], lw["b_r"], lw["tri"]]
    cur = lambda i: jnp.minimum(i, n_tiles - 1)
    return pl.pallas_call(
        functools.partial(_mixer_kernel, seq=seq, n_tiles=n_tiles),
        grid=(n_tiles + 1,),
        in_specs=[
            pl.BlockSpec((TILE, D_MODEL), lambda i: (cur(i), 0)),
            pl.BlockSpec((HALO, D_MODEL), lambda i: (jnp.maximum(cur(i) * hb - 1, 0), 0)),
            pl.BlockSpec((HALO, D_MODEL), lambda i: (jnp.minimum((cur(i) + 1) * hb, n_hblk - 1), 0)),
            pl.BlockSpec((TILE, BRANCH_W), lambda i: (cur(i), 0)),
            pl.BlockSpec(COUNT_SHAPE, lambda i: (0, 0)),
        ] + [_const_spec(c.shape) for c in consts],
        out_specs=[
            pl.BlockSpec((TILE, ROW_W), lambda i: (jnp.maximum(i - 1, 0), 0)),
            pl.BlockSpec((None, 8, TILE), lambda i: (jnp.maximum(i - 1, 0), 0, 0)),
            pl.BlockSpec(COUNT_SHAPE, lambda i: (0, 0)),
        ],
        out_shape=[
            jax.ShapeDtypeStruct((n_tok, ROW_W), F32),
            jax.ShapeDtypeStruct((n_tiles, 8, TILE), F32),
            jax.ShapeDtypeStruct(COUNT_SHAPE, F32),
        ],
        scratch_shapes=[pltpu.VMEM(COUNT_SHAPE, F32), pltpu.VMEM((TILE, D_MODEL), F32)],
        compiler_params=pltpu.CompilerParams(
            dimension_semantics=("arbitrary",), vmem_limit_bytes=VMEM_LIMIT),
        name="mixer",
    )(x, x, x, y_b, counts_in, *consts)


def _cumsum_sublanes(a):
    n = a.shape[0]
    row = lax.broadcasted_iota(jnp.int32, a.shape, 0)
    s = 1
    while s < n:
        a = a + jnp.where(row >= s, pltpu.roll(a, s, 0), 0.0)
        s *= 2
    return a


def _tables_kernel(cnt_ref, tab_ref, start_ref, *, nbp):
    cnt = cnt_ref[...]
    nb = jnp.floor((cnt + (MOE_BLOCK - 1)) * (1.0 / MOE_BLOCK))
    end = _cumsum_sublanes(nb)
    start_ref[...] = (end - nb) * MOE_BLOCK
    end_t = jnp.concatenate([end] * (nbp // META_W), axis=1)
    j = lax.broadcasted_iota(jnp.int32, (N_CLASS_IDS, nbp), 1).astype(F32)
    blk_cls = jnp.sum((end_t <= j).astype(F32), axis=0, keepdims=True)
    total = end_t[N_CLASS_IDS - 1:N_CLASS_IDS, :]
    jr = j[0:1, :]
    active = jr < total
    last_cls = jnp.max(jnp.where(active, blk_cls, 0.0), axis=-1, keepdims=True)
    cls_i = jnp.where(active, blk_cls, last_cls).astype(jnp.int32)
    grp = cls_i >> 6
    e_lo = (cls_i >> 3) & 7
    e_hi = cls_i & 7
    bidx = jnp.minimum(jr, total - 1.0).astype(jnp.int32)
    match = lax.broadcasted_iota(jnp.int32, (N_CLASS_IDS, nbp), 0) == cls_i
    reps = nbp // META_W
    cnt_j = jnp.sum(jnp.where(match, jnp.concatenate([cnt] * reps, axis=1), 0.0), axis=0, keepdims=True)
    first_j = jnp.sum(jnp.where(match, jnp.concatenate([end - nb] * reps, axis=1), 0.0), axis=0, keepdims=True)
    n_valid = jnp.where(active, jnp.clip(cnt_j - (jr - first_j) * MOE_BLOCK, 0.0, MOE_BLOCK), 0.0).astype(jnp.int32)
    zero = jnp.zeros_like(bidx)
    tab_ref[...] = jnp.concatenate([e_lo, e_hi, bidx, total.astype(jnp.int32), n_valid, grp, zero, zero], axis=0)


def _tables(counts, nbp):
    return pl.pallas_call(
        functools.partial(_tables_kernel, nbp=nbp),
        out_shape=[jax.ShapeDtypeStruct((8, nbp), jnp.int32),
                   jax.ShapeDtypeStruct(COUNT_SHAPE, F32)],
        name="moe_tables",
    )(counts)


def _dest_kernel(mrow_ref, start_ref, dest_ref, *, tiles):
    start = jnp.concatenate([start_ref[...]] * (TILE // META_W), axis=1)
    cls_ids = lax.broadcasted_iota(jnp.int32, (N_CLASS_IDS, TILE), 0)
    for t in range(tiles):
        cls = mrow_ref[t, 2:3, :].astype(jnp.int32)
        first = jnp.sum(jnp.where(cls_ids == cls, start, 0.0), axis=0, keepdims=True)
        dest_ref[t] = (first + mrow_ref[t, 3:4, :]).astype(jnp.int32)


def _dest(mrow, start):
    n_tiles = mrow.shape[0]
    tiles = math.gcd(n_tiles, DEST_TILES)
    return pl.pallas_call(
        functools.partial(_dest_kernel, tiles=tiles),
        grid=(n_tiles // tiles,),
        in_specs=[pl.BlockSpec((tiles, 8, TILE), lambda i: (i, 0, 0)),
                  pl.BlockSpec(COUNT_SHAPE, lambda i: (0, 0))],
        out_specs=pl.BlockSpec((tiles, 1, TILE), lambda i: (i, 0, 0)),
        out_shape=jax.ShapeDtypeStruct((n_tiles, 1, TILE), jnp.int32),
        compiler_params=pltpu.CompilerParams(dimension_semantics=("arbitrary",)),
        name="moe_dest",
    )(mrow, start)


def _sc_mesh():
    return plsc.VectorSubcoreMesh(core_axis_name="core", subcore_axis_name="subcore")


def _col_chunks(width):
    chunks, c = [], 0
    while c < width:
        w = min(SC_COLS, width - c)
        assert c % w == 0
        chunks.append((c // w, w))
        c += w
    return chunks


_SC_PARAMS = dict(core_axis_name=("core", "subcore"), dimension_semantics=(pltpu.PARALLEL,))


def _sc_scatter_rows(idx, src, n_out):
    n, d = src.shape

    @pl.kernel(out_type=jax.ShapeDtypeStruct((n_out, d), src.dtype), mesh=_sc_mesh(), scratch_types=[],
               compiler_params=pltpu.CompilerParams(use_tc_tiling_on_sc=True))
    def scatter(src_hbm, idx_hbm, out_hbm):
        for cb, cw in _col_chunks(d):
            def body(rows_vmem, idx_vmem, cb=cb, cw=cw):
                pltpu.sync_copy(rows_vmem, out_hbm.at[:, pl.ds(cb * cw, cw)].at[idx_vmem.at[0]])

            pltpu.emit_pipeline(
                body, grid=(n // SC_WINDOW,),
                in_specs=[pl.BlockSpec((SC_WINDOW, cw), lambda i, cb=cb: (i, cb)),
                          pl.BlockSpec((1, SC_WINDOW), lambda i: (0, i))],
                out_specs=[], **_SC_PARAMS)(src_hbm, idx_hbm)

    return scatter(src, idx)


def _sc_gather_rows(idx, src):
    n, d = idx.shape[1], src.shape[1]

    @pl.kernel(out_type=jax.ShapeDtypeStruct((n, d), src.dtype), mesh=_sc_mesh(), scratch_types=[],
               compiler_params=pltpu.CompilerParams(use_tc_tiling_on_sc=True))
    def gather(src_hbm, idx_hbm, out_hbm):
        for cb, cw in _col_chunks(d):
            def body(idx_vmem, rows_vmem, cb=cb, cw=cw):
                pltpu.sync_copy(src_hbm.at[:, pl.ds(cb * cw, cw)].at[idx_vmem.at[0]], rows_vmem)

            pltpu.emit_pipeline(
                body, grid=(n // SC_WINDOW,),
                in_specs=[pl.BlockSpec((1, SC_WINDOW), lambda i: (0, i))],
                out_specs=[pl.BlockSpec((SC_WINDOW, cw), lambda i, cb=cb: (i, cb))],
                **_SC_PARAMS)(idx_hbm, out_hbm)

    return gather(src, idx)


def _expert_kernel(grp_ref, elo_ref, ehi_ref, bidx_ref, nvalid_ref, tot_ref,
                   xb_ref, w1_hbm, w3_hbm, w2_hbm, g_ref, b_ref, o_ref,
                   pre_ref, w1_ref, w3_ref, w2_ref, w_sem, state_ref, ready_ref):
    del bidx_ref
    j = pl.program_id(0)
    total = tot_ref[0]
    n_chunks = 3 * EXPERTS_PER_GROUP

    def start_chunk(grp, slot, c):
        e = c // 3
        for m, (hbm, buf) in enumerate(((w1_hbm, w1_ref), (w3_hbm, w3_ref), (w2_hbm, w2_ref))):
            @pl.when(c % 3 == m)
            def _(hbm=hbm, buf=buf):
                pltpu.make_async_copy(hbm.at[grp * EXPERTS_PER_GROUP + e], buf.at[slot, e], w_sem.at[slot, e]).start()

    def wait_chunk(slot, e):
        pltpu.make_async_copy(w1_hbm.at[0], w1_ref.at[slot, 0], w_sem.at[slot, e]).wait()

    def wait_issued(slot, count):
        def body(c, carry):
            wait_chunk(slot, c // 3)
            return carry
        lax.fori_loop(0, count, body, 0)

    def start_chunks(grp, slot, first, last):
        def body(c, carry):
            start_chunk(grp, slot, c)
            return carry
        lax.fori_loop(first, last, body, 0)

    def need_expert(slot, e):
        @pl.when(ready_ref[slot * EXPERTS_PER_GROUP + e] == 0)
        def _():
            for _ in range(3):
                wait_chunk(slot, e)
            ready_ref[slot * EXPERTS_PER_GROUP + e] = 1

    def need_all_experts(slot):
        def body(e, carry):
            need_expert(slot, e)
            return carry
        lax.fori_loop(0, EXPERTS_PER_GROUP, body, 0)

    def mark_unready(slot):
        def body(e, carry):
            ready_ref[slot * EXPERTS_PER_GROUP + e] = 0
            return carry
        lax.fori_loop(0, EXPERTS_PER_GROUP, body, 0)

    @pl.when(j == 0)
    def _():
        pre_ref[...] = jnp.zeros_like(pre_ref)
        state_ref[0] = 1
        state_ref[1] = -1
        state_ref[2] = 0
        def body(k, carry):
            ready_ref[k] = 1
            return carry
        lax.fori_loop(0, 2 * EXPERTS_PER_GROUP, body, 0)

    def finish_previous():
        o_ref[...] = _layer_norm(pre_ref[...], g_ref[...], b_ref[...])

    def hidden(half_a, b):
        return ((half_a * (jnp.tanh(half_a) + 1.0)) * b).astype(BF16)

    def run_block(rows):
        n_valid = nvalid_ref[j]
        x = jnp.where(lax.broadcasted_iota(jnp.int32, (rows, D_MODEL), 0) < n_valid, xb_ref[0:rows, 0:D_MODEL], 0.0)
        xh = x.astype(BF16)
        e_lo = elo_ref[j]
        e_hi = ehi_ref[j]
        live = lax.broadcasted_iota(jnp.int32, (rows, 1), 0) < n_valid
        w_lo = jnp.where(live, xb_ref[0:rows, D_MODEL:D_MODEL + 1], 0.0)
        w_hi = jnp.where(live, xb_ref[0:rows, D_MODEL + 1:D_MODEL + 2], 0.0)
        slot = state_ref[0]
        need_expert(slot, e_lo)
        need_expert(slot, e_hi)
        half_a_lo = _dot(xh, w1_ref[slot, e_lo])
        b_lo = _dot(xh, w3_ref[slot, e_lo])
        finish_previous()
        half_a_hi = _dot(xh, w1_ref[slot, e_hi])
        b_hi = _dot(xh, w3_ref[slot, e_hi])
        y_lo = _dot(hidden(half_a_lo, b_lo), w2_ref[slot, e_lo])
        partial = DN_ALPHA * x + w_lo * y_lo
        y_hi = _dot(hidden(half_a_hi, b_hi), w2_ref[slot, e_hi])
        pre_ref[0:rows, :] = partial + w_hi * y_hi

    @pl.when(j < total)
    def _():
        grp = grp_ref[j]

        @pl.when(jnp.logical_or(j == 0, grp != grp_ref[jnp.maximum(j - 1, 0)]))
        def _():
            slot = 1 - state_ref[0]
            copying = state_ref[1]

            @pl.when(copying != grp)
            def _():
                @pl.when(copying >= 0)
                def _():
                    wait_issued(slot, state_ref[2])
                state_ref[2] = 0

            @pl.when(state_ref[2] == 0)
            def _():
                mark_unready(slot)

            start_chunks(grp, slot, state_ref[2], n_chunks)
            need_all_experts(1 - slot)
            state_ref[0] = slot
            state_ref[1] = jnp.where(grp + 1 < N_EXPERT_GROUPS, grp + 1, -1)
            state_ref[2] = 0

        @pl.when(jnp.logical_and(state_ref[1] >= 0, state_ref[2] < n_chunks))
        def _():
            @pl.when(state_ref[2] == 0)
            def _():
                mark_unready(1 - state_ref[0])
            start_chunk(state_ref[1], 1 - state_ref[0], state_ref[2])
            state_ref[2] = state_ref[2] + 1

    half_full = nvalid_ref[jnp.minimum(j, total - 1)] <= MOE_BLOCK // 2

    @pl.when(jnp.logical_and(j < total, jnp.logical_not(half_full)))
    def _():
        run_block(MOE_BLOCK)

    @pl.when(jnp.logical_and(j < total, half_full))
    def _():
        run_block(MOE_BLOCK // 2)

    @pl.when(j == total)
    def _():
        finish_previous()

        need_all_experts(state_ref[0])

        @pl.when(state_ref[1] >= 0)
        def _():
            wait_issued(1 - state_ref[0], state_ref[2])
            state_ref[1] = -1


def _experts(tab, xb, lw):
    n_blocks = xb.shape[0] // MOE_BLOCK
    hbm = pl.BlockSpec(memory_space=pl.ANY)
    group_up = (2, EXPERTS_PER_GROUP, D_MODEL, D_EXPERT)
    group_down = (2, EXPERTS_PER_GROUP, D_EXPERT, D_MODEL)
    vec = pl.BlockSpec((1, D_MODEL), lambda j, g, lo, hi, bi, nv, tot: (0, 0))
    grid_spec = pltpu.PrefetchScalarGridSpec(
        num_scalar_prefetch=6,
        grid=(n_blocks + 1,),
        in_specs=[pl.BlockSpec((MOE_BLOCK, ROW_W), lambda j, g, lo, hi, bi, nv, tot: (bi[j], 0)),
                  hbm, hbm, hbm, vec, vec],
        out_specs=pl.BlockSpec((MOE_BLOCK, D_MODEL),
                               lambda j, g, lo, hi, bi, nv, tot: (jnp.maximum(jnp.minimum(j, tot[0]) - 1, 0), 0)),
        scratch_shapes=[pltpu.VMEM((MOE_BLOCK, D_MODEL), F32),
                        pltpu.VMEM(group_up, BF16), pltpu.VMEM(group_up, BF16), pltpu.VMEM(group_down, BF16),
                        pltpu.SemaphoreType.DMA((2, EXPERTS_PER_GROUP)), pltpu.SMEM((3,), jnp.int32),
                        pltpu.SMEM((2 * EXPERTS_PER_GROUP,), jnp.int32)],
    )
    return pl.pallas_call(
        _expert_kernel,
        grid_spec=grid_spec,
        out_shape=jax.ShapeDtypeStruct((n_blocks * MOE_BLOCK, D_MODEL), F32),
        compiler_params=pltpu.CompilerParams(
            dimension_semantics=("arbitrary",), vmem_limit_bytes=EXPERT_VMEM_LIMIT),
        name="moe_experts",
    )(tab[5], tab[0], tab[1], tab[2], tab[4], tab[3, 0:1], xb,
      lw["w1"], lw["w3"], lw["w2"], lw["ln2_g"], lw["ln2_b"])


def _layer_weights(p, l):
    row = lambda a: a[l].reshape(1, -1).astype(F32)
    w_r = jnp.zeros((META_W, D_MODEL), F32)
    w_r = w_r.at[0:N_EXPERT_GROUPS].set(p["w_rg"][l].T).at[8:8 + N_EXPERTS].set(p["w_re"][l].T)
    b_r = jnp.zeros((META_W, 1), F32)
    b_r = b_r.at[0:N_EXPERT_GROUPS, 0].set(p["b_rg"][l]).at[N_EXPERT_GROUPS:8, 0].set(NEG_BIG)
    b_r = b_r.at[8:8 + N_EXPERTS, 0].set(p["b_re"][l])
    t = np.arange(TILE)
    gate_half = jnp.asarray(np.where(np.arange(p["w_in"].shape[-1]) >= OFF_G, 0.5, 1.0), F32)
    return {
        "w_in": (p["w_in"][l] * gate_half).astype(BF16),
        "b_in": row(p["b_in"]) * gate_half,
        "w_b": p["w_in"][l][:, OFF_B:OFF_B + BRANCH_W].astype(BF16),
        "b_b": p["b_in"][l][OFF_B:OFF_B + BRANCH_W].reshape(1, -1),
        "pool_w": jax.scipy.linalg.block_diag(*[p["pool_w"][l][g] for g in range(N_GROUPS)]).astype(BF16),
        "pool_scale": row(p["pool_scale"]),
        "conv_w": p["conv_w"][l],
        "conv_b": row(p["conv_b"]),
        "sgu_ln_g": row(p["sgu_ln_g"]),
        "sgu_ln_b": row(p["sgu_ln_b"]),
        "sgu_w": p["sgu_w"][l].astype(BF16),
        "sgu_b": jnp.repeat(p["sgu_b"][l].T, GROUP_W, axis=1),
        "branch_proj": (0.5 * p["branch_proj"][l]).astype(BF16),
        "w_out": p["w_out"][l].astype(BF16),
        "ln1_g": row(p["ln1_g"]),
        "ln1_b": row(p["ln1_b"]),
        "w_r": w_r,
        "b_r": b_r,
        "tri": jnp.asarray((t[:, None] < t[None, :]).astype(np.float32)).astype(BF16),
        "w1": (0.5 * p["w1"][l]).astype(BF16),
        "w3": p["w3"][l].astype(BF16),
        "w2": p["w2"][l].astype(BF16),
        "ln2_g": row(p["ln2_g"]),
        "ln2_b": row(p["ln2_b"]),
    }


def _encoder_layer(xs, seqs, lw):
    zero_counts = jnp.zeros(COUNT_SHAPE, F32)
    routed = []
    for x, seq in zip(xs, seqs):
        n_blocks = x.shape[0] // MOE_BLOCK + N_REACHABLE_CLASSES
        nbp = -(-(n_blocks + 1) // META_W) * META_W
        y_b = _fourier_mixer(x, lw["w_b"], lw["b_b"], seq)
        x1e, mrow, counts = _mixer(x, y_b, zero_counts, lw, seq)
        tab, start = _tables(counts, nbp)
        dest = _dest(mrow, start).reshape(1, -1)
        routed.append((tab, dest, _sc_scatter_rows(dest, x1e, n_blocks * MOE_BLOCK)))
    return [_sc_gather_rows(dest, _experts(tab, xb, lw)) for tab, dest, xb in routed]


def kernel(x_prompt, x_sample, w_in, b_in, pool_w, pool_scale, conv_w, conv_b, sgu_ln_g, sgu_ln_b,
           sgu_w, sgu_b, branch_proj, w_out, ln1_g, ln1_b, w_rg, b_rg, w_re, b_re, w1, w3, w2,
           ln2_g, ln2_b):
    p = dict(w_in=w_in, b_in=b_in, pool_w=pool_w, pool_scale=pool_scale, conv_w=conv_w, conv_b=conv_b,
             sgu_ln_g=sgu_ln_g, sgu_ln_b=sgu_ln_b, sgu_w=sgu_w, sgu_b=sgu_b, branch_proj=branch_proj,
             w_out=w_out, ln1_g=ln1_g, ln1_b=ln1_b, w_rg=w_rg, b_rg=b_rg, w_re=w_re, b_re=b_re,
             w1=w1, w3=w3, w2=w2, ln2_g=ln2_g, ln2_b=ln2_b)
    shapes = (x_prompt.shape, x_sample.shape)
    seqs = [s[1] for s in shapes]
    xs = [x_prompt.reshape(-1, D_MODEL), x_sample.reshape(-1, D_MODEL)]
    for l in range(w_in.shape[0]):
        xs = _encoder_layer(xs, seqs, _layer_weights(p, l))
    return tuple(x.reshape(s) for x, s in zip(xs, shapes))
```
